```python
import math
import jax, jax.numpy as jnp
from jax import lax
import numpy as np

D_MODEL = 2048
BATCH = 1
SEQ = 16384
DEPTH = 4

N_MIXERS = 2
NSA_HEADS = 16
NSA_GROUPS = 4
NSA_HEAD_DIM = 128
NSA_REP = NSA_HEADS // NSA_GROUPS
CMP_LEN = 32
CMP_STRIDE = 16
SEL_BLOCK = 64
N_SELECT = 16
WINDOW = 512
Q_BLOCK = 128
NSA_IN = NSA_HEADS * NSA_HEAD_DIM + 6 * NSA_GROUPS * NSA_HEAD_DIM + 3 * NSA_HEADS
ML_HEADS = 8
ML_V_DIM = D_MODEL // ML_HEADS
ML_QK_DIM = ML_V_DIM // 2
ML_CHUNK = 64
ML_IN = 2 * ML_HEADS * ML_QK_DIM + 2 * ML_HEADS * ML_V_DIM + 2 * ML_HEADS
D_FF = -(-8 * D_MODEL // (3 * 256)) * 256
EPS = 1e-6
NEG_INIT = -1e30

kernel_name = 'hybrid_nsa_mlstm_swiglu_trunk'


def rms_norm(x, g):
    x32 = x.astype(jnp.float32)
    y = x32 * lax.rsqrt(jnp.mean(x32 * x32, axis=-1, keepdims=True) + EPS)
    return (y * g.astype(jnp.float32)).astype(x.dtype)


def masked_softmax(s, mask):
    s = jnp.where(mask, s.astype(jnp.float32), -jnp.inf)
    m = jnp.max(s, axis=-1, keepdims=True)
    m = jnp.where(jnp.isfinite(m), m, 0.0)
    p = jnp.exp(s - m)
    return p / jnp.maximum(jnp.sum(p, axis=-1, keepdims=True), 1e-30)


def compress_kv(kv, pos, w1, b1, w2):
    B, S, G, dh = kv.shape
    n_cmp = (S - CMP_LEN) // CMP_STRIDE + 1
    idx = jnp.arange(n_cmp)[:, None] * CMP_STRIDE + jnp.arange(CMP_LEN)[None, :]
    blocks = kv[:, idx] + pos[None, None, :, None, :]
    blocks = jnp.transpose(blocks, (0, 1, 3, 2, 4)).reshape(B, n_cmp, G, CMP_LEN * dh)
    hdn = jax.nn.gelu(blocks @ w1 + b1)
    return hdn @ w2


def nsa_mixer(h, w_in, b_gate, q_g, k_g, cmp_pos, cmp_w1, cmp_b1, cmp_w2, w_out):
    B, S, _ = h.shape
    G, R, dh = NSA_GROUPS, NSA_REP, NSA_HEAD_DIM
    qd, kvd = NSA_HEADS * dh, G * dh
    proj = h @ w_in
    q = rms_norm(proj[..., :qd].reshape(B, S, G, R, dh), q_g)
    kv = proj[..., qd:qd + 6 * kvd].reshape(B, S, 6, G, dh)
    gates = jax.nn.sigmoid(proj[..., qd + 6 * kvd:] + b_gate).reshape(B, S, G, R, 3)
    k_cmp = rms_norm(compress_kv(kv[:, :, 0], cmp_pos[0], cmp_w1[0], cmp_b1[0], cmp_w2[0]), k_g[0])
    v_cmp = compress_kv(kv[:, :, 1], cmp_pos[1], cmp_w1[1], cmp_b1[1], cmp_w2[1])
    k_sel = rms_norm(kv[:, :, 2], k_g[1])
    v_sel = kv[:, :, 3]
    k_win = rms_norm(kv[:, :, 4], k_g[2])
    v_win = kv[:, :, 5]

    n_cmp = k_cmp.shape[1]
    n_sel = S // SEL_BLOCK
    k_top = min(N_SELECT, n_sel)
    cmp_start = jnp.arange(n_cmp) * CMP_STRIDE
    cmp_end = cmp_start + CMP_LEN - 1
    sel_start = jnp.arange(n_sel) * SEL_BLOCK
    overlap = ((cmp_start[:, None] < sel_start[None, :] + SEL_BLOCK)
               & (cmp_start[:, None] + CMP_LEN > sel_start[None, :])).astype(jnp.float32)
    ks_blk = jnp.transpose(k_sel.reshape(B, n_sel, SEL_BLOCK, G, dh), (0, 3, 1, 2, 4))
    vs_blk = jnp.transpose(v_sel.reshape(B, n_sel, SEL_BLOCK, G, dh), (0, 3, 1, 2, 4))
    kw_pad = jnp.pad(k_win, ((0, 0), (WINDOW, 0), (0, 0), (0, 0)))
    vw_pad = jnp.pad(v_win, ((0, 0), (WINDOW, 0), (0, 0), (0, 0)))
    scale = dh ** -0.5
    bi = jnp.arange(B)[:, None, None, None]
    gi = jnp.arange(G)[None, :, None, None]
    jj = jnp.arange(n_sel)

    def block(qb):
        s0 = qb * Q_BLOCK
        t = s0 + jnp.arange(Q_BLOCK)
        qx = lax.dynamic_slice_in_dim(q, s0, Q_BLOCK, axis=1)
        gx = lax.dynamic_slice_in_dim(gates, s0, Q_BLOCK, axis=1)
        s = jnp.einsum('bqgrd,bcgd->bgrqc', qx, k_cmp) * scale
        p_cmp = masked_softmax(s, cmp_end[None, :] <= t[:, None])
        o_cmp = jnp.einsum('bgrqc,bcgd->bqgrd', p_cmp.astype(v_cmp.dtype), v_cmp)
        imp = jnp.einsum('bgrqc,cj->bgqj', p_cmp, overlap)
        cur = t // SEL_BLOCK
        valid = jj[None, :] <= cur[:, None]
        forced = (jj[None, :] == 0) | (jj[None, :] == cur[:, None]) | (jj[None, :] == cur[:, None] - 1)
        score = jnp.where(valid, jnp.where(forced, jnp.inf, imp), -jnp.inf)
        top_val, top_idx = lax.top_k(score, k_top)
        blk_ok = top_val > -jnp.inf
        kg = ks_blk[bi, gi, top_idx]
        vg = vs_blk[bi, gi, top_idx]
        s = jnp.einsum('bqgrd,bgqknd->bgrqkn', qx, kg) * scale
        key_pos = top_idx[..., None] * SEL_BLOCK + jnp.arange(SEL_BLOCK)
        smask = blk_ok[..., None] & (key_pos <= t[None, None, :, None, None])
        nk = k_top * SEL_BLOCK
        p = masked_softmax(s.reshape(B, G, R, Q_BLOCK, nk), smask.reshape(B, G, 1, Q_BLOCK, nk))
        o_sel = jnp.einsum('bgrqm,bgqmd->bqgrd', p.astype(vg.dtype), vg.reshape(B, G, Q_BLOCK, nk, dh))
        kw = lax.dynamic_slice_in_dim(kw_pad, s0, WINDOW + Q_BLOCK, axis=1)
        vw = lax.dynamic_slice_in_dim(vw_pad, s0, WINDOW + Q_BLOCK, axis=1)
        pos = s0 - WINDOW + jnp.arange(WINDOW + Q_BLOCK)
        wmask = (pos[None, :] >= 0) & (pos[None, :] <= t[:, None]) & (t[:, None] - pos[None, :] < WINDOW)
        s = jnp.einsum('bqgrd,bkgd->bgrqk', qx, kw) * scale
        p = masked_softmax(s, wmask)
        o_win = jnp.einsum('bgrqk,bkgd->bqgrd', p.astype(vw.dtype), vw)
        return gx[..., 0:1] * o_cmp + gx[..., 1:2] * o_sel + gx[..., 2:3] * o_win

    out = lax.map(block, jnp.arange(S // Q_BLOCK))
    out = jnp.moveaxis(out, 0, 1).reshape(B, S, NSA_HEADS * dh)
    return out @ w_out


def mlstm_mixer(h, w_in, b_if, out_g, w_out):
    B, S, _ = h.shape
    H, dk, dv, L = ML_HEADS, ML_QK_DIM, ML_V_DIM, ML_CHUNK
    nc = S // L
    proj = (h @ w_in).astype(jnp.float32)
    cuts = [H * dk, 2 * H * dk, 2 * H * dk + H * dv, 2 * H * dk + 2 * H * dv, 2 * H * dk + 2 * H * dv + H]
    q, k, v, o, ig, fg = jnp.split(proj, cuts, axis=-1)
    b_if = b_if.astype(jnp.float32)
    ig = ig + b_if[:H]
    lf = jax.nn.log_sigmoid(fg + b_if[H:])

    def chunks(a, d):
        return jnp.transpose(a.reshape(B, nc, L, H, d), (1, 0, 3, 2, 4))

    qc = chunks(q, dk)
    kc = chunks(k * dk ** -0.5, dk)
    vc = chunks(v, dv)
    igc = jnp.transpose(ig.reshape(B, nc, L, H), (1, 0, 3, 2))
    lfc = jnp.transpose(lf.reshape(B, nc, L, H), (1, 0, 3, 2))
    tril = jnp.tril(jnp.ones((L, L), dtype=bool))

    def step(carry, xs):
        C, n, m = carry
        qx, kx, vx, ix, fx = xs
        b = jnp.cumsum(fx, axis=-1)
        D = jnp.where(tril, b[..., :, None] - b[..., None, :] + ix[..., None, :], -jnp.inf)
        m_inter = b + m[..., None]
        m_t = jnp.maximum(m_inter, jnp.max(D, axis=-1))
        A = jnp.exp(D - m_t[..., None]) * jnp.einsum('bhtd,bhsd->bhts', qx, kx)
        dec = jnp.exp(m_inter - m_t)
        num = jnp.einsum('bhts,bhsv->bhtv', A, vx) + dec[..., None] * jnp.einsum('bhtd,bhdv->bhtv', qx, C)
        den = jnp.sum(A, axis=-1) + dec * jnp.einsum('bhtd,bhd->bht', qx, n)
        hx = num / jnp.maximum(jnp.abs(den), jnp.exp(-m_t))[..., None]
        b_last = b[..., -1]
        g = b_last[..., None] - b + ix
        m_new = jnp.maximum(b_last + m, jnp.max(g, axis=-1))
        w = jnp.exp(g - m_new[..., None])
        cd = jnp.exp(b_last + m - m_new)
        C = cd[..., None, None] * C + jnp.einsum('bhs,bhsd,bhsv->bhdv', w, kx, vx)
        n = cd[..., None] * n + jnp.einsum('bhs,bhsd->bhd', w, kx)
        return (C, n, m_new), hx

    init = (jnp.zeros((B, H, dk, dv), jnp.float32), jnp.zeros((B, H, dk), jnp.float32),
            jnp.full((B, H), NEG_INIT, jnp.float32))
    _, hs = lax.scan(step, init, (qc, kc, vc, igc, lfc))
    hs = jnp.transpose(hs, (1, 0, 3, 2, 4)).reshape(B, S, H, dv)
    hs = rms_norm(hs, out_g.reshape(H, dv)).reshape(B, S, H * dv)
    y = jax.nn.sigmoid(o) * hs
    return y.astype(h.dtype) @ w_out


def swiglu(h, wg, wu, wd):
    return (jax.nn.silu(h @ wg) * (h @ wu)) @ wd


def setup_inputs(seed: int = 0) -> dict:
    key = jax.random.key(seed)
    ks = jax.random.split(key, 24)
    n_nsa = sum(1 for i in range(DEPTH) if i % N_MIXERS == 0)
    n_ml = sum(1 for i in range(DEPTH) if i % N_MIXERS == 1)
    dh = NSA_HEAD_DIM
    f32 = jnp.float32

    def nrm(k, shape, scale):
        return jax.random.normal(k, shape, f32) * scale

    f_bias = jnp.linspace(3.0, 6.0, ML_HEADS, dtype=f32)
    b_if = jnp.concatenate([nrm(ks[14], (n_ml, ML_HEADS), 0.1),
                            f_bias[None, :] + nrm(ks[15], (n_ml, ML_HEADS), 0.1)], axis=-1)
    return {
        'x': nrm(ks[0], (BATCH, SEQ, D_MODEL), 1.0),
        'norm_mix_g': 1.0 + nrm(ks[1], (DEPTH, D_MODEL), 0.02),
        'norm_ffn_g': 1.0 + nrm(ks[2], (DEPTH, D_MODEL), 0.02),
        'nsa_w_in': nrm(ks[3], (n_nsa, D_MODEL, NSA_IN), D_MODEL ** -0.5),
        'nsa_b_gate': nrm(ks[4], (n_nsa, 3 * NSA_HEADS), 0.1),
        'nsa_q_norm_g': 1.0 + nrm(ks[5], (n_nsa, dh), 0.02),
        'nsa_k_norm_g': 1.0 + nrm(ks[6], (n_nsa, 3, dh), 0.02),
        'nsa_cmp_pos': nrm(ks[7], (n_nsa, 2, CMP_LEN, dh), 0.1),
        'nsa_cmp_w1': nrm(ks[8], (n_nsa, 2, CMP_LEN * dh, dh), (CMP_LEN * dh) ** -0.5),
        'nsa_cmp_b1': nrm(ks[9], (n_nsa, 2, dh), 0.02),
        'nsa_cmp_w2': nrm(ks[10], (n_nsa, 2, dh, dh), dh ** -0.5),
        'nsa_w_out': nrm(ks[11], (n_nsa, NSA_HEADS * dh, D_MODEL), (NSA_HEADS * dh) ** -0.5),
        'ml_w_in': nrm(ks[12], (n_ml, D_MODEL, ML_IN), D_MODEL ** -0.5),
        'ml_b_if': b_if,
        'ml_out_norm_g': 1.0 + nrm(ks[13], (n_ml, ML_HEADS * ML_V_DIM), 0.02),
        'ml_w_out': nrm(ks[16], (n_ml, ML_HEADS * ML_V_DIM, D_MODEL), (ML_HEADS * ML_V_DIM) ** -0.5),
        'ffn_w_gate': nrm(ks[17], (DEPTH, D_MODEL, D_FF), D_MODEL ** -0.5),
        'ffn_w_up': nrm(ks[18], (DEPTH, D_MODEL, D_FF), D_MODEL ** -0.5),
        'ffn_w_down': nrm(ks[19], (DEPTH, D_FF, D_MODEL), D_FF ** -0.5),
    }


def reference(x, norm_mix_g, norm_ffn_g, nsa_w_in, nsa_b_gate, nsa_q_norm_g, nsa_k_norm_g,
              nsa_cmp_pos, nsa_cmp_w1, nsa_cmp_b1, nsa_cmp_w2, nsa_w_out,
              ml_w_in, ml_b_if, ml_out_norm_g, ml_w_out,
              ffn_w_gate, ffn_w_up, ffn_w_down):
    for i in range(DEPTH):
        j = i // N_MIXERS
        h = rms_norm(x, norm_mix_g[i])
        if i % N_MIXERS == 0:
            x = x + nsa_mixer(h, nsa_w_in[j], nsa_b_gate[j], nsa_q_norm_g[j], nsa_k_norm_g[j],
                              nsa_cmp_pos[j], nsa_cmp_w1[j], nsa_cmp_b1[j], nsa_cmp_w2[j], nsa_w_out[j])
        else:
            x = x + mlstm_mixer(h, ml_w_in[j], ml_b_if[j], ml_out_norm_g[j], ml_w_out[j])
        h = rms_norm(x, norm_ffn_g[i])
        x = x + swiglu(h, ffn_w_gate[i], ffn_w_up[i], ffn_w_down[i])
    return x
```

```python
import functools

import jax
import jax.numpy as jnp
from jax import lax
from jax.experimental import pallas as pl
from jax.experimental.pallas import tpu as pltpu

F32 = jnp.float32
BF16 = jnp.bfloat16

EPS = 1e-6
NEG_INIT = -1e30

LANE = 128
VMEM_LIMIT = 56 * 1024 * 1024

NSA_HEADS = 16
NSA_GROUPS = 4
NSA_REP = NSA_HEADS // NSA_GROUPS
DH = 128
CMP_LEN = 32
CMP_STRIDE = 16
SEL_BLOCK = 64
N_SELECT = 16
WINDOW = 512
ML_HEADS = 8
ML_DK = 128
ML_DV = 256
ML_CHUNK = 64

MASK_BIAS = -(2.0 ** 100)


def _params(*sem):
    return pltpu.CompilerParams(dimension_semantics=sem, vmem_limit_bytes=VMEM_LIMIT)


def _rms(x, g):
    ms = jnp.mean(x * x, axis=-1, keepdims=True)
    return x * lax.rsqrt(ms + EPS) * g


def _proj_kernel(x_ref, g_ref, w_ref, a_ref, b_ref, o_ref, h_scr, *, mode):
    @pl.when(pl.program_id(1) == 0)
    def _():
        h_scr[...] = _rms(x_ref[...], g_ref[...]).astype(BF16)

    y = jnp.dot(h_scr[...], w_ref[...], preferred_element_type=F32)
    if mode == "headnorm":
        for c in range(y.shape[1] // LANE):
            sl = slice(c * LANE, (c + 1) * LANE)
            yc = y[:, sl]
            ms = jnp.mean(yc * yc, axis=-1, keepdims=True)
            mult = jnp.where(b_ref[:, sl] > 0.0, lax.rsqrt(ms + EPS), 1.0) * a_ref[:, sl]
            o_ref[:, sl] = (yc * mult).astype(o_ref.dtype)
    elif mode == "scale":
        o_ref[...] = (y * a_ref[...]).astype(o_ref.dtype)
    elif mode == "bias":
        o_ref[...] = (y + a_ref[...]).astype(o_ref.dtype)
    elif mode == "bias_sigmoid":
        o_ref[...] = jax.nn.sigmoid(y + a_ref[...]).astype(o_ref.dtype)
    else:
        raise ValueError(mode)


def _proj(x, g, w, a, b, *, mode, out_dtype, tm, tn, name):
    s, d = x.shape
    n = w.shape[1]
    assert s % tm == 0 and n % tn == 0
    return pl.pallas_call(
        functools.partial(_proj_kernel, mode=mode),
        grid=(s // tm, n // tn),
        in_specs=[
            pl.BlockSpec((tm, d), lambda i, j: (i, 0)),
            pl.BlockSpec((1, d), lambda i, j: (0, 0)),
            pl.BlockSpec((d, tn), lambda i, j: (0, j)),
            pl.BlockSpec((1, tn), lambda i, j: (0, j)),
            pl.BlockSpec((1, tn), lambda i, j: (0, j)),
        ],
        out_specs=pl.BlockSpec((tm, tn), lambda i, j: (i, j)),
        out_shape=jax.ShapeDtypeStruct((s, n), out_dtype),
        scratch_shapes=[pltpu.VMEM((tm, d), BF16)],
        compiler_params=_params("parallel", "arbitrary"),
        name=name,
    )(x, g, w, a, b)


def _matmul_res_kernel(a_ref, w_ref, r_ref, o_ref):
    o_ref[...] = r_ref[...] + jnp.dot(a_ref[...], w_ref[...], preferred_element_type=F32)


def _matmul_res(a, w, res, *, tm, tn, name):
    s, k = a.shape
    n = w.shape[1]
    assert s % tm == 0 and n % tn == 0
    return pl.pallas_call(
        _matmul_res_kernel,
        grid=(s // tm, n // tn),
        in_specs=[
            pl.BlockSpec((tm, k), lambda i, j: (i, 0)),
            pl.BlockSpec((k, tn), lambda i, j: (0, j)),
            pl.BlockSpec((tm, tn), lambda i, j: (i, j)),
        ],
        out_specs=pl.BlockSpec((tm, tn), lambda i, j: (i, j)),
        out_shape=jax.ShapeDtypeStruct((s, n), F32),
        compiler_params=_params("parallel", "arbitrary"),
        name=name,
    )(a, w, res)


def _ffn_kernel(x_ref, g_ref, wg_ref, wu_ref, wd_ref, o_ref, h_scr, acc_scr):
    f = pl.program_id(1)

    @pl.when(f == 0)
    def _():
        h_scr[...] = _rms(x_ref[...], g_ref[...]).astype(BF16)
        acc_scr[...] = jnp.zeros_like(acc_scr)

    h = h_scr[...]
    gate = jnp.dot(h, wg_ref[...], preferred_element_type=F32)
    up = jnp.dot(h, wu_ref[...], preferred_element_type=F32)
    act = (gate * jax.nn.sigmoid(gate) * up).astype(BF16)
    acc_scr[...] += jnp.dot(act, wd_ref[...], preferred_element_type=F32)

    @pl.when(f == pl.num_programs(1) - 1)
    def _():
        o_ref[...] = x_ref[...] + acc_scr[...]


def _ffn(x, g, wg, wu, wd, *, tm, tf):
    s, d = x.shape
    dff = wg.shape[1]
    assert s % tm == 0 and dff % tf == 0
    return pl.pallas_call(
        _ffn_kernel,
        grid=(s // tm, dff // tf),
        in_specs=[
            pl.BlockSpec((tm, d), lambda i, f: (i, 0)),
            pl.BlockSpec((1, d), lambda i, f: (0, 0)),
            pl.BlockSpec((d, tf), lambda i, f: (0, f)),
            pl.BlockSpec((d, tf), lambda i, f: (0, f)),
            pl.BlockSpec((tf, d), lambda i, f: (f, 0)),
        ],
        out_specs=pl.BlockSpec((tm, d), lambda i, f: (i, 0)),
        out_shape=jax.ShapeDtypeStruct((s, d), F32),
        scratch_shapes=[pltpu.VMEM((tm, d), BF16), pltpu.VMEM((tm, d), F32)],
        compiler_params=_params("parallel", "arbitrary"),
        name="ffn",
    )(x, g, wg, wu, wd)


def _compress_kernel(x_ref, w1_ref, b1_ref, w2_ref, pos_ref, kg_ref, o_ref):
    half = CMP_STRIDE * DH
    x = x_ref[0, 0]
    nc = x.shape[0]
    top = jnp.dot(x, w1_ref[0, :half, :], preferred_element_type=F32)
    bot = jnp.dot(x, w1_ref[0, half:, :], preferred_element_type=F32)
    bot = pltpu.roll(bot, nc - 1, 0)
    row = lax.broadcasted_iota(jnp.int32, bot.shape, 0)
    bot = jnp.where(row == nc - 1, 0.0, bot)
    pos8 = jnp.broadcast_to(pos_ref[0], (8, 2 * half))
    posb = jnp.dot(pos8, w1_ref[0], preferred_element_type=F32)[0:1]
    hdn = jax.nn.gelu(top + bot + posb + b1_ref[0])
    y = jnp.dot(hdn.astype(BF16), w2_ref[0], preferred_element_type=F32)
    yn = _rms(y, kg_ref[...])
    o_ref[0, 0] = jnp.where(pl.program_id(0) == 0, yn, y).astype(o_ref.dtype)


def _compress(xc, w1, b1, w2, pos, kg):
    _, g, nc, k = xc.shape
    return pl.pallas_call(
        _compress_kernel,
        grid=(2, g),
        in_specs=[
            pl.BlockSpec((1, 1, nc, k), lambda s, i: (s, i, 0, 0)),
            pl.BlockSpec((1, 2 * k, DH), lambda s, i: (s, 0, 0)),
            pl.BlockSpec((1, 1, DH), lambda s, i: (s, 0, 0)),
            pl.BlockSpec((1, DH, DH), lambda s, i: (s, 0, 0)),
            pl.BlockSpec((1, 1, 2 * k), lambda s, i: (s, 0, 0)),
            pl.BlockSpec((1, DH), lambda s, i: (0, 0)),
        ],
        out_specs=pl.BlockSpec((1, 1, nc, DH), lambda s, i: (s, i, 0, 0)),
        out_shape=jax.ShapeDtypeStruct((2, g, nc, DH), BF16),
        compiler_params=_params("parallel", "parallel"),
        name="nsa_compress",
    )(xc, w1, b1, w2, pos, kg)


def _stack_heads(qb):
    return jnp.concatenate([qb[:, r * DH:(r + 1) * DH] for r in range(NSA_REP)], axis=0)


def _cmp_kernel(q_ref, kc_ref, vc_ref, ov_ref, o_ref, mn_ref, *, tq, ktop):
    qi = pl.program_id(0)
    q4 = _stack_heads(q_ref[...])
    s = lax.dot_general(q4, kc_ref[0], (((1,), (1,)), ((), ())), preferred_element_type=F32)
    row = lax.broadcasted_iota(jnp.int32, s.shape, 0)
    col = lax.broadcasted_iota(jnp.int32, s.shape, 1)
    t = qi * tq + (row & (tq - 1))
    s = jnp.where(col * CMP_STRIDE + (CMP_LEN - 1) <= t, s, -jnp.inf)
    mx = jnp.max(s, axis=-1, keepdims=True)
    mx = jnp.where(jnp.abs(mx) < jnp.inf, mx, 0.0)
    p = jnp.exp(s - mx)
    p = p / jnp.maximum(jnp.sum(p, axis=-1, keepdims=True), 1e-30)
    o = jnp.dot(p.astype(BF16), vc_ref[0], preferred_element_type=F32)
    for r in range(NSA_REP):
        o_ref[:, r * DH:(r + 1) * DH] = o[r * tq:(r + 1) * tq].astype(o_ref.dtype)

    ps = p[0:tq]
    for r in range(1, NSA_REP):
        ps = ps + p[r * tq:(r + 1) * tq]
    imp = jnp.dot(ps.astype(BF16), ov_ref[...], preferred_element_type=F32)
    jj = lax.broadcasted_iota(jnp.int32, imp.shape, 1)
    cur = (qi * tq + lax.broadcasted_iota(jnp.int32, imp.shape, 0)) // SEL_BLOCK
    forced = (jj == 0) | (jj == cur) | (jj == cur - 1)
    score = jnp.where(jj <= cur, jnp.where(forced, jnp.inf, imp), -jnp.inf)
    jjf = jj.astype(F32)
    notsel = jnp.ones(imp.shape, F32)
    for _ in range(ktop):
        top = jnp.max(score, axis=-1, keepdims=True)
        first = jnp.min(jnp.where(score == top, jjf, 1e9), axis=-1, keepdims=True)
        hit = jjf == first
        notsel = jnp.where(hit & (top > -jnp.inf), 0.0, notsel)
        score = jnp.where(hit, -jnp.inf, score)
    mn_ref[0] = notsel.astype(mn_ref.dtype)


def _cmp_attention(q, kc, vc, overlap, *, tq):
    s = q.shape[0]
    g, nc, _ = kc.shape
    nselp = overlap.shape[1]
    ktop = min(N_SELECT, s // SEL_BLOCK)
    gw = NSA_REP * DH
    return pl.pallas_call(
        functools.partial(_cmp_kernel, tq=tq, ktop=ktop),
        grid=(s // tq, g),
        in_specs=[
            pl.BlockSpec((tq, gw), lambda i, j: (i, j)),
            pl.BlockSpec((1, nc, DH), lambda i, j: (j, 0, 0)),
            pl.BlockSpec((1, nc, DH), lambda i, j: (j, 0, 0)),
            pl.BlockSpec((nc, nselp), lambda i, j: (0, 0)),
        ],
        out_specs=[
            pl.BlockSpec((tq, gw), lambda i, j: (i, j)),
            pl.BlockSpec((1, tq, nselp), lambda i, j: (j, i, 0)),
        ],
        out_shape=[
            jax.ShapeDtypeStruct((s, g * gw), BF16),
            jax.ShapeDtypeStruct((g, s, nselp), BF16),
        ],
        compiler_params=_params("parallel", "parallel"),
        name="nsa_cmp_select",
    )(q, kc, vc, overlap)


def _sel_kernel(q_ref, mn_ref, k_ref, v_ref, e_ref, o_ref, qa_scr, m_scr, l_scr, acc_scr, *, tq, tk):
    qi = pl.program_id(1)
    kv = pl.program_id(2)
    last = ((qi + 1) * tq - 1) // tk
    nhalf = qa_scr.shape[0]

    @pl.when(kv == 0)
    def _():
        q4 = _stack_heads(q_ref[...])
        mn = mn_ref[0]
        for hf in range(nhalf):
            part = mn[:, hf * LANE:(hf + 1) * LANE]
            qa_scr[hf, :, 0:DH] = q4
            qa_scr[hf, :, DH:2 * DH] = jnp.concatenate([part] * NSA_REP, axis=0)
        m_scr[...] = jnp.full_like(m_scr, -jnp.inf)
        l_scr[...] = jnp.zeros_like(l_scr)
        acc_scr[...] = jnp.zeros_like(acc_scr)

    @pl.when(kv <= last)
    def _():
        hf = (kv * tk) // (SEL_BLOCK * LANE)
        qa = qa_scr[hf]
        ka = jnp.concatenate([k_ref[...], e_ref[...]], axis=1)
        s = lax.dot_general(qa, ka, (((1,), (1,)), ((), ())), preferred_element_type=F32)
        row = lax.broadcasted_iota(jnp.int32, s.shape, 0)
        col = lax.broadcasted_iota(jnp.int32, s.shape, 1)
        t = qi * tq + (row & (tq - 1))
        s = jnp.where(kv * tk + col <= t, s, -jnp.inf)
        m_prev = m_scr[...]
        m_new = jnp.maximum(m_prev, jnp.max(s, axis=-1, keepdims=True))
        alpha = jnp.exp(m_prev - m_new)
        p = jnp.exp(s - m_new)
        l_scr[...] = alpha * l_scr[...] + jnp.sum(p, axis=-1, keepdims=True)
        acc_scr[...] = alpha * acc_scr[...] + jnp.dot(p.astype(BF16), v_ref[...], preferred_element_type=F32)
        m_scr[...] = m_new

    @pl.when(kv == last)
    def _():
        o = acc_scr[...] / jnp.maximum(l_scr[...], 1e-30)
        for r in range(NSA_REP):
            o_ref[:, r * DH:(r + 1) * DH] = o[r * tq:(r + 1) * tq].astype(o_ref.dtype)


def _sel_attention(q, notsel, kv_arr, expand, *, k_col, v_col, tq, tk):
    s = q.shape[0]
    g = notsel.shape[0]
    nselp = notsel.shape[2]
    gw = NSA_REP * DH
    nkv = s // tk

    def kvmap(col0):
        def f(gi, i, j):
            return (jnp.minimum(j, ((i + 1) * tq - 1) // tk), col0 + gi)
        return f

    return pl.pallas_call(
        functools.partial(_sel_kernel, tq=tq, tk=tk),
        grid=(g, s // tq, nkv),
        in_specs=[
            pl.BlockSpec((tq, gw), lambda gi, i, j: (i, gi)),
            pl.BlockSpec((1, tq, nselp), lambda gi, i, j: (gi, i, 0)),
            pl.BlockSpec((tk, DH), kvmap(k_col)),
            pl.BlockSpec((tk, DH), kvmap(v_col)),
            pl.BlockSpec((tk, LANE), lambda gi, i, j: (jnp.minimum(j, ((i + 1) * tq - 1) // tk), 0)),
        ],
        out_specs=pl.BlockSpec((tq, gw), lambda gi, i, j: (i, gi)),
        out_shape=jax.ShapeDtypeStruct((s, g * gw), BF16),
        scratch_shapes=[
            pltpu.VMEM((nselp // LANE, NSA_REP * tq, 2 * DH), BF16),
            pltpu.VMEM((NSA_REP * tq, 1), F32),
            pltpu.VMEM((NSA_REP * tq, 1), F32),
            pltpu.VMEM((NSA_REP * tq, DH), F32),
        ],
        compiler_params=_params("parallel", "parallel", "arbitrary"),
        name="nsa_sel_attention",
    )(q, notsel, kv_arr, kv_arr, expand)


def _win_kernel(q_ref, *refs, tq, nback):
    nblk = nback + 1
    k_refs = refs[:nblk]
    v_refs = refs[nblk:2 * nblk]
    oc_ref, os_ref, gt_ref, o_ref = refs[2 * nblk:]
    qi = pl.program_id(0)
    gi = pl.program_id(1)
    q4 = _stack_heads(q_ref[...])
    kc = jnp.concatenate([r[...] for r in k_refs], axis=0)
    vc = jnp.concatenate([r[...] for r in v_refs], axis=0)
    s = lax.dot_general(q4, kc, (((1,), (1,)), ((), ())), preferred_element_type=F32)
    row = lax.broadcasted_iota(jnp.int32, s.shape, 0)
    col = lax.broadcasted_iota(jnp.int32, s.shape, 1)
    t = qi * tq + (row & (tq - 1))
    pos = (qi - nback) * tq + col
    ok = (pos >= 0) & (pos <= t) & (t - pos < WINDOW)
    s = jnp.where(ok, s, -jnp.inf)
    mx = jnp.max(s, axis=-1, keepdims=True)
    mx = jnp.where(jnp.abs(mx) < jnp.inf, mx, 0.0)
    p = jnp.exp(s - mx)
    p = p / jnp.maximum(jnp.sum(p, axis=-1, keepdims=True), 1e-30)
    ow = jnp.dot(p.astype(BF16), vc, preferred_element_type=F32)
    gates = gt_ref[...]
    for r in range(NSA_REP):
        sl = slice(r * DH, (r + 1) * DH)
        out = (gates[:, 3 * r:3 * r + 1] * oc_ref[:, sl].astype(F32)
               + gates[:, 3 * r + 1:3 * r + 2] * os_ref[:, sl].astype(F32)
               + gates[:, 3 * r + 2:3 * r + 3] * ow[r * tq:(r + 1) * tq])
        o_ref[:, sl] = out.astype(o_ref.dtype)


def _win_attention(q, kv_arr, o_cmp, o_sel, gates, *, k_col, v_col, tq):
    s = q.shape[0]
    g = NSA_GROUPS
    gw = NSA_REP * DH
    nback = WINDOW // tq
    assert nback * tq == WINDOW

    def kvmap(col0, b):
        def f(i, gi):
            return (jnp.maximum(i - nback + b, 0), col0 + gi)
        return f

    k_specs = [pl.BlockSpec((tq, DH), kvmap(k_col, b)) for b in range(nback + 1)]
    v_specs = [pl.BlockSpec((tq, DH), kvmap(v_col, b)) for b in range(nback + 1)]
    blk = pl.BlockSpec((tq, gw), lambda i, gi: (i, gi))
    return pl.pallas_call(
        functools.partial(_win_kernel, tq=tq, nback=nback),
        grid=(s // tq, g),
        in_specs=[blk] + k_specs + v_specs + [blk, blk, pl.BlockSpec((tq, LANE), lambda i, gi: (i, gi))],
        out_specs=blk,
        out_shape=jax.ShapeDtypeStruct((s, g * gw), BF16),
        compiler_params=_params("parallel", "parallel"),
        name="nsa_win_combine",
    )(q, *([kv_arr] * (2 * (nback + 1))), o_cmp, o_sel, gates)


def _mlstm_kernel(q_ref, k_ref, v_ref, o_ref, gt_ref, og_ref, y_ref, c_scr, n_scr, m_scr):
    L = ML_CHUNK

    @pl.when(pl.program_id(0) == 0)
    def _():
        c_scr[...] = jnp.zeros_like(c_scr)
        n_scr[...] = jnp.zeros_like(n_scr)
        m_scr[...] = jnp.full_like(m_scr, NEG_INIT)

    ri = lax.broadcasted_iota(jnp.int32, (L, L), 0)
    ci = lax.broadcasted_iota(jnp.int32, (L, L), 1)
    eye = ri == ci
    tril = ci <= ri
    triu = ri <= ci
    gates = gt_ref[...]

    def to_row(col):
        return jnp.sum(jnp.where(eye, col, 0.0), axis=0, keepdims=True)

    for h in range(ML_HEADS):
        qh = q_ref[:, h * ML_DK:(h + 1) * ML_DK]
        kh = k_ref[:, h * ML_DK:(h + 1) * ML_DK]
        vh = v_ref[:, h * ML_DV:(h + 1) * ML_DV]
        ig_col = gates[:, h:h + 1]
        fg_col = gates[:, ML_HEADS + h:ML_HEADS + h + 1]
        lf_col = jnp.minimum(fg_col, 0.0) - jnp.log(1.0 + jnp.exp(-jnp.abs(fg_col)))
        lf_row = to_row(lf_col)
        ig_row = to_row(ig_col)
        b_col = jnp.sum(jnp.where(tril, lf_row, 0.0), axis=1, keepdims=True)
        b_row = jnp.sum(jnp.where(triu, lf_col, 0.0), axis=0, keepdims=True)
        m_old = m_scr[h:h + 1, 0:1]
        dmat = jnp.where(tril, b_col - b_row + ig_row, -jnp.inf)
        m_inter = b_col + m_old
        m_t = jnp.maximum(m_inter, jnp.max(dmat, axis=1, keepdims=True))
        qk = lax.dot_general(qh, kh, (((1,), (1,)), ((), ())), preferred_element_type=F32)
        a = jnp.exp(dmat - m_t) * qk
        dec = jnp.exp(m_inter - m_t)
        c_old = c_scr[h]
        n_old = n_scr[h:h + 1, :]
        num = (jnp.dot(a.astype(BF16), vh, preferred_element_type=F32)
               + dec * jnp.dot(qh, c_old.astype(BF16), preferred_element_type=F32))
        qn = jnp.sum(qh.astype(F32) * n_old, axis=1, keepdims=True)
        den = jnp.sum(a, axis=1, keepdims=True) + dec * qn
        hx = num / jnp.maximum(jnp.abs(den), jnp.exp(-m_t))

        b_last = b_col[L - 1:L, :]
        g_col = b_last - b_col + ig_col
        m_new = jnp.maximum(b_last + m_old, jnp.max(g_col, axis=0, keepdims=True))
        w_col = jnp.exp(g_col - m_new)
        cd = jnp.exp(b_last + m_old - m_new)
        kw = kh.astype(F32) * w_col
        c_scr[h] = cd * c_old + lax.dot_general(kw.astype(BF16), vh, (((0,), (0,)), ((), ())),
                                                preferred_element_type=F32)
        n_scr[h:h + 1, :] = cd * n_old + jnp.sum(kw, axis=0, keepdims=True)
        m_scr[h:h + 1, :] = jnp.broadcast_to(m_new, (1, LANE))

        sl = slice(h * ML_DV, (h + 1) * ML_DV)
        hn = _rms(hx, og_ref[:, sl])
        y_ref[:, sl] = (jax.nn.sigmoid(o_ref[:, sl].astype(F32)) * hn).astype(y_ref.dtype)


def _mlstm(qkvo, gates, out_g):
    s = qkvo.shape[0]
    L = ML_CHUNK
    wq = ML_HEADS * ML_DK
    wv = ML_HEADS * ML_DV
    return pl.pallas_call(
        _mlstm_kernel,
        grid=(s // L,),
        in_specs=[
            pl.BlockSpec((L, wq), lambda c: (c, 0)),
            pl.BlockSpec((L, wq), lambda c: (c, 1)),
            pl.BlockSpec((L, wv), lambda c: (c, 1)),
            pl.BlockSpec((L, wv), lambda c: (c, 2)),
            pl.BlockSpec((L, LANE), lambda c: (c, 0)),
            pl.BlockSpec((1, wv), lambda c: (0, 0)),
        ],
        out_specs=pl.BlockSpec((L, wv), lambda c: (c, 0)),
        out_shape=jax.ShapeDtypeStruct((s, wv), BF16),
        scratch_shapes=[
            pltpu.VMEM((ML_HEADS, ML_DK, ML_DV), F32),
            pltpu.VMEM((ML_HEADS, ML_DK), F32),
            pltpu.VMEM((ML_HEADS, LANE), F32),
        ],
        compiler_params=_params("arbitrary"),
        name="mlstm_scan",
    )(qkvo, qkvo, qkvo, qkvo, gates, out_g)


def _pad_cols(a, n):
    return jnp.pad(a, ((0, 0), (0, n - a.shape[1])))


def _nsa_layer(x, norm_g, w_in, b_gate, q_g, k_g, cmp_pos, cmp_w1, cmp_b1, cmp_w2, w_out, *, tm, tq, tk):
    s, d = x.shape
    qd = NSA_HEADS * DH
    kvd = NSA_GROUPS * DH
    norm_g = norm_g.reshape(1, d)

    w_main = w_in[:, :qd + 6 * kvd].astype(BF16)
    ones = jnp.ones((kvd,), F32)
    gain = jnp.concatenate([jnp.tile(q_g, NSA_HEADS) * DH ** -0.5, ones, ones,
                            jnp.tile(k_g[1], NSA_GROUPS), ones, jnp.tile(k_g[2], NSA_GROUPS), ones])
    flag = jnp.concatenate([jnp.ones((qd,), F32), 0 * ones, 0 * ones, ones, 0 * ones, ones, 0 * ones])
    proj = _proj(x, norm_g, w_main, gain.reshape(1, -1), flag.reshape(1, -1), mode="headnorm",
                 out_dtype=BF16, tm=tm, tn=512, name="nsa_proj")
    ngate = 3 * NSA_REP
    w_gate = w_in[:, qd + 6 * kvd:].reshape(d, NSA_GROUPS, ngate)
    w_gate = jnp.pad(w_gate, ((0, 0), (0, 0), (0, LANE - ngate))).reshape(d, NSA_GROUPS * LANE).astype(BF16)
    bias = jnp.pad(b_gate.reshape(NSA_GROUPS, ngate), ((0, 0), (0, LANE - ngate))).reshape(1, -1)
    gates = _proj(x, norm_g, w_gate, bias, bias, mode="bias_sigmoid", out_dtype=F32, tm=tm, tn=LANE,
                  name="nsa_gate_proj")

    nc = s // CMP_STRIDE
    xc = proj[:, qd:qd + 2 * kvd].reshape(nc, CMP_STRIDE, 2, NSA_GROUPS, DH)
    xc = jnp.transpose(xc, (2, 3, 0, 1, 4)).reshape(2, NSA_GROUPS, nc, CMP_STRIDE * DH)
    kvc = _compress(xc, cmp_w1.astype(BF16), cmp_b1.reshape(2, 1, DH), cmp_w2.astype(BF16),
                    cmp_pos.reshape(2, 1, CMP_LEN * DH).astype(BF16), k_g[0].reshape(1, DH))

    n_sel = s // SEL_BLOCK
    nselp = -(-n_sel // LANE) * LANE
    cstart = jnp.arange(nc) * CMP_STRIDE
    sstart = jnp.arange(nselp) * SEL_BLOCK
    overlap = ((cstart[:, None] < sstart[None, :] + SEL_BLOCK)
               & (cstart[:, None] + CMP_LEN > sstart[None, :])
               & (jnp.arange(nselp)[None, :] < n_sel)
               & (jnp.arange(nc)[:, None] < nc - 1)).astype(BF16)
    o_cmp, notsel = _cmp_attention(proj, kvc[0], kvc[1], overlap, tq=tq)

    blk = (jnp.arange(s) // SEL_BLOCK) % LANE
    expand = jnp.where(blk[:, None] == jnp.arange(LANE)[None, :], MASK_BIAS, 0.0).astype(BF16)
    col0 = qd // DH
    o_sel = _sel_attention(proj, notsel, proj, expand, k_col=col0 + 2 * NSA_GROUPS,
                           v_col=col0 + 3 * NSA_GROUPS, tq=tq, tk=tk)
    mixed = _win_attention(proj, proj, o_cmp, o_sel, gates, k_col=col0 + 4 * NSA_GROUPS,
                           v_col=col0 + 5 * NSA_GROUPS, tq=tq)
    return _matmul_res(mixed, w_out.astype(BF16), x, tm=tm, tn=512, name="nsa_out_proj")


def _mlstm_layer(x, norm_g, w_in, b_if, out_g, w_out, *, tm):
    s, d = x.shape
    norm_g = norm_g.reshape(1, d)
    wq = ML_HEADS * ML_DK
    wv = ML_HEADS * ML_DV
    nmain = 2 * wq + 2 * wv
    w_main = w_in[:, :nmain].astype(BF16)
    scale = jnp.concatenate([jnp.ones((wq,), F32), jnp.full((wq,), ML_DK ** -0.5, F32),
                             jnp.ones((2 * wv,), F32)]).reshape(1, -1)
    qkvo = _proj(x, norm_g, w_main, scale, scale, mode="scale", out_dtype=BF16, tm=tm, tn=512,
                 name="ml_proj")
    w_gate = _pad_cols(w_in[:, nmain:], LANE).astype(BF16)
    bias = _pad_cols(b_if.reshape(1, -1), LANE)
    gates = _proj(x, norm_g, w_gate, bias, bias, mode="bias", out_dtype=F32, tm=tm, tn=LANE,
                  name="ml_gate_proj")
    y = _mlstm(qkvo, gates, out_g.reshape(1, -1))
    return _matmul_res(y, w_out.astype(BF16), x, tm=tm, tn=512, name="ml_out_proj")


def _ffn_layer(x, norm_g, wg, wu, wd, *, tm):
    return _ffn(x, norm_g.reshape(1, -1), wg.astype(BF16), wu.astype(BF16), wd.astype(BF16), tm=tm, tf=512)


def kernel(x, norm_mix_g, norm_ffn_g, nsa_w_in, nsa_b_gate, nsa_q_norm_g, nsa_k_norm_g, nsa_cmp_pos,
           nsa_cmp_w1, nsa_cmp_b1, nsa_cmp_w2, nsa_w_out, ml_w_in, ml_b_if, ml_out_norm_g, ml_w_out,
           ffn_w_gate, ffn_w_up, ffn_w_down):
    b, s, d = x.shape
    depth = norm_mix_g.shape[0]
    tm = min(512, s)
    tq = 128
    tk = min(512, s)
    outs = []
    for bi in range(b):
        xb = x[bi]
        for i in range(depth):
            j = i // 2
            if i % 2 == 0:
                xb = _nsa_layer(xb, norm_mix_g[i], nsa_w_in[j], nsa_b_gate[j], nsa_q_norm_g[j],
                                nsa_k_norm_g[j], nsa_cmp_pos[j], nsa_cmp_w1[j], nsa_cmp_b1[j],
                                nsa_cmp_w2[j], nsa_w_out[j], tm=tm, tq=tq, tk=tk)
            else:
                xb = _mlstm_layer(xb, norm_mix_g[i], ml_w_in[j], ml_b_if[j], ml_out_norm_g[j],
                                  ml_w_out[j], tm=tm)
            xb = _ffn_layer(xb, norm_ffn_g[i], ffn_w_gate[i], ffn_w_up[i], ffn_w_down[i], tm=tm)
        outs.append(xb)
    return jnp.stack(outs, axis=0)
```

```python
import functools

import jax
import jax.numpy as jnp
from jax import lax
from jax.experimental import pallas as pl
from jax.experimental.pallas import tpu as pltpu

F32 = jnp.float32
BF16 = jnp.bfloat16

EPS = 1e-6
NEG_INIT = -1e30
LOG2E = 1.4426950408889634

LANE = 128
VMEM_LIMIT = 56 * 1024 * 1024

NSA_HEADS = 16
NSA_GROUPS = 4
NSA_REP = NSA_HEADS // NSA_GROUPS
DH = 128
CMP_LEN = 32
CMP_STRIDE = 16
SEL_BLOCK = 64
N_SELECT = 16
WINDOW = 512
ML_HEADS = 8
ML_DK = 128
ML_DV = 256
ML_CHUNK = 64

MASK_BIAS = -(2.0 ** 100)


def _params(*sem):
    return pltpu.CompilerParams(dimension_semantics=sem, vmem_limit_bytes=VMEM_LIMIT)


def _rms(x, g):
    ms = jnp.mean(x * x, axis=-1, keepdims=True)
    return x * lax.rsqrt(ms + EPS) * g


def _proj_kernel(x_ref, g_ref, w_ref, a_ref, b_ref, o_ref, h_scr, *, mode):
    @pl.when(pl.program_id(1) == 0)
    def _():
        h_scr[...] = _rms(x_ref[...], g_ref[...]).astype(BF16)

    y = jnp.dot(h_scr[...], w_ref[...], preferred_element_type=F32)
    if mode == "headnorm":
        for c in range(y.shape[1] // LANE):
            sl = slice(c * LANE, (c + 1) * LANE)
            yc = y[:, sl]
            ms = jnp.mean(yc * yc, axis=-1, keepdims=True)
            mult = jnp.where(b_ref[:, sl] > 0.0, lax.rsqrt(ms + EPS), 1.0) * a_ref[:, sl]
            o_ref[:, sl] = (yc * mult).astype(o_ref.dtype)
    elif mode == "scale":
        o_ref[...] = (y * a_ref[...]).astype(o_ref.dtype)
    elif mode == "bias":
        o_ref[...] = (y + a_ref[...]).astype(o_ref.dtype)
    elif mode == "bias_sigmoid":
        o_ref[...] = jax.nn.sigmoid(y + a_ref[...]).astype(o_ref.dtype)
    else:
        raise ValueError(mode)


def _proj(x, g, w, a, b, *, mode, out_dtype, tm, tn, name):
    s, d = x.shape
    n = w.shape[1]
    assert s % tm == 0 and n % tn == 0
    return pl.pallas_call(
        functools.partial(_proj_kernel, mode=mode),
        grid=(s // tm, n // tn),
        in_specs=[
            pl.BlockSpec((tm, d), lambda i, j: (i, 0)),
            pl.BlockSpec((1, d), lambda i, j: (0, 0)),
            pl.BlockSpec((d, tn), lambda i, j: (0, j)),
            pl.BlockSpec((1, tn), lambda i, j: (0, j)),
            pl.BlockSpec((1, tn), lambda i, j: (0, j)),
        ],
        out_specs=pl.BlockSpec((tm, tn), lambda i, j: (i, j)),
        out_shape=jax.ShapeDtypeStruct((s, n), out_dtype),
        scratch_shapes=[pltpu.VMEM((tm, d), BF16)],
        compiler_params=_params("parallel", "arbitrary"),
        name=name,
    )(x, g, w, a, b)


def _matmul_res_kernel(a_ref, w_ref, r_ref, o_ref):
    o_ref[...] = r_ref[...] + jnp.dot(a_ref[...], w_ref[...], preferred_element_type=F32)


def _matmul_res(a, w, res, *, tm, tn, name):
    s, k = a.shape
    n = w.shape[1]
    assert s % tm == 0 and n % tn == 0
    return pl.pallas_call(
        _matmul_res_kernel,
        grid=(s // tm, n // tn),
        in_specs=[
            pl.BlockSpec((tm, k), lambda i, j: (i, 0)),
            pl.BlockSpec((k, tn), lambda i, j: (0, j)),
            pl.BlockSpec((tm, tn), lambda i, j: (i, j)),
        ],
        out_specs=pl.BlockSpec((tm, tn), lambda i, j: (i, j)),
        out_shape=jax.ShapeDtypeStruct((s, n), F32),
        compiler_params=_params("parallel", "arbitrary"),
        name=name,
    )(a, w, res)


def _ffn_kernel(x_ref, g_ref, wg_ref, wu_ref, wd_ref, o_ref, h_scr, acc_scr):
    f = pl.program_id(1)

    @pl.when(f == 0)
    def _():
        h_scr[...] = _rms(x_ref[...], g_ref[...]).astype(BF16)
        acc_scr[...] = jnp.zeros_like(acc_scr)

    h = h_scr[...]
    gate = jnp.dot(h, wg_ref[...], preferred_element_type=F32)
    up = jnp.dot(h, wu_ref[...], preferred_element_type=F32)
    act = (gate * jax.nn.sigmoid(gate) * up).astype(BF16)
    acc_scr[...] += jnp.dot(act, wd_ref[...], preferred_element_type=F32)

    @pl.when(f == pl.num_programs(1) - 1)
    def _():
        o_ref[...] = x_ref[...] + acc_scr[...]


def _ffn(x, g, wg, wu, wd, *, tm, tf):
    s, d = x.shape
    dff = wg.shape[1]
    assert s % tm == 0 and dff % tf == 0
    return pl.pallas_call(
        _ffn_kernel,
        grid=(s // tm, dff // tf),
        in_specs=[
            pl.BlockSpec((tm, d), lambda i, f: (i, 0)),
            pl.BlockSpec((1, d), lambda i, f: (0, 0)),
            pl.BlockSpec((d, tf), lambda i, f: (0, f)),
            pl.BlockSpec((d, tf), lambda i, f: (0, f)),
            pl.BlockSpec((tf, d), lambda i, f: (f, 0)),
        ],
        out_specs=pl.BlockSpec((tm, d), lambda i, f: (i, 0)),
        out_shape=jax.ShapeDtypeStruct((s, d), F32),
        scratch_shapes=[pltpu.VMEM((tm, d), BF16), pltpu.VMEM((tm, d), F32)],
        compiler_params=_params("parallel", "arbitrary"),
        name="ffn",
    )(x, g, wg, wu, wd)


def _compress_kernel(x_ref, w1_ref, b1_ref, w2_ref, pos_ref, kg_ref, o_ref):
    half = CMP_STRIDE * DH
    x = x_ref[0, 0]
    nc = x.shape[0]
    top = jnp.dot(x, w1_ref[0, :half, :], preferred_element_type=F32)
    bot = jnp.dot(x, w1_ref[0, half:, :], preferred_element_type=F32)
    bot = pltpu.roll(bot, nc - 1, 0)
    row = lax.broadcasted_iota(jnp.int32, bot.shape, 0)
    bot = jnp.where(row == nc - 1, 0.0, bot)
    pos8 = jnp.broadcast_to(pos_ref[0], (8, 2 * half))
    posb = jnp.dot(pos8, w1_ref[0], preferred_element_type=F32)[0:1]
    hdn = jax.nn.gelu(top + bot + posb + b1_ref[0])
    y = jnp.dot(hdn.astype(BF16), w2_ref[0], preferred_element_type=F32)
    yn = _rms(y, kg_ref[...])
    o_ref[0, 0] = jnp.where(pl.program_id(0) == 0, yn, y).astype(o_ref.dtype)


def _compress(xc, w1, b1, w2, pos, kg):
    _, g, nc, k = xc.shape
    return pl.pallas_call(
        _compress_kernel,
        grid=(2, g),
        in_specs=[
            pl.BlockSpec((1, 1, nc, k), lambda s, i: (s, i, 0, 0)),
            pl.BlockSpec((1, 2 * k, DH), lambda s, i: (s, 0, 0)),
            pl.BlockSpec((1, 1, DH), lambda s, i: (s, 0, 0)),
            pl.BlockSpec((1, DH, DH), lambda s, i: (s, 0, 0)),
            pl.BlockSpec((1, 1, 2 * k), lambda s, i: (s, 0, 0)),
            pl.BlockSpec((1, DH), lambda s, i: (0, 0)),
        ],
        out_specs=pl.BlockSpec((1, 1, nc, DH), lambda s, i: (s, i, 0, 0)),
        out_shape=jax.ShapeDtypeStruct((2, g, nc, DH), BF16),
        compiler_params=_params("parallel", "parallel"),
        name="nsa_compress",
    )(xc, w1, b1, w2, pos, kg)


def _stack_heads(qb):
    return jnp.concatenate([qb[:, r * DH:(r + 1) * DH] for r in range(NSA_REP)], axis=0)


def _cmp_kernel(q_ref, kc_ref, vc_ref, ov_ref, o_ref, mn_ref, *, tq, ktop):
    qi = pl.program_id(0)
    q4 = _stack_heads(q_ref[...])
    s = lax.dot_general(q4, kc_ref[0], (((1,), (1,)), ((), ())), preferred_element_type=F32)
    row = lax.broadcasted_iota(jnp.int32, s.shape, 0)
    col = lax.broadcasted_iota(jnp.int32, s.shape, 1)
    t = qi * tq + (row & (tq - 1))
    s = jnp.where(col * CMP_STRIDE + (CMP_LEN - 1) <= t, s, -jnp.inf)
    mx = jnp.max(s, axis=-1, keepdims=True)
    mx = jnp.where(jnp.abs(mx) < jnp.inf, mx, 0.0)
    p = jnp.exp2(s - mx)
    p = p / jnp.maximum(jnp.sum(p, axis=-1, keepdims=True), 1e-30)
    o = jnp.dot(p.astype(BF16), vc_ref[0], preferred_element_type=F32)
    for r in range(NSA_REP):
        o_ref[:, r * DH:(r + 1) * DH] = o[r * tq:(r + 1) * tq].astype(o_ref.dtype)

    ps = p[0:tq]
    for r in range(1, NSA_REP):
        ps = ps + p[r * tq:(r + 1) * tq]
    imp = jnp.dot(ps.astype(BF16), ov_ref[...], preferred_element_type=F32)
    jj = lax.broadcasted_iota(jnp.int32, imp.shape, 1)
    cur = (qi * tq + lax.broadcasted_iota(jnp.int32, imp.shape, 0)) // SEL_BLOCK
    forced = (jj == 0) | (jj == cur) | (jj == cur - 1)
    score = jnp.where(jj <= cur, jnp.where(forced, jnp.inf, imp), -jnp.inf)
    jjf = jj.astype(F32)
    notsel = jnp.ones(imp.shape, F32)
    for _ in range(ktop):
        top = jnp.max(score, axis=-1, keepdims=True)
        first = jnp.min(jnp.where(score == top, jjf, 1e9), axis=-1, keepdims=True)
        hit = jjf == first
        notsel = jnp.where(hit & (top > -jnp.inf), 0.0, notsel)
        score = jnp.where(hit, -jnp.inf, score)
    mn_ref[0] = notsel.astype(mn_ref.dtype)


def _cmp_attention(q, kc, vc, overlap, *, tq):
    s = q.shape[0]
    g, nc, _ = kc.shape
    nselp = overlap.shape[1]
    ktop = min(N_SELECT, s // SEL_BLOCK)
    gw = NSA_REP * DH
    return pl.pallas_call(
        functools.partial(_cmp_kernel, tq=tq, ktop=ktop),
        grid=(s // tq, g),
        in_specs=[
            pl.BlockSpec((tq, gw), lambda i, j: (i, j)),
            pl.BlockSpec((1, nc, DH), lambda i, j: (j, 0, 0)),
            pl.BlockSpec((1, nc, DH), lambda i, j: (j, 0, 0)),
            pl.BlockSpec((nc, nselp), lambda i, j: (0, 0)),
        ],
        out_specs=[
            pl.BlockSpec((tq, gw), lambda i, j: (i, j)),
            pl.BlockSpec((1, tq, nselp), lambda i, j: (j, i, 0)),
        ],
        out_shape=[
            jax.ShapeDtypeStruct((s, g * gw), BF16),
            jax.ShapeDtypeStruct((g, s, nselp), BF16),
        ],
        compiler_params=_params("parallel", "parallel"),
        name="nsa_cmp_select",
    )(q, kc, vc, overlap)


def _sel_kernel(q_ref, mn_ref, k_ref, v_ref, e_ref, o_ref, ka_scr, va_scr, qa_scr, s_scr, m_scr, acc_scr,
                *, tq, tk):
    qi = pl.program_id(1)
    nhalf = qa_scr.shape[0]
    nper = e_ref.shape[0]

    @pl.when(qi == 0)
    def _():
        for c in range(ka_scr.shape[0] // nper):
            rows = slice(c * nper, (c + 1) * nper)
            ka_scr[rows, 0:DH] = k_ref[rows, :]
            ka_scr[rows, DH:2 * DH] = e_ref[...]
            va_scr[rows, 0:DH] = v_ref[rows, :]
            va_scr[rows, DH:2 * DH] = jnp.ones((nper, DH), BF16)

    q4 = _stack_heads(q_ref[...])
    mn = mn_ref[0]
    for hf in range(nhalf):
        part = mn[:, hf * LANE:(hf + 1) * LANE]
        qa_scr[hf, :, 0:DH] = q4
        qa_scr[hf, :, DH:2 * DH] = jnp.concatenate([part] * NSA_REP, axis=0)
    m_scr[...] = jnp.full_like(m_scr, -jnp.inf)
    acc_scr[...] = jnp.zeros_like(acc_scr)

    def scores(j, slot):
        k0 = pl.multiple_of(j * tk, tk)
        s_scr[slot] = lax.dot_general(qa_scr[k0 // (SEL_BLOCK * LANE)], ka_scr[pl.ds(k0, tk), :],
                                      (((1,), (1,)), ((), ())), preferred_element_type=F32)

    def accumulate(j, slot, masked):
        k0 = pl.multiple_of(j * tk, tk)
        s = s_scr[slot]
        if masked:
            row = lax.broadcasted_iota(jnp.int32, s.shape, 0)
            col = lax.broadcasted_iota(jnp.int32, s.shape, 1)
            s = jnp.where(k0 + col <= qi * tq + (row & (tq - 1)), s, -jnp.inf)
        m_old = m_scr[...]
        m_new = jnp.maximum(m_old, jnp.max(s, axis=-1, keepdims=True))
        p = jnp.exp2(s - m_new).astype(BF16)
        acc_scr[...] = (jnp.exp2(m_old - m_new) * acc_scr[...]
                        + jnp.dot(p, va_scr[pl.ds(k0, tk), :], preferred_element_type=F32))
        m_scr[...] = m_new

    n = (qi * tq + tq - 1) // tk + 1
    npair = (n - 1) // 2
    scores(0, 0)

    def pair(i, c):
        scores(2 * i + 1, 1)
        accumulate(2 * i, 0, False)
        scores(2 * i + 2, 0)
        accumulate(2 * i + 1, 1, False)
        return c

    lax.fori_loop(0, npair, pair, 0)

    @pl.when(n % 2 == 1)
    def _():
        accumulate(n - 1, 0, True)

    @pl.when(n % 2 == 0)
    def _():
        scores(n - 1, 1)
        accumulate(n - 2, 0, False)
        accumulate(n - 1, 1, True)

    acc = acc_scr[...]
    o = acc[:, 0:DH] / jnp.maximum(acc[:, DH:2 * DH], 1e-30)
    for r in range(NSA_REP):
        o_ref[:, r * DH:(r + 1) * DH] = o[r * tq:(r + 1) * tq].astype(o_ref.dtype)


def _sel_attention(q, notsel, kv_arr, expand, *, k_col, v_col, tq, tk):
    s = q.shape[0]
    g = notsel.shape[0]
    nselp = notsel.shape[2]
    gw = NSA_REP * DH
    nper = expand.shape[0]
    assert s % nper == 0 and nper % tk == 0 and tk % tq == 0 and s % tq == 0
    once = pl.Buffered(1)
    return pl.pallas_call(
        functools.partial(_sel_kernel, tq=tq, tk=tk),
        grid=(g, s // tq),
        in_specs=[
            pl.BlockSpec((tq, gw), lambda gi, i: (i, gi)),
            pl.BlockSpec((1, tq, nselp), lambda gi, i: (gi, i, 0)),
            pl.BlockSpec((s, DH), lambda gi, i: (0, k_col + gi), pipeline_mode=once),
            pl.BlockSpec((s, DH), lambda gi, i: (0, v_col + gi), pipeline_mode=once),
            pl.BlockSpec((nper, LANE), lambda gi, i: (0, 0), pipeline_mode=once),
        ],
        out_specs=pl.BlockSpec((tq, gw), lambda gi, i: (i, gi)),
        out_shape=jax.ShapeDtypeStruct((s, g * gw), BF16),
        scratch_shapes=[
            pltpu.VMEM((s, 2 * DH), BF16),
            pltpu.VMEM((s, 2 * DH), BF16),
            pltpu.VMEM((nselp // LANE, NSA_REP * tq, 2 * DH), BF16),
            pltpu.VMEM((2, NSA_REP * tq, tk), F32),
            pltpu.VMEM((NSA_REP * tq, 1), F32),
            pltpu.VMEM((NSA_REP * tq, 2 * DH), F32),
        ],
        compiler_params=_params("arbitrary", "arbitrary"),
        name="nsa_sel_attention",
    )(q, notsel, kv_arr, kv_arr, expand)


def _win_kernel(q_ref, *refs, tq, nback):
    nblk = nback + 1
    k_refs = refs[:nblk]
    v_refs = refs[nblk:2 * nblk]
    oc_ref, os_ref, gt_ref, o_ref = refs[2 * nblk:]
    qi = pl.program_id(0)
    gi = pl.program_id(1)
    q4 = _stack_heads(q_ref[...])
    kc = jnp.concatenate([r[...] for r in k_refs], axis=0)
    vc = jnp.concatenate([r[...] for r in v_refs], axis=0)
    s = lax.dot_general(q4, kc, (((1,), (1,)), ((), ())), preferred_element_type=F32)
    row = lax.broadcasted_iota(jnp.int32, s.shape, 0)
    col = lax.broadcasted_iota(jnp.int32, s.shape, 1)
    t = qi * tq + (row & (tq - 1))
    pos = (qi - nback) * tq + col
    ok = (pos >= 0) & (pos <= t) & (t - pos < WINDOW)
    s = jnp.where(ok, s, -jnp.inf)
    mx = jnp.max(s, axis=-1, keepdims=True)
    mx = jnp.where(jnp.abs(mx) < jnp.inf, mx, 0.0)
    p = jnp.exp2(s - mx)
    p = p / jnp.maximum(jnp.sum(p, axis=-1, keepdims=True), 1e-30)
    ow = jnp.dot(p.astype(BF16), vc, preferred_element_type=F32)
    gates = gt_ref[...]
    for r in range(NSA_REP):
        sl = slice(r * DH, (r + 1) * DH)
        out = (gates[:, 3 * r:3 * r + 1] * oc_ref[:, sl].astype(F32)
               + gates[:, 3 * r + 1:3 * r + 2] * os_ref[:, sl].astype(F32)
               + gates[:, 3 * r + 2:3 * r + 3] * ow[r * tq:(r + 1) * tq])
        o_ref[:, sl] = out.astype(o_ref.dtype)


def _win_attention(q, kv_arr, o_cmp, o_sel, gates, *, k_col, v_col, tq):
    s = q.shape[0]
    g = NSA_GROUPS
    gw = NSA_REP * DH
    nback = WINDOW // tq
    assert nback * tq == WINDOW

    def kvmap(col0, b):
        def f(i, gi):
            return (jnp.maximum(i - nback + b, 0), col0 + gi)
        return f

    k_specs = [pl.BlockSpec((tq, DH), kvmap(k_col, b)) for b in range(nback + 1)]
    v_specs = [pl.BlockSpec((tq, DH), kvmap(v_col, b)) for b in range(nback + 1)]
    blk = pl.BlockSpec((tq, gw), lambda i, gi: (i, gi))
    return pl.pallas_call(
        functools.partial(_win_kernel, tq=tq, nback=nback),
        grid=(s // tq, g),
        in_specs=[blk] + k_specs + v_specs + [blk, blk, pl.BlockSpec((tq, LANE), lambda i, gi: (i, gi))],
        out_specs=blk,
        out_shape=jax.ShapeDtypeStruct((s, g * gw), BF16),
        compiler_params=_params("parallel", "parallel"),
        name="nsa_win_combine",
    )(q, *([kv_arr] * (2 * (nback + 1))), o_cmp, o_sel, gates)


def _mlstm_kernel(q_ref, k_ref, v_ref, o_ref, gt_ref, og_ref, y_ref, c_scr, n_scr, m_scr):
    L = ML_CHUNK

    @pl.when(pl.program_id(0) == 0)
    def _():
        c_scr[...] = jnp.zeros_like(c_scr)
        n_scr[...] = jnp.zeros_like(n_scr)
        m_scr[...] = jnp.full_like(m_scr, NEG_INIT)

    ri = lax.broadcasted_iota(jnp.int32, (L, L), 0)
    ci = lax.broadcasted_iota(jnp.int32, (L, L), 1)
    eye = ri == ci
    tril = ci <= ri
    triu = ri <= ci
    gates = gt_ref[...]

    def to_row(col):
        return jnp.sum(jnp.where(eye, col, 0.0), axis=0, keepdims=True)

    for h in range(ML_HEADS):
        qh = q_ref[:, h * ML_DK:(h + 1) * ML_DK]
        kh = k_ref[:, h * ML_DK:(h + 1) * ML_DK]
        vh = v_ref[:, h * ML_DV:(h + 1) * ML_DV]
        ig_col = gates[:, h:h + 1]
        fg_col = gates[:, ML_HEADS + h:ML_HEADS + h + 1]
        lf_col = jnp.minimum(fg_col, 0.0) - jnp.log(1.0 + jnp.exp(-jnp.abs(fg_col)))
        lf_row = to_row(lf_col)
        ig_row = to_row(ig_col)
        b_col = jnp.sum(jnp.where(tril, lf_row, 0.0), axis=1, keepdims=True)
        b_row = jnp.sum(jnp.where(triu, lf_col, 0.0), axis=0, keepdims=True)
        m_old = m_scr[h:h + 1, 0:1]
        dmat = jnp.where(tril, b_col - b_row + ig_row, -jnp.inf)
        m_inter = b_col + m_old
        m_t = jnp.maximum(m_inter, jnp.max(dmat, axis=1, keepdims=True))
        qk = lax.dot_general(qh, kh, (((1,), (1,)), ((), ())), preferred_element_type=F32)
        a = jnp.exp(dmat - m_t) * qk
        dec = jnp.exp(m_inter - m_t)
        c_old = c_scr[h]
        n_old = n_scr[h:h + 1, :]
        num = (jnp.dot(a.astype(BF16), vh, preferred_element_type=F32)
               + dec * jnp.dot(qh, c_old.astype(BF16), preferred_element_type=F32))
        qn = jnp.sum(qh.astype(F32) * n_old, axis=1, keepdims=True)
        den = jnp.sum(a, axis=1, keepdims=True) + dec * qn
        hx = num / jnp.maximum(jnp.abs(den), jnp.exp(-m_t))

        b_last = b_col[L - 1:L, :]
        g_col = b_last - b_col + ig_col
        m_new = jnp.maximum(b_last + m_old, jnp.max(g_col, axis=0, keepdims=True))
        w_col = jnp.exp(g_col - m_new)
        cd = jnp.exp(b_last + m_old - m_new)
        kw = kh.astype(F32) * w_col
        c_scr[h] = cd * c_old + lax.dot_general(kw.astype(BF16), vh, (((0,), (0,)), ((), ())),
                                                preferred_element_type=F32)
        n_scr[h:h + 1, :] = cd * n_old + jnp.sum(kw, axis=0, keepdims=True)
        m_scr[h:h + 1, :] = jnp.broadcast_to(m_new, (1, LANE))

        sl = slice(h * ML_DV, (h + 1) * ML_DV)
        hn = _rms(hx, og_ref[:, sl])
        y_ref[:, sl] = (jax.nn.sigmoid(o_ref[:, sl].astype(F32)) * hn).astype(y_ref.dtype)


def _mlstm(qkvo, gates, out_g):
    s = qkvo.shape[0]
    L = ML_CHUNK
    wq = ML_HEADS * ML_DK
    wv = ML_HEADS * ML_DV
    return pl.pallas_call(
        _mlstm_kernel,
        grid=(s // L,),
        in_specs=[
            pl.BlockSpec((L, wq), lambda c: (c, 0)),
            pl.BlockSpec((L, wq), lambda c: (c, 1)),
            pl.BlockSpec((L, wv), lambda c: (c, 1)),
            pl.BlockSpec((L, wv), lambda c: (c, 2)),
            pl.BlockSpec((L, LANE), lambda c: (c, 0)),
            pl.BlockSpec((1, wv), lambda c: (0, 0)),
        ],
        out_specs=pl.BlockSpec((L, wv), lambda c: (c, 0)),
        out_shape=jax.ShapeDtypeStruct((s, wv), BF16),
        scratch_shapes=[
            pltpu.VMEM((ML_HEADS, ML_DK, ML_DV), F32),
            pltpu.VMEM((ML_HEADS, ML_DK), F32),
            pltpu.VMEM((ML_HEADS, LANE), F32),
        ],
        compiler_params=_params("arbitrary"),
        name="mlstm_scan",
    )(qkvo, qkvo, qkvo, qkvo, gates, out_g)


def _pad_cols(a, n):
    return jnp.pad(a, ((0, 0), (0, n - a.shape[1])))


def _nsa_layer(x, norm_g, w_in, b_gate, q_g, k_g, cmp_pos, cmp_w1, cmp_b1, cmp_w2, w_out, *, tm, tq, tk):
    s, d = x.shape
    qd = NSA_HEADS * DH
    kvd = NSA_GROUPS * DH
    norm_g = norm_g.reshape(1, d)

    w_main = w_in[:, :qd + 6 * kvd].astype(BF16)
    ones = jnp.ones((kvd,), F32)
    gain = jnp.concatenate([jnp.tile(q_g, NSA_HEADS) * (DH ** -0.5 * LOG2E), ones, ones,
                            jnp.tile(k_g[1], NSA_GROUPS), ones, jnp.tile(k_g[2], NSA_GROUPS), ones])
    flag = jnp.concatenate([jnp.ones((qd,), F32), 0 * ones, 0 * ones, ones, 0 * ones, ones, 0 * ones])
    proj = _proj(x, norm_g, w_main, gain.reshape(1, -1), flag.reshape(1, -1), mode="headnorm",
                 out_dtype=BF16, tm=tm, tn=512, name="nsa_proj")
    ngate = 3 * NSA_REP
    w_gate = w_in[:, qd + 6 * kvd:].reshape(d, NSA_GROUPS, ngate)
    w_gate = jnp.pad(w_gate, ((0, 0), (0, 0), (0, LANE - ngate))).reshape(d, NSA_GROUPS * LANE).astype(BF16)
    bias = jnp.pad(b_gate.reshape(NSA_GROUPS, ngate), ((0, 0), (0, LANE - ngate))).reshape(1, -1)
    gates = _proj(x, norm_g, w_gate, bias, bias, mode="bias_sigmoid", out_dtype=F32, tm=tm, tn=LANE,
                  name="nsa_gate_proj")

    nc = s // CMP_STRIDE
    xc = proj[:, qd:qd + 2 * kvd].reshape(nc, CMP_STRIDE, 2, NSA_GROUPS, DH)
    xc = jnp.transpose(xc, (2, 3, 0, 1, 4)).reshape(2, NSA_GROUPS, nc, CMP_STRIDE * DH)
    kvc = _compress(xc, cmp_w1.astype(BF16), cmp_b1.reshape(2, 1, DH), cmp_w2.astype(BF16),
                    cmp_pos.reshape(2, 1, CMP_LEN * DH).astype(BF16), k_g[0].reshape(1, DH))

    n_sel = s // SEL_BLOCK
    nselp = -(-n_sel // LANE) * LANE
    cstart = jnp.arange(nc) * CMP_STRIDE
    sstart = jnp.arange(nselp) * SEL_BLOCK
    overlap = ((cstart[:, None] < sstart[None, :] + SEL_BLOCK)
               & (cstart[:, None] + CMP_LEN > sstart[None, :])
               & (jnp.arange(nselp)[None, :] < n_sel)
               & (jnp.arange(nc)[:, None] < nc - 1)).astype(BF16)
    o_cmp, notsel = _cmp_attention(proj, kvc[0], kvc[1], overlap, tq=tq)

    blk = jnp.arange(min(s, SEL_BLOCK * LANE)) // SEL_BLOCK
    expand =jnp.where(blk[:, None] == jnp.arange(LANE)[None, :], MASK_BIAS, 0.0).astype(BF16)
    col0 = qd // DH
    o_sel = _sel_attention(proj, notsel, proj, expand, k_col=col0 + 2 * NSA_GROUPS,
                           v_col=col0 + 3 * NSA_GROUPS, tq=tq, tk=tk)
    mixed = _win_attention(proj, proj, o_cmp, o_sel, gates, k_col=col0 + 4 * NSA_GROUPS,
                           v_col=col0 + 5 * NSA_GROUPS, tq=tq)
    return _matmul_res(mixed, w_out.astype(BF16), x, tm=tm, tn=512, name="nsa_out_proj")


def _mlstm_layer(x, norm_g, w_in, b_if, out_g, w_out, *, tm):
    s, d = x.shape
    norm_g = norm_g.reshape(1, d)
    wq = ML_HEADS * ML_DK
    wv = ML_HEADS * ML_DV
    nmain = 2 * wq + 2 * wv
    w_main = w_in[:, :nmain].astype(BF16)
    scale = jnp.concatenate([jnp.ones((wq,), F32), jnp.full((wq,), ML_DK ** -0.5, F32),
                             jnp.ones((2 * wv,), F32)]).reshape(1, -1)
    qkvo = _proj(x, norm_g, w_main, scale, scale, mode="scale", out_dtype=BF16, tm=tm, tn=512,
                 name="ml_proj")
    w_gate = _pad_cols(w_in[:, nmain:], LANE).astype(BF16)
    bias = _pad_cols(b_if.reshape(1, -1), LANE)
    gates = _proj(x, norm_g, w_gate, bias, bias, mode="bias", out_dtype=F32, tm=tm, tn=LANE,
                  name="ml_gate_proj")
    y = _mlstm(qkvo, gates, out_g.reshape(1, -1))
    return _matmul_res(y, w_out.astype(BF16), x, tm=tm, tn=512, name="ml_out_proj")


def _ffn_layer(x, norm_g, wg, wu, wd, *, tm):
    return _ffn(x, norm_g.reshape(1, -1), wg.astype(BF16), wu.astype(BF16), wd.astype(BF16), tm=tm, tf=512)


def kernel(x, norm_mix_g, norm_ffn_g, nsa_w_in, nsa_b_gate, nsa_q_norm_g, nsa_k_norm_g, nsa_cmp_pos,
           nsa_cmp_w1, nsa_cmp_b1, nsa_cmp_w2, nsa_w_out, ml_w_in, ml_b_if, ml_out_norm_g, ml_w_out,
           ffn_w_gate, ffn_w_up, ffn_w_down):
    b, s, d = x.shape
    depth = norm_mix_g.shape[0]
    tm = min(512, s)
    tq = 128
    tk = min(1024, s)
    outs = []
    for bi in range(b):
        xb = x[bi]
        for i in range(depth):
            j = i // 2
            if i % 2 == 0:
                xb = _nsa_layer(xb, norm_mix_g[i], nsa_w_in[j], nsa_b_gate[j], nsa_q_norm_g[j],
                                nsa_k_norm_g[j], nsa_cmp_pos[j], nsa_cmp_w1[j], nsa_cmp_b1[j],
                                nsa_cmp_w2[j], nsa_w_out[j], tm=tm, tq=tq, tk=tk)
            else:
                xb = _mlstm_layer(xb, norm_mix_g[i], ml_w_in[j], ml_b_if[j], ml_out_norm_g[j],
                                  ml_w_out[j], tm=tm)
            xb = _ffn_layer(xb, norm_ffn_g[i], ffn_w_gate[i], ffn_w_up[i], ffn_w_down[i], tm=tm)
        outs.append(xb)
    return jnp.stack(outs, axis=0)
```

```python
import functools

import jax
import jax.numpy as jnp
from jax import lax
from jax.experimental import pallas as pl
from jax.experimental.pallas import tpu as pltpu

F32 = jnp.float32
BF16 = jnp.bfloat16

EPS = 1e-6
NEG_INIT = -1e30
LOG2E = 1.4426950408889634

LANE = 128
VMEM_LIMIT = 56 * 1024 * 1024
PROJ_CHUNK = 512

NSA_HEADS = 16
NSA_GROUPS = 4
NSA_REP = NSA_HEADS // NSA_GROUPS
DH = 128
CMP_LEN = 32
CMP_STRIDE = 16
SEL_BLOCK = 64
N_SELECT = 16
WINDOW = 512
ML_HEADS = 8
ML_DK = 128
ML_DV = 256
ML_CHUNK = 64

MASK_BIAS = -(2.0 ** 100)


def _params(*sem):
    return pltpu.CompilerParams(dimension_semantics=sem, vmem_limit_bytes=VMEM_LIMIT)


def _rms(x, g):
    ms = jnp.mean(x * x, axis=-1, keepdims=True)
    return x * lax.rsqrt(ms + EPS) * g


def _proj_kernel(x_ref, g_ref, w_ref, a_ref, b_ref, wg_ref, bg_ref, o_ref, og_ref, h_scr, *, mode, gate_mode):
    @pl.when(pl.program_id(1) == 0)
    def _():
        h = _rms(x_ref[...], g_ref[...]).astype(BF16)
        h_scr[...] = h
        gl = jnp.dot(h, wg_ref[...], preferred_element_type=F32) + bg_ref[...]
        og_ref[...] = jax.nn.sigmoid(gl) if gate_mode == "sigmoid" else gl

    h = h_scr[...]
    for c in range(o_ref.shape[1] // PROJ_CHUNK):
        cs = slice(c * PROJ_CHUNK, (c + 1) * PROJ_CHUNK)
        y = jnp.dot(h, w_ref[:, cs], preferred_element_type=F32)
        if mode == "headnorm":
            for u in range(PROJ_CHUNK // LANE):
                sl = slice(c * PROJ_CHUNK + u * LANE, c * PROJ_CHUNK + (u + 1) * LANE)
                yc = y[:, u * LANE:(u + 1) * LANE]
                ms = jnp.mean(yc * yc, axis=-1, keepdims=True)
                mult = jnp.where(b_ref[:, sl] > 0.0, lax.rsqrt(ms + EPS), 1.0) * a_ref[:, sl]
                o_ref[:, sl] = (yc * mult).astype(o_ref.dtype)
        elif mode == "scale":
            o_ref[:, cs] = (y * a_ref[:, cs]).astype(o_ref.dtype)
        else:
            raise ValueError(mode)


def _proj(x, g, w, a, b, wg, bg, *, mode, gate_mode, tm, tn, name):
    s, d = x.shape
    n = w.shape[1]
    ng = wg.shape[1]
    assert s % tm == 0 and n % tn == 0 and tn % PROJ_CHUNK == 0
    return pl.pallas_call(
        functools.partial(_proj_kernel, mode=mode, gate_mode=gate_mode),
        grid=(s // tm, n // tn),
        in_specs=[
            pl.BlockSpec((tm, d), lambda i, j: (i, 0)),
            pl.BlockSpec((1, d), lambda i, j: (0, 0)),
            pl.BlockSpec((d, tn), lambda i, j: (0, j)),
            pl.BlockSpec((1, tn), lambda i, j: (0, j)),
            pl.BlockSpec((1, tn), lambda i, j: (0, j)),
            pl.BlockSpec((d, ng), lambda i, j: (0, 0)),
            pl.BlockSpec((1, ng), lambda i, j: (0, 0)),
        ],
        out_specs=[
            pl.BlockSpec((tm, tn), lambda i, j: (i, j)),
            pl.BlockSpec((tm, ng), lambda i, j: (i, 0)),
        ],
        out_shape=[
            jax.ShapeDtypeStruct((s, n), BF16),
            jax.ShapeDtypeStruct((s, ng), F32),
        ],
        scratch_shapes=[pltpu.VMEM((tm, d), BF16)],
        compiler_params=_params("parallel", "arbitrary"),
        name=name,
    )(x, g, w, a, b, wg, bg)


def _matmul_res_kernel(a_ref, w_ref, r_ref, o_ref):
    a = a_ref[...]
    for c in range(o_ref.shape[1] // PROJ_CHUNK):
        cs = slice(c * PROJ_CHUNK, (c + 1) * PROJ_CHUNK)
        o_ref[:, cs] = r_ref[:, cs] + jnp.dot(a, w_ref[:, cs], preferred_element_type=F32)


def _matmul_res(a, w, res, *, tm, name):
    s, k = a.shape
    n = w.shape[1]
    assert s % tm == 0 and n % PROJ_CHUNK == 0
    return pl.pallas_call(
        _matmul_res_kernel,
        grid=(s // tm,),
        in_specs=[
            pl.BlockSpec((tm, k), lambda i: (i, 0)),
            pl.BlockSpec((k, n), lambda i: (0, 0), pipeline_mode=pl.Buffered(1)),
            pl.BlockSpec((tm, n), lambda i: (i, 0)),
        ],
        out_specs=pl.BlockSpec((tm, n), lambda i: (i, 0)),
        out_shape=jax.ShapeDtypeStruct((s, n), F32),
        compiler_params=_params("parallel"),
        name=name,
    )(a, w, res)


def _ffn_kernel(x_ref, g_ref, wg_ref, wu_ref, wd_ref, o_ref, h_scr, acc_scr):
    f = pl.program_id(1)

    @pl.when(f == 0)
    def _():
        h_scr[...] = _rms(x_ref[...], g_ref[...]).astype(BF16)
        acc_scr[...] = jnp.zeros_like(acc_scr)

    h = h_scr[...]
    gate = jnp.dot(h, wg_ref[...], preferred_element_type=F32)
    up = jnp.dot(h, wu_ref[...], preferred_element_type=F32)
    act = (gate * jax.nn.sigmoid(gate) * up).astype(BF16)
    acc_scr[...] += jnp.dot(act, wd_ref[...], preferred_element_type=F32)

    @pl.when(f == pl.num_programs(1) - 1)
    def _():
        o_ref[...] = x_ref[...] + acc_scr[...]


def _ffn(x, g, wg, wu, wd, *, tm, tf):
    s, d = x.shape
    dff = wg.shape[1]
    assert s % tm == 0 and dff % tf == 0
    return pl.pallas_call(
        _ffn_kernel,
        grid=(s // tm, dff // tf),
        in_specs=[
            pl.BlockSpec((tm, d), lambda i, f: (i, 0)),
            pl.BlockSpec((1, d), lambda i, f: (0, 0)),
            pl.BlockSpec((d, tf), lambda i, f: (0, f)),
            pl.BlockSpec((d, tf), lambda i, f: (0, f)),
            pl.BlockSpec((tf, d), lambda i, f: (f, 0)),
        ],
        out_specs=pl.BlockSpec((tm, d), lambda i, f: (i, 0)),
        out_shape=jax.ShapeDtypeStruct((s, d), F32),
        scratch_shapes=[pltpu.VMEM((tm, d), BF16), pltpu.VMEM((tm, d), F32)],
        compiler_params=_params("parallel", "arbitrary"),
        name="ffn",
    )(x, g, wg, wu, wd)


def _compress_kernel(x_ref, w1_ref, b1_ref, w2_ref, pos_ref, kg_ref, o_ref):
    half = CMP_STRIDE * DH
    x = x_ref[0, 0]
    nc = x.shape[0]
    top = jnp.dot(x, w1_ref[0, :half, :], preferred_element_type=F32)
    bot = jnp.dot(x, w1_ref[0, half:, :], preferred_element_type=F32)
    bot = pltpu.roll(bot, nc - 1, 0)
    row = lax.broadcasted_iota(jnp.int32, bot.shape, 0)
    bot = jnp.where(row == nc - 1, 0.0, bot)
    pos8 = jnp.broadcast_to(pos_ref[0], (8, 2 * half))
    posb = jnp.dot(pos8, w1_ref[0], preferred_element_type=F32)[0:1]
    hdn = jax.nn.gelu(top + bot + posb + b1_ref[0])
    y = jnp.dot(hdn.astype(BF16), w2_ref[0], preferred_element_type=F32)
    yn = _rms(y, kg_ref[...])
    o_ref[0, 0] = jnp.where(pl.program_id(0) == 0, yn, y).astype(o_ref.dtype)


def _compress(xc, w1, b1, w2, pos, kg):
    _, g, nc, k = xc.shape
    return pl.pallas_call(
        _compress_kernel,
        grid=(2, g),
        in_specs=[
            pl.BlockSpec((1, 1, nc, k), lambda s, i: (s, i, 0, 0)),
            pl.BlockSpec((1, 2 * k, DH), lambda s, i: (s, 0, 0)),
            pl.BlockSpec((1, 1, DH), lambda s, i: (s, 0, 0)),
            pl.BlockSpec((1, DH, DH), lambda s, i: (s, 0, 0)),
            pl.BlockSpec((1, 1, 2 * k), lambda s, i: (s, 0, 0)),
            pl.BlockSpec((1, DH), lambda s, i: (0, 0)),
        ],
        out_specs=pl.BlockSpec((1, 1, nc, DH), lambda s, i: (s, i, 0, 0)),
        out_shape=jax.ShapeDtypeStruct((2, g, nc, DH), BF16),
        compiler_params=_params("parallel", "parallel"),
        name="nsa_compress",
    )(xc, w1, b1, w2, pos, kg)


def _stack_heads(qb):
    return jnp.concatenate([qb[:, r * DH:(r + 1) * DH] for r in range(NSA_REP)], axis=0)


def _cmp_kernel(q_ref, kc_ref, vc_ref, ovt_ref, o_ref, mn_ref, imp_scr, *, tq, ktop, col_steps):
    qi = pl.program_id(0)
    gw = NSA_REP * DH

    def attend(ncols):
        for g in range(NSA_GROUPS):
            q4 = _stack_heads(q_ref[:, g * gw:(g + 1) * gw])
            s = lax.dot_general(q4, kc_ref[g, 0:ncols, :], (((1,), (1,)), ((), ())),
                                preferred_element_type=F32)
            row = lax.broadcasted_iota(jnp.int32, s.shape, 0)
            col = lax.broadcasted_iota(jnp.int32, s.shape, 1)
            t = qi * tq + (row & (tq - 1))
            s = jnp.where(col * CMP_STRIDE + (CMP_LEN - 1) <= t, s, -jnp.inf)
            mx = jnp.max(s, axis=-1, keepdims=True)
            mx = jnp.where(jnp.abs(mx) < jnp.inf, mx, 0.0)
            p = jnp.exp2(s - mx)
            p = p / jnp.maximum(jnp.sum(p, axis=-1, keepdims=True), 1e-30)
            o = jnp.dot(p.astype(BF16), vc_ref[g, 0:ncols, :], preferred_element_type=F32)
            for r in range(NSA_REP):
                o_ref[:, g * gw + r * DH:g * gw + (r + 1) * DH] = o[r * tq:(r + 1) * tq].astype(o_ref.dtype)
            ps = p[0:tq]
            for r in range(1, NSA_REP):
                ps = ps + p[r * tq:(r + 1) * tq]
            imp_scr[g] = lax.dot_general(ovt_ref[:, 0:ncols], ps.astype(BF16), (((1,), (1,)), ((), ())),
                                         preferred_element_type=F32)

    needed = ((qi + 1) * tq - CMP_LEN) // CMP_STRIDE + 1
    lo = 0
    for ncols in col_steps:
        pl.when((needed > lo) & (needed <= ncols))(functools.partial(attend, ncols))
        lo = ncols

    shape = imp_scr.shape[1:]
    jj = lax.broadcasted_iota(jnp.int32, shape, 0)
    cur = (qi * tq + lax.broadcasted_iota(jnp.int32, shape, 1)) // SEL_BLOCK
    forced = (jj == 0) | (jj == cur) | (jj == cur - 1)
    free = (jj >= 1) & (jj <= cur - 2)
    jjf = jj.astype(F32)
    scores = [jnp.where(free, imp_scr[g], -jnp.inf) for g in range(NSA_GROUPS)]
    for _ in range(ktop - 3):
        for g in range(NSA_GROUPS):
            top = jnp.max(scores[g], axis=0, keepdims=True)
            first = jnp.min(jnp.where(scores[g] == top, jjf, 1e9), axis=0, keepdims=True)
            scores[g] = jnp.where(jjf == first, -jnp.inf, scores[g])
    for g in range(NSA_GROUPS):
        picked = forced | (free & (scores[g] == -jnp.inf))
        mn_ref[g] = jnp.where(picked, 0.0, 1.0).T.astype(mn_ref.dtype)


def _cmp_attention(q, kc, vc, overlap_t, *, tq):
    s = q.shape[0]
    g, nc, _ = kc.shape
    nselp = overlap_t.shape[0]
    ktop = min(N_SELECT, s // SEL_BLOCK)
    assert ktop >= 3
    gw = NSA_REP * DH
    col_steps = tuple(range(2 * LANE, nc + 1, 2 * LANE)) if nc % (2 * LANE) == 0 else (nc,)
    return pl.pallas_call(
        functools.partial(_cmp_kernel, tq=tq, ktop=ktop, col_steps=col_steps),
        grid=(s // tq,),
        in_specs=[
            pl.BlockSpec((tq, g * gw), lambda i: (i, 0)),
            pl.BlockSpec((g, nc, DH), lambda i: (0, 0, 0)),
            pl.BlockSpec((g, nc, DH), lambda i: (0, 0, 0)),
            pl.BlockSpec((nselp, nc), lambda i: (0, 0)),
        ],
        out_specs=[
            pl.BlockSpec((tq, g * gw), lambda i: (i, 0)),
            pl.BlockSpec((g, tq, nselp), lambda i: (0, i, 0)),
        ],
        out_shape=[
            jax.ShapeDtypeStruct((s, g * gw), BF16),
            jax.ShapeDtypeStruct((g, s, nselp), BF16),
        ],
        scratch_shapes=[pltpu.VMEM((g, nselp, tq), F32)],
        compiler_params=_params("parallel"),
        name="nsa_cmp_select",
    )(q, kc, vc, overlap_t)


def _sel_kernel(q_ref, mn_ref, k_ref, v_ref, e_ref, o_ref, ka_scr, va_scr, qa_scr, s_scr, m_scr, acc_scr,
                *, tq, tk):
    qi = pl.program_id(1)
    nhalf = qa_scr.shape[0]
    nper = e_ref.shape[0]

    @pl.when(qi == 0)
    def _():
        for c in range(ka_scr.shape[0] // nper):
            rows = slice(c * nper, (c + 1) * nper)
            ka_scr[rows, 0:DH] = k_ref[rows, :]
            ka_scr[rows, DH:2 * DH] = e_ref[...]
            va_scr[rows, 0:DH] = v_ref[rows, :]
            va_scr[rows, DH:2 * DH] = jnp.ones((nper, DH), BF16)

    q4 = _stack_heads(q_ref[...])
    mn = mn_ref[0]
    for hf in range(nhalf):
        part = mn[:, hf * LANE:(hf + 1) * LANE]
        qa_scr[hf, :, 0:DH] = q4
        qa_scr[hf, :, DH:2 * DH] = jnp.concatenate([part] * NSA_REP, axis=0)
    m_scr[...] = jnp.full_like(m_scr, -jnp.inf)
    acc_scr[...] = jnp.zeros_like(acc_scr)

    def scores(j, slot):
        k0 = pl.multiple_of(j * tk, tk)
        s_scr[slot] = lax.dot_general(qa_scr[k0 // (SEL_BLOCK * LANE)], ka_scr[pl.ds(k0, tk), :],
                                      (((1,), (1,)), ((), ())), preferred_element_type=F32)

    def accumulate(j, slot, masked):
        k0 = pl.multiple_of(j * tk, tk)
        s = s_scr[slot]
        if masked:
            row = lax.broadcasted_iota(jnp.int32, s.shape, 0)
            col = lax.broadcasted_iota(jnp.int32, s.shape, 1)
            s = jnp.where(k0 + col <= qi * tq + (row & (tq - 1)), s, -jnp.inf)
        m_old = m_scr[...]
        m_new = jnp.maximum(m_old, jnp.max(s, axis=-1, keepdims=True))
        p = jnp.exp2(s - m_new).astype(BF16)
        acc_scr[...] = (jnp.exp2(m_old - m_new) * acc_scr[...]
                        + jnp.dot(p, va_scr[pl.ds(k0, tk), :], preferred_element_type=F32))
        m_scr[...] = m_new

    n = (qi * tq + tq - 1) // tk + 1
    npair = (n - 1) // 2
    scores(0, 0)

    def pair(i, c):
        scores(2 * i + 1, 1)
        accumulate(2 * i, 0, False)
        scores(2 * i + 2, 0)
        accumulate(2 * i + 1, 1, False)
        return c

    lax.fori_loop(0, npair, pair, 0)

    @pl.when(n % 2 == 1)
    def _():
        accumulate(n - 1, 0, True)

    @pl.when(n % 2 == 0)
    def _():
        scores(n - 1, 1)
        accumulate(n - 2, 0, False)
        accumulate(n - 1, 1, True)

    acc = acc_scr[...]
    o = acc[:, 0:DH] / jnp.maximum(acc[:, DH:2 * DH], 1e-30)
    for r in range(NSA_REP):
        o_ref[:, r * DH:(r + 1) * DH] = o[r * tq:(r + 1) * tq].astype(o_ref.dtype)


def _sel_attention(q, notsel, kv_arr, expand, *, k_col, v_col, tq, tk):
    s = q.shape[0]
    g = notsel.shape[0]
    nselp = notsel.shape[2]
    gw = NSA_REP * DH
    nper = expand.shape[0]
    assert s % nper == 0 and nper % tk == 0 and tk % tq == 0 and s % tq == 0
    once = pl.Buffered(1)
    return pl.pallas_call(
        functools.partial(_sel_kernel, tq=tq, tk=tk),
        grid=(g, s // tq),
        in_specs=[
            pl.BlockSpec((tq, gw), lambda gi, i: (i, gi)),
            pl.BlockSpec((1, tq, nselp), lambda gi, i: (gi, i, 0)),
            pl.BlockSpec((s, DH), lambda gi, i: (0, k_col + gi), pipeline_mode=once),
            pl.BlockSpec((s, DH), lambda gi, i: (0, v_col + gi), pipeline_mode=once),
            pl.BlockSpec((nper, LANE), lambda gi, i: (0, 0), pipeline_mode=once),
        ],
        out_specs=pl.BlockSpec((tq, gw), lambda gi, i: (i, gi)),
        out_shape=jax.ShapeDtypeStruct((s, g * gw), BF16),
        scratch_shapes=[
            pltpu.VMEM((s, 2 * DH), BF16),
            pltpu.VMEM((s, 2 * DH), BF16),
            pltpu.VMEM((nselp // LANE, NSA_REP * tq, 2 * DH), BF16),
            pltpu.VMEM((2, NSA_REP * tq, tk), F32),
            pltpu.VMEM((NSA_REP * tq, 1), F32),
            pltpu.VMEM((NSA_REP * tq, 2 * DH), F32),
        ],
        compiler_params=_params("arbitrary", "arbitrary"),
        name="nsa_sel_attention",
    )(q, notsel, kv_arr, kv_arr, expand)


def _win_kernel(q_ref, *refs, tq, nback):
    nblk = nback + 1
    k_refs = refs[:nblk]
    v_refs = refs[nblk:2 * nblk]
    oc_ref, os_ref, gt_ref, o_ref = refs[2 * nblk:]
    qi = pl.program_id(0)
    gi = pl.program_id(1)
    q4 = _stack_heads(q_ref[...])
    kc = jnp.concatenate([r[...] for r in k_refs], axis=0)
    vc = jnp.concatenate([r[...] for r in v_refs], axis=0)
    s = lax.dot_general(q4, kc, (((1,), (1,)), ((), ())), preferred_element_type=F32)
    row = lax.broadcasted_iota(jnp.int32, s.shape, 0)
    col = lax.broadcasted_iota(jnp.int32, s.shape, 1)
    t = qi * tq + (row & (tq - 1))
    pos = (qi - nback) * tq + col
    ok = (pos >= 0) & (pos <= t) & (t - pos < WINDOW)
    s = jnp.where(ok, s, -jnp.inf)
    mx = jnp.max(s, axis=-1, keepdims=True)
    mx = jnp.where(jnp.abs(mx) < jnp.inf, mx, 0.0)
    p = jnp.exp2(s - mx)
    p = p / jnp.maximum(jnp.sum(p, axis=-1, keepdims=True), 1e-30)
    ow = jnp.dot(p.astype(BF16), vc, preferred_element_type=F32)
    gates = gt_ref[...]
    for r in range(NSA_REP):
        sl = slice(r * DH, (r + 1) * DH)
        out = (gates[:, 3 * r:3 * r + 1] * oc_ref[:, sl].astype(F32)
               + gates[:, 3 * r + 1:3 * r + 2] * os_ref[:, sl].astype(F32)
               + gates[:, 3 * r + 2:3 * r + 3] * ow[r * tq:(r + 1) * tq])
        o_ref[:, sl] = out.astype(o_ref.dtype)


def _win_attention(q, kv_arr, o_cmp, o_sel, gates, *, k_col, v_col, tq):
    s = q.shape[0]
    g = NSA_GROUPS
    gw = NSA_REP * DH
    nback = WINDOW // tq
    assert nback * tq == WINDOW

    def kvmap(col0, b):
        def f(i, gi):
            return (jnp.maximum(i - nback + b, 0), col0 + gi)
        return f

    k_specs = [pl.BlockSpec((tq, DH), kvmap(k_col, b)) for b in range(nback + 1)]
    v_specs = [pl.BlockSpec((tq, DH), kvmap(v_col, b)) for b in range(nback + 1)]
    blk = pl.BlockSpec((tq, gw), lambda i, gi: (i, gi))
    return pl.pallas_call(
        functools.partial(_win_kernel, tq=tq, nback=nback),
        grid=(s // tq, g),
        in_specs=[blk] + k_specs + v_specs + [blk, blk, pl.BlockSpec((tq, LANE), lambda i, gi: (i, gi))],
        out_specs=blk,
        out_shape=jax.ShapeDtypeStruct((s, g * gw), BF16),
        compiler_params=_params("parallel", "parallel"),
        name="nsa_win_combine",
    )(q, *([kv_arr] * (2 * (nback + 1))), o_cmp, o_sel, gates)


def _mlstm_kernel(q_ref, k_ref, v_ref, o_ref, gt_ref, og_ref, y_ref, c_scr, n_scr, m_scr):
    L = ML_CHUNK

    @pl.when(pl.program_id(0) == 0)
    def _():
        c_scr[...] = jnp.zeros_like(c_scr)
        n_scr[...] = jnp.zeros_like(n_scr)
        m_scr[...] = jnp.full_like(m_scr, NEG_INIT)

    ri = lax.broadcasted_iota(jnp.int32, (L, L), 0)
    ci = lax.broadcasted_iota(jnp.int32, (L, L), 1)
    eye = ri == ci
    tril = ci <= ri
    triu = ri <= ci
    gates = gt_ref[...]

    def to_row(col):
        return jnp.sum(jnp.where(eye, col, 0.0), axis=0, keepdims=True)

    for h in range(ML_HEADS):
        qh = q_ref[:, h * ML_DK:(h + 1) * ML_DK]
        kh = k_ref[:, h * ML_DK:(h + 1) * ML_DK]
        vh = v_ref[:, h * ML_DV:(h + 1) * ML_DV]
        ig_col = gates[:, h:h + 1]
        fg_col = gates[:, ML_HEADS + h:ML_HEADS + h + 1]
        lf_col = jnp.minimum(fg_col, 0.0) - jnp.log(1.0 + jnp.exp(-jnp.abs(fg_col)))
        lf_row = to_row(lf_col)
        ig_row = to_row(ig_col)
        b_col = jnp.sum(jnp.where(tril, lf_row, 0.0), axis=1, keepdims=True)
        b_row = jnp.sum(jnp.where(triu, lf_col, 0.0), axis=0, keepdims=True)
        m_old = m_scr[h:h + 1, 0:1]
        dmat = jnp.where(tril, b_col - b_row + ig_row, -jnp.inf)
        m_inter = b_col + m_old
        m_t = jnp.maximum(m_inter, jnp.max(dmat, axis=1, keepdims=True))
        qk = lax.dot_general(qh, kh, (((1,), (1,)), ((), ())), preferred_element_type=F32)
        a = jnp.exp(dmat - m_t) * qk
        dec = jnp.exp(m_inter - m_t)
        c_old = c_scr[h]
        n_old = n_scr[h:h + 1, :]
        num = (jnp.dot(a.astype(BF16), vh, preferred_element_type=F32)
               + dec * jnp.dot(qh, c_old.astype(BF16), preferred_element_type=F32))
        qn = jnp.sum(qh.astype(F32) * n_old, axis=1, keepdims=True)
        den = jnp.sum(a, axis=1, keepdims=True) + dec * qn
        hx = num / jnp.maximum(jnp.abs(den), jnp.exp(-m_t))

        b_last = b_col[L - 1:L, :]
        g_col = b_last - b_col + ig_col
        m_new = jnp.maximum(b_last + m_old, jnp.max(g_col, axis=0, keepdims=True))
        w_col = jnp.exp(g_col - m_new)
        cd = jnp.exp(b_last + m_old - m_new)
        kw = kh.astype(F32) * w_col
        c_scr[h] = cd * c_old + lax.dot_general(kw.astype(BF16), vh, (((0,), (0,)), ((), ())),
                                                preferred_element_type=F32)
        n_scr[h:h + 1, :] = cd * n_old + jnp.sum(kw, axis=0, keepdims=True)
        m_scr[h:h + 1, :] = jnp.broadcast_to(m_new, (1, LANE))

        sl = slice(h * ML_DV, (h + 1) * ML_DV)
        hn = _rms(hx, og_ref[:, sl])
        y_ref[:, sl] = (jax.nn.sigmoid(o_ref[:, sl].astype(F32)) * hn).astype(y_ref.dtype)


def _mlstm(qkvo, gates, out_g):
    s = qkvo.shape[0]
    L = ML_CHUNK
    wq = ML_HEADS * ML_DK
    wv = ML_HEADS * ML_DV
    return pl.pallas_call(
        _mlstm_kernel,
        grid=(s // L,),
        in_specs=[
            pl.BlockSpec((L, wq), lambda c: (c, 0)),
            pl.BlockSpec((L, wq), lambda c: (c, 1)),
            pl.BlockSpec((L, wv), lambda c: (c, 1)),
            pl.BlockSpec((L, wv), lambda c: (c, 2)),
            pl.BlockSpec((L, LANE), lambda c: (c, 0)),
            pl.BlockSpec((1, wv), lambda c: (0, 0)),
        ],
        out_specs=pl.BlockSpec((L, wv), lambda c: (c, 0)),
        out_shape=jax.ShapeDtypeStruct((s, wv), BF16),
        scratch_shapes=[
            pltpu.VMEM((ML_HEADS, ML_DK, ML_DV), F32),
            pltpu.VMEM((ML_HEADS, ML_DK), F32),
            pltpu.VMEM((ML_HEADS, LANE), F32),
        ],
        compiler_params=_params("arbitrary"),
        name="mlstm_scan",
    )(qkvo, qkvo, qkvo, qkvo, gates, out_g)


def _pad_cols(a, n):
    return jnp.pad(a, ((0, 0), (0, n - a.shape[1])))


def _nsa_layer(x, norm_g, w_in, b_gate, q_g, k_g, cmp_pos, cmp_w1, cmp_b1, cmp_w2, w_out, *, tm, tq, tk):
    s, d = x.shape
    qd = NSA_HEADS * DH
    kvd = NSA_GROUPS * DH
    norm_g = norm_g.reshape(1, d)

    w_main = w_in[:, :qd + 6 * kvd].astype(BF16)
    ones = jnp.ones((kvd,), F32)
    gain = jnp.concatenate([jnp.tile(q_g, NSA_HEADS) * (DH ** -0.5 * LOG2E), ones, ones,
                            jnp.tile(k_g[1], NSA_GROUPS), ones, jnp.tile(k_g[2], NSA_GROUPS), ones])
    flag = jnp.concatenate([jnp.ones((qd,), F32), 0 * ones, 0 * ones, ones, 0 * ones, ones, 0 * ones])
    ngate = 3 * NSA_REP
    w_gate = w_in[:, qd + 6 * kvd:].reshape(d, NSA_GROUPS, ngate)
    w_gate = jnp.pad(w_gate, ((0, 0), (0, 0), (0, LANE - ngate))).reshape(d, NSA_GROUPS * LANE).astype(BF16)
    bias = jnp.pad(b_gate.reshape(NSA_GROUPS, ngate), ((0, 0), (0, LANE - ngate))).reshape(1, -1)
    proj, gates = _proj(x, norm_g, w_main, gain.reshape(1, -1), flag.reshape(1, -1), w_gate, bias,
                        mode="headnorm", gate_mode="sigmoid", tm=tm, tn=1024, name="nsa_proj")

    nc = s // CMP_STRIDE
    xc = proj[:, qd:qd + 2 * kvd].reshape(nc, CMP_STRIDE, 2, NSA_GROUPS, DH)
    xc = jnp.transpose(xc, (2, 3, 0, 1, 4)).reshape(2, NSA_GROUPS, nc, CMP_STRIDE * DH)
    kvc = _compress(xc, cmp_w1.astype(BF16), cmp_b1.reshape(2, 1, DH), cmp_w2.astype(BF16),
                    cmp_pos.reshape(2, 1, CMP_LEN * DH).astype(BF16), k_g[0].reshape(1, DH))

    n_sel = s // SEL_BLOCK
    nselp = -(-n_sel // LANE) * LANE
    cstart = jnp.arange(nc) * CMP_STRIDE
    sstart = jnp.arange(nselp) * SEL_BLOCK
    overlap_t = ((cstart[None, :] < sstart[:, None] + SEL_BLOCK)
                 & (cstart[None, :] + CMP_LEN > sstart[:, None])
                 & (jnp.arange(nselp)[:, None] < n_sel)
                 & (jnp.arange(nc)[None, :] < nc - 1)).astype(BF16)
    o_cmp, notsel = _cmp_attention(proj, kvc[0], kvc[1], overlap_t, tq=tq)

    blk = jnp.arange(min(s, SEL_BLOCK * LANE)) // SEL_BLOCK
    expand =jnp.where(blk[:, None] == jnp.arange(LANE)[None, :], MASK_BIAS, 0.0).astype(BF16)
    col0 = qd // DH
    o_sel = _sel_attention(proj, notsel, proj, expand, k_col=col0 + 2 * NSA_GROUPS,
                           v_col=col0 + 3 * NSA_GROUPS, tq=min(2 * tq, s), tk=tk)
    mixed = _win_attention(proj, proj, o_cmp, o_sel, gates, k_col=col0 + 4 * NSA_GROUPS,
                           v_col=col0 + 5 * NSA_GROUPS, tq=tq)
    return _matmul_res(mixed, w_out.astype(BF16), x, tm=tm, name="nsa_out_proj")


def _mlstm_layer(x, norm_g, w_in, b_if, out_g, w_out, *, tm):
    s, d = x.shape
    norm_g = norm_g.reshape(1, d)
    wq = ML_HEADS * ML_DK
    wv = ML_HEADS * ML_DV
    nmain = 2 * wq + 2 * wv
    w_main = w_in[:, :nmain].astype(BF16)
    scale = jnp.concatenate([jnp.ones((wq,), F32), jnp.full((wq,), ML_DK ** -0.5, F32),
                             jnp.ones((2 * wv,), F32)]).reshape(1, -1)
    w_gate = _pad_cols(w_in[:, nmain:], LANE).astype(BF16)
    bias = _pad_cols(b_if.reshape(1, -1), LANE)
    qkvo, gates = _proj(x, norm_g, w_main, scale, scale, w_gate, bias, mode="scale", gate_mode="bias",
                        tm=tm, tn=1024, name="ml_proj")
    y = _mlstm(qkvo, gates, out_g.reshape(1, -1))
    return _matmul_res(y, w_out.astype(BF16), x, tm=tm, name="ml_out_proj")


def _ffn_layer(x, norm_g, wg, wu, wd, *, tm):
    return _ffn(x, norm_g.reshape(1, -1), wg.astype(BF16), wu.astype(BF16), wd.astype(BF16), tm=tm, tf=512)


def kernel(x, norm_mix_g, norm_ffn_g, nsa_w_in, nsa_b_gate, nsa_q_norm_g, nsa_k_norm_g, nsa_cmp_pos,
           nsa_cmp_w1, nsa_cmp_b1, nsa_cmp_w2, nsa_w_out, ml_w_in, ml_b_if, ml_out_norm_g, ml_w_out,
           ffn_w_gate, ffn_w_up, ffn_w_down):
    b, s, d = x.shape
    depth = norm_mix_g.shape[0]
    tm = min(512, s)
    tq = 128
    tk = min(1024, s)
    outs = []
    for bi in range(b):
        xb = x[bi]
        for i in range(depth):
            j = i // 2
            if i % 2 == 0:
                xb = _nsa_layer(xb, norm_mix_g[i], nsa_w_in[j], nsa_b_gate[j], nsa_q_norm_g[j],
                                nsa_k_norm_g[j], nsa_cmp_pos[j], nsa_cmp_w1[j], nsa_cmp_b1[j],
                                nsa_cmp_w2[j], nsa_w_out[j], tm=tm, tq=tq, tk=tk)
            else:
                xb = _mlstm_layer(xb, norm_mix_g[i], ml_w_in[j], ml_b_if[j], ml_out_norm_g[j],
                                  ml_w_out[j], tm=tm)
            xb = _ffn_layer(xb, norm_ffn_g[i], ffn_w_gate[i], ffn_w_up[i], ffn_w_down[i], tm=tm)
        outs.append(xb)
    return jnp.stack(outs, axis=0)
```

```python
import functools

import jax
import jax.numpy as jnp
from jax import lax
from jax.experimental import pallas as pl
from jax.experimental.pallas import tpu as pltpu

F32 = jnp.float32
BF16 = jnp.bfloat16

EPS = 1e-6
NEG_INIT = -1e30
LOG2E = 1.4426950408889634

LANE = 128
VMEM_LIMIT = 56 * 1024 * 1024
PROJ_CHUNK = 512

NSA_HEADS = 16
NSA_GROUPS = 4
NSA_REP = NSA_HEADS // NSA_GROUPS
DH = 128
CMP_LEN = 32
CMP_STRIDE = 16
SEL_BLOCK = 64
N_SELECT = 16
WINDOW = 512
ML_HEADS = 8
ML_DK = 128
ML_DV = 256
ML_CHUNK = 64

MASK_BIAS = -(2.0 ** 100)


def _params(*sem):
    return pltpu.CompilerParams(dimension_semantics=sem, vmem_limit_bytes=VMEM_LIMIT)


def _rms(x, g):
    ms = jnp.mean(x * x, axis=-1, keepdims=True)
    return x * lax.rsqrt(ms + EPS) * g


def _proj_kernel(x_ref, g_ref, w_ref, a_ref, b_ref, wg_ref, bg_ref, o_ref, og_ref, h_scr, *, mode, gate_mode):
    @pl.when(pl.program_id(1) == 0)
    def _():
        h = _rms(x_ref[...], g_ref[...]).astype(BF16)
        h_scr[...] = h
        gl = jnp.dot(h, wg_ref[...], preferred_element_type=F32) + bg_ref[...]
        og_ref[...] = jax.nn.sigmoid(gl) if gate_mode == "sigmoid" else gl

    h = h_scr[...]
    for c in range(o_ref.shape[1] // PROJ_CHUNK):
        cs = slice(c * PROJ_CHUNK, (c + 1) * PROJ_CHUNK)
        y = jnp.dot(h, w_ref[:, cs], preferred_element_type=F32)
        if mode == "headnorm":
            for u in range(PROJ_CHUNK // LANE):
                sl = slice(c * PROJ_CHUNK + u * LANE, c * PROJ_CHUNK + (u + 1) * LANE)
                yc = y[:, u * LANE:(u + 1) * LANE]
                ms = jnp.mean(yc * yc, axis=-1, keepdims=True)
                mult = jnp.where(b_ref[:, sl] > 0.0, lax.rsqrt(ms + EPS), 1.0) * a_ref[:, sl]
                o_ref[:, sl] = (yc * mult).astype(o_ref.dtype)
        elif mode == "scale":
            o_ref[:, cs] = (y * a_ref[:, cs]).astype(o_ref.dtype)
        else:
            raise ValueError(mode)


def _proj(x, g, w, a, b, wg, bg, *, mode, gate_mode, tm, tn, name):
    s, d = x.shape
    n = w.shape[1]
    ng = wg.shape[1]
    assert s % tm == 0 and n % tn == 0 and tn % PROJ_CHUNK == 0
    return pl.pallas_call(
        functools.partial(_proj_kernel, mode=mode, gate_mode=gate_mode),
        grid=(s // tm, n // tn),
        in_specs=[
            pl.BlockSpec((tm, d), lambda i, j: (i, 0)),
            pl.BlockSpec((1, d), lambda i, j: (0, 0)),
            pl.BlockSpec((d, tn), lambda i, j: (0, j)),
            pl.BlockSpec((1, tn), lambda i, j: (0, j)),
            pl.BlockSpec((1, tn), lambda i, j: (0, j)),
            pl.BlockSpec((d, ng), lambda i, j: (0, 0)),
            pl.BlockSpec((1, ng), lambda i, j: (0, 0)),
        ],
        out_specs=[
            pl.BlockSpec((tm, tn), lambda i, j: (i, j)),
            pl.BlockSpec((tm, ng), lambda i, j: (i, 0)),
        ],
        out_shape=[
            jax.ShapeDtypeStruct((s, n), BF16),
            jax.ShapeDtypeStruct((s, ng), F32),
        ],
        scratch_shapes=[pltpu.VMEM((tm, d), BF16)],
        compiler_params=_params("parallel", "arbitrary"),
        name=name,
    )(x, g, w, a, b, wg, bg)


def _matmul_res_kernel(a_ref, w_ref, r_ref, o_ref):
    a = a_ref[...]
    for c in range(o_ref.shape[1] // PROJ_CHUNK):
        cs = slice(c * PROJ_CHUNK, (c + 1) * PROJ_CHUNK)
        o_ref[:, cs] = r_ref[:, cs] + jnp.dot(a, w_ref[:, cs], preferred_element_type=F32)


def _matmul_res(a, w, res, *, tm, name):
    s, k = a.shape
    n = w.shape[1]
    assert s % tm == 0 and n % PROJ_CHUNK == 0
    return pl.pallas_call(
        _matmul_res_kernel,
        grid=(s // tm,),
        in_specs=[
            pl.BlockSpec((tm, k), lambda i: (i, 0)),
            pl.BlockSpec((k, n), lambda i: (0, 0), pipeline_mode=pl.Buffered(1)),
            pl.BlockSpec((tm, n), lambda i: (i, 0)),
        ],
        out_specs=pl.BlockSpec((tm, n), lambda i: (i, 0)),
        out_shape=jax.ShapeDtypeStruct((s, n), F32),
        compiler_params=_params("parallel"),
        name=name,
    )(a, w, res)


def _ffn_kernel(x_ref, g_ref, wg_ref, wu_ref, wd_ref, o_ref, h_scr, acc_scr):
    f = pl.program_id(1)

    @pl.when(f == 0)
    def _():
        h_scr[...] = _rms(x_ref[...], g_ref[...]).astype(BF16)
        acc_scr[...] = jnp.zeros_like(acc_scr)

    h = h_scr[...]
    gate = jnp.dot(h, wg_ref[...], preferred_element_type=F32)
    up = jnp.dot(h, wu_ref[...], preferred_element_type=F32)
    act = (gate * jax.nn.sigmoid(gate) * up).astype(BF16)
    acc_scr[...] += jnp.dot(act, wd_ref[...], preferred_element_type=F32)

    @pl.when(f == pl.num_programs(1) - 1)
    def _():
        o_ref[...] = x_ref[...] + acc_scr[...]


def _ffn(x, g, wg, wu, wd, *, tm, tf):
    s, d = x.shape
    dff = wg.shape[1]
    assert s % tm == 0 and dff % tf == 0
    return pl.pallas_call(
        _ffn_kernel,
        grid=(s // tm, dff // tf),
        in_specs=[
            pl.BlockSpec((tm, d), lambda i, f: (i, 0)),
            pl.BlockSpec((1, d), lambda i, f: (0, 0)),
            pl.BlockSpec((d, tf), lambda i, f: (0, f)),
            pl.BlockSpec((d, tf), lambda i, f: (0, f)),
            pl.BlockSpec((tf, d), lambda i, f: (f, 0)),
        ],
        out_specs=pl.BlockSpec((tm, d), lambda i, f: (i, 0)),
        out_shape=jax.ShapeDtypeStruct((s, d), F32),
        scratch_shapes=[pltpu.VMEM((tm, d), BF16), pltpu.VMEM((tm, d), F32)],
        compiler_params=_params("parallel", "arbitrary"),
        name="ffn",
    )(x, g, wg, wu, wd)


def _compress_kernel(x_ref, w1_ref, b1_ref, w2_ref, pos_ref, kg_ref, o_ref):
    half = CMP_STRIDE * DH
    x = x_ref[0, 0]
    nc = x.shape[0]
    top = jnp.dot(x, w1_ref[0, :half, :], preferred_element_type=F32)
    bot = jnp.dot(x, w1_ref[0, half:, :], preferred_element_type=F32)
    bot = pltpu.roll(bot, nc - 1, 0)
    row = lax.broadcasted_iota(jnp.int32, bot.shape, 0)
    bot = jnp.where(row == nc - 1, 0.0, bot)
    pos8 = jnp.broadcast_to(pos_ref[0], (8, 2 * half))
    posb = jnp.dot(pos8, w1_ref[0], preferred_element_type=F32)[0:1]
    hdn = jax.nn.gelu(top + bot + posb + b1_ref[0])
    y = jnp.dot(hdn.astype(BF16), w2_ref[0], preferred_element_type=F32)
    yn = _rms(y, kg_ref[...])
    o_ref[0, 0] = jnp.where(pl.program_id(0) == 0, yn, y).astype(o_ref.dtype)


def _compress(xc, w1, b1, w2, pos, kg):
    _, g, nc, k = xc.shape
    return pl.pallas_call(
        _compress_kernel,
        grid=(2, g),
        in_specs=[
            pl.BlockSpec((1, 1, nc, k), lambda s, i: (s, i, 0, 0)),
            pl.BlockSpec((1, 2 * k, DH), lambda s, i: (s, 0, 0)),
            pl.BlockSpec((1, 1, DH), lambda s, i: (s, 0, 0)),
            pl.BlockSpec((1, DH, DH), lambda s, i: (s, 0, 0)),
            pl.BlockSpec((1, 1, 2 * k), lambda s, i: (s, 0, 0)),
            pl.BlockSpec((1, DH), lambda s, i: (0, 0)),
        ],
        out_specs=pl.BlockSpec((1, 1, nc, DH), lambda s, i: (s, i, 0, 0)),
        out_shape=jax.ShapeDtypeStruct((2, g, nc, DH), BF16),
        compiler_params=_params("parallel", "parallel"),
        name="nsa_compress",
    )(xc, w1, b1, w2, pos, kg)


def _stack_heads(qb):
    return jnp.concatenate([qb[:, r * DH:(r + 1) * DH] for r in range(NSA_REP)], axis=0)


def _cmp_kernel(q_ref, kc_ref, vc_ref, ovt_ref, o_ref, mn_ref, imp_scr, *, tq, ktop, col_steps):
    qi = pl.program_id(0)
    gw = NSA_REP * DH

    def attend(ncols):
        for g in range(NSA_GROUPS):
            q4 = _stack_heads(q_ref[:, g * gw:(g + 1) * gw])
            s = lax.dot_general(q4, kc_ref[g, 0:ncols, :], (((1,), (1,)), ((), ())),
                                preferred_element_type=F32)
            row = lax.broadcasted_iota(jnp.int32, s.shape, 0)
            col = lax.broadcasted_iota(jnp.int32, s.shape, 1)
            t = qi * tq + (row & (tq - 1))
            s = jnp.where(col * CMP_STRIDE + (CMP_LEN - 1) <= t, s, -jnp.inf)
            mx = jnp.max(s, axis=-1, keepdims=True)
            mx = jnp.where(jnp.abs(mx) < jnp.inf, mx, 0.0)
            p = jnp.exp2(s - mx)
            p = p / jnp.maximum(jnp.sum(p, axis=-1, keepdims=True), 1e-30)
            o = jnp.dot(p.astype(BF16), vc_ref[g, 0:ncols, :], preferred_element_type=F32)
            for r in range(NSA_REP):
                o_ref[:, g * gw + r * DH:g * gw + (r + 1) * DH] = o[r * tq:(r + 1) * tq].astype(o_ref.dtype)
            ps = p[0:tq]
            for r in range(1, NSA_REP):
                ps = ps + p[r * tq:(r + 1) * tq]
            imp_scr[g] = lax.dot_general(ovt_ref[:, 0:ncols], ps.astype(BF16), (((1,), (1,)), ((), ())),
                                         preferred_element_type=F32)

    needed = ((qi + 1) * tq - CMP_LEN) // CMP_STRIDE + 1
    lo = 0
    for ncols in col_steps:
        pl.when((needed > lo) & (needed <= ncols))(functools.partial(attend, ncols))
        lo = ncols

    shape = imp_scr.shape[1:]
    jj = lax.broadcasted_iota(jnp.int32, shape, 0)
    cur = (qi * tq + lax.broadcasted_iota(jnp.int32, shape, 1)) // SEL_BLOCK
    forced = (jj == 0) | (jj == cur) | (jj == cur - 1)
    free = (jj >= 1) & (jj <= cur - 2)
    jjf = jj.astype(F32)
    scores = [jnp.where(free, imp_scr[g], -jnp.inf) for g in range(NSA_GROUPS)]
    for _ in range(ktop - 3):
        for g in range(NSA_GROUPS):
            top = jnp.max(scores[g], axis=0, keepdims=True)
            first = jnp.min(jnp.where(scores[g] == top, jjf, 1e9), axis=0, keepdims=True)
            scores[g] = jnp.where(jjf == first, -jnp.inf, scores[g])
    for g in range(NSA_GROUPS):
        picked = forced | (free & (scores[g] == -jnp.inf))
        mn_ref[g] = jnp.where(picked, 0.0, 1.0).T.astype(mn_ref.dtype)


def _cmp_attention(q, kc, vc, overlap_t, *, tq):
    s = q.shape[0]
    g, nc, _ = kc.shape
    nselp = overlap_t.shape[0]
    ktop = min(N_SELECT, s // SEL_BLOCK)
    assert ktop >= 3
    gw = NSA_REP * DH
    col_steps = tuple(range(2 * LANE, nc + 1, 2 * LANE)) if nc % (2 * LANE) == 0 else (nc,)
    return pl.pallas_call(
        functools.partial(_cmp_kernel, tq=tq, ktop=ktop, col_steps=col_steps),
        grid=(s // tq,),
        in_specs=[
            pl.BlockSpec((tq, g * gw), lambda i: (i, 0)),
            pl.BlockSpec((g, nc, DH), lambda i: (0, 0, 0)),
            pl.BlockSpec((g, nc, DH), lambda i: (0, 0, 0)),
            pl.BlockSpec((nselp, nc), lambda i: (0, 0)),
        ],
        out_specs=[
            pl.BlockSpec((tq, g * gw), lambda i: (i, 0)),
            pl.BlockSpec((g, tq, nselp), lambda i: (0, i, 0)),
        ],
        out_shape=[
            jax.ShapeDtypeStruct((s, g * gw), BF16),
            jax.ShapeDtypeStruct((g, s, nselp), BF16),
        ],
        scratch_shapes=[pltpu.VMEM((g, nselp, tq), F32)],
        compiler_params=_params("parallel"),
        name="nsa_cmp_select",
    )(q, kc, vc, overlap_t)


def _sel_kernel(q_ref, mn_ref, k_ref, v_ref, e_ref, o_ref, ka_scr, va_scr, qa_scr, s_scr, m_scr, acc_scr,
                *, tq, tk):
    qi = pl.program_id(1)
    nhalf = qa_scr.shape[0]
    nper = e_ref.shape[0]

    @pl.when(qi == 0)
    def _():
        for c in range(ka_scr.shape[0] // nper):
            rows = slice(c * nper, (c + 1) * nper)
            ka_scr[rows, 0:DH] = k_ref[rows, :]
            ka_scr[rows, DH:2 * DH] = e_ref[...]
            va_scr[rows, 0:DH] = v_ref[rows, :]
            va_scr[rows, DH:2 * DH] = jnp.ones((nper, DH), BF16)

    q4 = _stack_heads(q_ref[...])
    mn = mn_ref[0]
    for hf in range(nhalf):
        part = mn[:, hf * LANE:(hf + 1) * LANE]
        qa_scr[hf, :, 0:DH] = q4
        qa_scr[hf, :, DH:2 * DH] = jnp.concatenate([part] * NSA_REP, axis=0)
    m_scr[...] = jnp.full_like(m_scr, -jnp.inf)
    acc_scr[...] = jnp.zeros_like(acc_scr)

    def scores(j, slot):
        k0 = pl.multiple_of(j * tk, tk)
        s_scr[slot] = lax.dot_general(qa_scr[k0 // (SEL_BLOCK * LANE)], ka_scr[pl.ds(k0, tk), :],
                                      (((1,), (1,)), ((), ())), preferred_element_type=F32)

    def accumulate(j, slot, masked):
        k0 = pl.multiple_of(j * tk, tk)
        s = s_scr[slot]
        if masked:
            row = lax.broadcasted_iota(jnp.int32, s.shape, 0)
            col = lax.broadcasted_iota(jnp.int32, s.shape, 1)
            s = jnp.where(k0 + col <= qi * tq + (row & (tq - 1)), s, -jnp.inf)
        m_old = m_scr[...]
        m_new = jnp.maximum(m_old, jnp.max(s, axis=-1, keepdims=True))
        p = jnp.exp2(s - m_new).astype(BF16)
        acc_scr[...] = (jnp.exp2(m_old - m_new) * acc_scr[...]
                        + jnp.dot(p, va_scr[pl.ds(k0, tk), :], preferred_element_type=F32))
        m_scr[...] = m_new

    n = (qi * tq + tq - 1) // tk + 1
    npair = (n - 1) // 2
    scores(0, 0)

    def pair(i, c):
        scores(2 * i + 1, 1)
        accumulate(2 * i, 0, False)
        scores(2 * i + 2, 0)
        accumulate(2 * i + 1, 1, False)
        return c

    lax.fori_loop(0, npair, pair, 0)

    @pl.when(n % 2 == 1)
    def _():
        accumulate(n - 1, 0, True)

    @pl.when(n % 2 == 0)
    def _():
        scores(n - 1, 1)
        accumulate(n - 2, 0, False)
        accumulate(n - 1, 1, True)

    acc = acc_scr[...]
    o = acc[:, 0:DH] / jnp.maximum(acc[:, DH:2 * DH], 1e-30)
    for r in range(NSA_REP):
        o_ref[:, r * DH:(r + 1) * DH] = o[r * tq:(r + 1) * tq].astype(o_ref.dtype)


def _sel_attention(q, notsel, kv_arr, expand, *, k_col, v_col, tq, tk):
    s = q.shape[0]
    g = notsel.shape[0]
    nselp = notsel.shape[2]
    gw = NSA_REP * DH
    nper = expand.shape[0]
    assert s % nper == 0 and nper % tk == 0 and tk % tq == 0 and s % tq == 0
    once = pl.Buffered(1)
    return pl.pallas_call(
        functools.partial(_sel_kernel, tq=tq, tk=tk),
        grid=(g, s // tq),
        in_specs=[
            pl.BlockSpec((tq, gw), lambda gi, i: (i, gi)),
            pl.BlockSpec((1, tq, nselp), lambda gi, i: (gi, i, 0)),
            pl.BlockSpec((s, DH), lambda gi, i: (0, k_col + gi), pipeline_mode=once),
            pl.BlockSpec((s, DH), lambda gi, i: (0, v_col + gi), pipeline_mode=once),
            pl.BlockSpec((nper, LANE), lambda gi, i: (0, 0), pipeline_mode=once),
        ],
        out_specs=pl.BlockSpec((tq, gw), lambda gi, i: (i, gi)),
        out_shape=jax.ShapeDtypeStruct((s, g * gw), BF16),
        scratch_shapes=[
            pltpu.VMEM((s, 2 * DH), BF16),
            pltpu.VMEM((s, 2 * DH), BF16),
            pltpu.VMEM((nselp // LANE, NSA_REP * tq, 2 * DH), BF16),
            pltpu.VMEM((2, NSA_REP * tq, tk), F32),
            pltpu.VMEM((NSA_REP * tq, 1), F32),
            pltpu.VMEM((NSA_REP * tq, 2 * DH), F32),
        ],
        compiler_params=_params("arbitrary", "arbitrary"),
        name="nsa_sel_attention",
    )(q, notsel, kv_arr, kv_arr, expand)


def _win_kernel(q_ref, *refs, tq, nback):
    nblk = nback + 1
    k_refs = refs[:nblk]
    v_refs = refs[nblk:2 * nblk]
    band_ref, oc_ref, os_ref, gt_ref, o_ref = refs[2 * nblk:]
    qi = pl.program_id(0)
    gw = NSA_REP * DH
    band = band_ref[...]
    col = lax.broadcasted_iota(jnp.int32, band.shape, 1)
    band = jnp.where(col >= (nback - qi) * tq, band, -jnp.inf)
    for g in range(NSA_GROUPS):
        q4 = _stack_heads(q_ref[:, g * gw:(g + 1) * gw])
        kc = jnp.concatenate([r[:, g * DH:(g + 1) * DH] for r in k_refs], axis=0)
        vc = jnp.concatenate([r[:, g * DH:(g + 1) * DH] for r in v_refs], axis=0)
        s = lax.dot_general(q4, kc, (((1,), (1,)), ((), ())), preferred_element_type=F32) + band
        mx = jnp.max(s, axis=-1, keepdims=True)
        mx = jnp.where(jnp.abs(mx) < jnp.inf, mx, 0.0)
        p = jnp.exp2(s - mx)
        p = p / jnp.maximum(jnp.sum(p, axis=-1, keepdims=True), 1e-30)
        ow = jnp.dot(p.astype(BF16), vc, preferred_element_type=F32)
        gates = gt_ref[...]
        for r in range(NSA_REP):
            sl = slice(g * gw + r * DH, g * gw + (r + 1) * DH)
            c0 = 3 * (g * NSA_REP + r)
            out = (gates[:, c0:c0 + 1] * oc_ref[:, sl].astype(F32)
                   + gates[:, c0 + 1:c0 + 2] * os_ref[:, sl].astype(F32)
                   + gates[:, c0 + 2:c0 + 3] * ow[r * tq:(r + 1) * tq])
            o_ref[:, sl] = out.astype(o_ref.dtype)


def _win_attention(q, kv_arr, o_cmp, o_sel, gates, *, k_blk, v_blk, tq):
    s = q.shape[0]
    g = NSA_GROUPS
    gw = NSA_REP * DH
    nback = WINDOW // tq
    assert nback * tq == WINDOW
    qq = jnp.arange(NSA_REP * tq)[:, None] % tq
    kk = jnp.arange((nback + 1) * tq)[None, :]
    band = jnp.where((kk > qq) & (kk <= qq + WINDOW), 0.0, -jnp.inf).astype(F32)

    def kvmap(blk, b):
        def f(i):
            return (jnp.maximum(i - nback + b, 0), blk)
        return f

    k_specs = [pl.BlockSpec((tq, g * DH), kvmap(k_blk, b)) for b in range(nback + 1)]
    v_specs = [pl.BlockSpec((tq, g * DH), kvmap(v_blk, b)) for b in range(nback + 1)]
    wide = pl.BlockSpec((tq, g * gw), lambda i: (i, 0))
    return pl.pallas_call(
        functools.partial(_win_kernel, tq=tq, nback=nback),
        grid=(s // tq,),
        in_specs=([wide] + k_specs + v_specs
                  + [pl.BlockSpec(band.shape, lambda i: (0, 0)), wide, wide,
                     pl.BlockSpec((tq, LANE), lambda i: (i, 0))]),
        out_specs=wide,
        out_shape=jax.ShapeDtypeStruct((s, g * gw), BF16),
        compiler_params=_params("parallel"),
        name="nsa_win_combine",
    )(q, *([kv_arr] * (2 * (nback + 1))), band, o_cmp, o_sel, gates)


def _mlstm_kernel(q_ref, k_ref, v_ref, o_ref, gt_ref, og_ref, y_ref, c_scr, n_scr, m_scr):
    L = ML_CHUNK

    @pl.when(pl.program_id(0) == 0)
    def _():
        c_scr[...] = jnp.zeros_like(c_scr)
        n_scr[...] = jnp.zeros_like(n_scr)
        m_scr[...] = jnp.full_like(m_scr, NEG_INIT)

    ri = lax.broadcasted_iota(jnp.int32, (L, L), 0)
    ci = lax.broadcasted_iota(jnp.int32, (L, L), 1)
    eye = ri == ci
    tril = ci <= ri
    triu = ri <= ci
    gates = gt_ref[...]
    n_all = n_scr[...]
    m_all = m_scr[...]
    n_rows, m_rows = [], []

    def to_row(col):
        return jnp.sum(jnp.where(eye, col, 0.0), axis=0, keepdims=True)

    for h in range(ML_HEADS):
        qh = q_ref[:, h * ML_DK:(h + 1) * ML_DK]
        kh = k_ref[:, h * ML_DK:(h + 1) * ML_DK]
        vh = v_ref[:, h * ML_DV:(h + 1) * ML_DV]
        ig_col = gates[:, h:h + 1]
        fg_col = gates[:, ML_HEADS + h:ML_HEADS + h + 1]
        lf_col = jnp.minimum(fg_col, 0.0) - jnp.log(1.0 + jnp.exp(-jnp.abs(fg_col)))
        lf_row = to_row(lf_col)
        ig_row = to_row(ig_col)
        b_col = jnp.sum(jnp.where(tril, lf_row, 0.0), axis=1, keepdims=True)
        b_row = jnp.sum(jnp.where(triu, lf_col, 0.0), axis=0, keepdims=True)
        m_old = m_all[h:h + 1, 0:1]
        dmat = jnp.where(tril, b_col - b_row + ig_row, -jnp.inf)
        m_inter = b_col + m_old
        m_t = jnp.maximum(m_inter, jnp.max(dmat, axis=1, keepdims=True))
        qk = lax.dot_general(qh, kh, (((1,), (1,)), ((), ())), preferred_element_type=F32)
        a = jnp.exp(dmat - m_t) * qk
        dec = jnp.exp(m_inter - m_t)
        c_old = c_scr[h]
        n_old = n_all[h:h + 1, :]
        num = (jnp.dot(a.astype(BF16), vh, preferred_element_type=F32)
               + dec * jnp.dot(qh, c_old.astype(BF16), preferred_element_type=F32))
        qn = jnp.sum(qh.astype(F32) * n_old, axis=1, keepdims=True)
        den = jnp.sum(a, axis=1, keepdims=True) + dec * qn
        hx = num / jnp.maximum(jnp.abs(den), jnp.exp(-m_t))

        b_last = b_col[L - 1:L, :]
        g_col = b_last - b_col + ig_col
        m_new = jnp.maximum(b_last + m_old, jnp.max(g_col, axis=0, keepdims=True))
        w_col = jnp.exp(g_col - m_new)
        cd = jnp.exp(b_last + m_old - m_new)
        kw = kh.astype(F32) * w_col
        c_scr[h] = cd * c_old + lax.dot_general(kw.astype(BF16), vh, (((0,), (0,)), ((), ())),
                                                preferred_element_type=F32)
        n_rows.append(cd * n_old + jnp.sum(kw, axis=0, keepdims=True))
        m_rows.append(jnp.broadcast_to(m_new, (1, LANE)))

        sl = slice(h * ML_DV, (h + 1) * ML_DV)
        hn = _rms(hx, og_ref[:, sl])
        y_ref[:, sl] = (jax.nn.sigmoid(o_ref[:, sl].astype(F32)) * hn).astype(y_ref.dtype)

    n_scr[...] = jnp.concatenate(n_rows, axis=0)
    m_scr[...] = jnp.concatenate(m_rows, axis=0)


def _mlstm(qkvo, gates, out_g):
    s = qkvo.shape[0]
    L = ML_CHUNK
    wq = ML_HEADS * ML_DK
    wv = ML_HEADS * ML_DV
    return pl.pallas_call(
        _mlstm_kernel,
        grid=(s // L,),
        in_specs=[
            pl.BlockSpec((L, wq), lambda c: (c, 0)),
            pl.BlockSpec((L, wq), lambda c: (c, 1)),
            pl.BlockSpec((L, wv), lambda c: (c, 1)),
            pl.BlockSpec((L, wv), lambda c: (c, 2)),
            pl.BlockSpec((L, LANE), lambda c: (c, 0)),
            pl.BlockSpec((1, wv), lambda c: (0, 0)),
        ],
        out_specs=pl.BlockSpec((L, wv), lambda c: (c, 0)),
        out_shape=jax.ShapeDtypeStruct((s, wv), BF16),
        scratch_shapes=[
            pltpu.VMEM((ML_HEADS, ML_DK, ML_DV), F32),
            pltpu.VMEM((ML_HEADS, ML_DK), F32),
            pltpu.VMEM((ML_HEADS, LANE), F32),
        ],
        compiler_params=_params("arbitrary"),
        name="mlstm_scan",
    )(qkvo, qkvo, qkvo, qkvo, gates, out_g)


def _pad_cols(a, n):
    return jnp.pad(a, ((0, 0), (0, n - a.shape[1])))


def _nsa_layer(x, norm_g, w_in, b_gate, q_g, k_g, cmp_pos, cmp_w1, cmp_b1, cmp_w2, w_out, *, tm, tq, tk):
    s, d = x.shape
    qd = NSA_HEADS * DH
    kvd = NSA_GROUPS * DH
    norm_g = norm_g.reshape(1, d)

    w_main = w_in[:, :qd + 6 * kvd].astype(BF16)
    ones = jnp.ones((kvd,), F32)
    gain = jnp.concatenate([jnp.tile(q_g, NSA_HEADS) * (DH ** -0.5 * LOG2E), ones, ones,
                            jnp.tile(k_g[1], NSA_GROUPS), ones, jnp.tile(k_g[2], NSA_GROUPS), ones])
    flag = jnp.concatenate([jnp.ones((qd,), F32), 0 * ones, 0 * ones, ones, 0 * ones, ones, 0 * ones])
    w_gate = _pad_cols(w_in[:, qd + 6 * kvd:], LANE).astype(BF16)
    bias = _pad_cols(b_gate.reshape(1, -1), LANE)
    proj, gates = _proj(x, norm_g, w_main, gain.reshape(1, -1), flag.reshape(1, -1), w_gate, bias,
                        mode="headnorm", gate_mode="sigmoid", tm=tm, tn=1024, name="nsa_proj")

    nc = s // CMP_STRIDE
    xc = proj[:, qd:qd + 2 * kvd].reshape(nc, CMP_STRIDE, 2, NSA_GROUPS, DH)
    xc = jnp.transpose(xc, (2, 3, 0, 1, 4)).reshape(2, NSA_GROUPS, nc, CMP_STRIDE * DH)
    kvc = _compress(xc, cmp_w1.astype(BF16), cmp_b1.reshape(2, 1, DH), cmp_w2.astype(BF16),
                    cmp_pos.reshape(2, 1, CMP_LEN * DH).astype(BF16), k_g[0].reshape(1, DH))

    n_sel = s // SEL_BLOCK
    nselp = -(-n_sel // LANE) * LANE
    cstart = jnp.arange(nc) * CMP_STRIDE
    sstart = jnp.arange(nselp) * SEL_BLOCK
    overlap_t = ((cstart[None, :] < sstart[:, None] + SEL_BLOCK)
                 & (cstart[None, :] + CMP_LEN > sstart[:, None])
                 & (jnp.arange(nselp)[:, None] < n_sel)
                 & (jnp.arange(nc)[None, :] < nc - 1)).astype(BF16)
    o_cmp, notsel = _cmp_attention(proj, kvc[0], kvc[1], overlap_t, tq=tq)

    blk = jnp.arange(min(s, SEL_BLOCK * LANE)) // SEL_BLOCK
    expand =jnp.where(blk[:, None] == jnp.arange(LANE)[None, :], MASK_BIAS, 0.0).astype(BF16)
    col0 = qd // DH
    o_sel = _sel_attention(proj, notsel, proj, expand, k_col=col0 + 2 * NSA_GROUPS,
                           v_col=col0 + 3 * NSA_GROUPS, tq=min(2 * tq, s), tk=tk)
    mixed = _win_attention(proj, proj, o_cmp, o_sel, gates, k_blk=(qd + 4 * kvd) // kvd,
                           v_blk=(qd + 5 * kvd) // kvd, tq=tq)
    return _matmul_res(mixed, w_out.astype(BF16), x, tm=tm, name="nsa_out_proj")


def _mlstm_layer(x, norm_g, w_in, b_if, out_g, w_out, *, tm):
    s, d = x.shape
    norm_g = norm_g.reshape(1, d)
    wq = ML_HEADS * ML_DK
    wv = ML_HEADS * ML_DV
    nmain = 2 * wq + 2 * wv
    w_main = w_in[:, :nmain].astype(BF16)
    scale = jnp.concatenate([jnp.ones((wq,), F32), jnp.full((wq,), ML_DK ** -0.5, F32),
                             jnp.ones((2 * wv,), F32)]).reshape(1, -1)
    w_gate = _pad_cols(w_in[:, nmain:], LANE).astype(BF16)
    bias = _pad_cols(b_if.reshape(1, -1), LANE)
    qkvo, gates = _proj(x, norm_g, w_main, scale, scale, w_gate, bias, mode="scale", gate_mode="bias",
                        tm=tm, tn=1024, name="ml_proj")
    y = _mlstm(qkvo, gates, out_g.reshape(1, -1))
    return _matmul_res(y, w_out.astype(BF16), x, tm=tm, name="ml_out_proj")


def _ffn_layer(x, norm_g, wg, wu, wd, *, tm):
    return _ffn(x, norm_g.reshape(1, -1), wg.astype(BF16), wu.astype(BF16), wd.astype(BF16), tm=tm, tf=512)


def kernel(x, norm_mix_g, norm_ffn_g, nsa_w_in, nsa_b_gate, nsa_q_norm_g, nsa_k_norm_g, nsa_cmp_pos,
           nsa_cmp_w1, nsa_cmp_b1, nsa_cmp_w2, nsa_w_out, ml_w_in, ml_b_if, ml_out_norm_g, ml_w_out,
           ffn_w_gate, ffn_w_up, ffn_w_down):
    b, s, d = x.shape
    depth = norm_mix_g.shape[0]
    tm = min(512, s)
    tq = 128
    tk = min(1024, s)
    outs = []
    for bi in range(b):
        xb = x[bi]
        for i in range(depth):
            j = i // 2
            if i % 2 == 0:
                xb = _nsa_layer(xb, norm_mix_g[i], nsa_w_in[j], nsa_b_gate[j], nsa_q_norm_g[j],
                                nsa_k_norm_g[j], nsa_cmp_pos[j], nsa_cmp_w1[j], nsa_cmp_b1[j],
                                nsa_cmp_w2[j], nsa_w_out[j], tm=tm, tq=tq, tk=tk)
            else:
                xb = _mlstm_layer(xb, norm_mix_g[i], ml_w_in[j], ml_b_if[j], ml_out_norm_g[j],
                                  ml_w_out[j], tm=tm)
            xb = _ffn_layer(xb, norm_ffn_g[i], ffn_w_gate[i], ffn_w_up[i], ffn_w_down[i], tm=tm)
        outs.append(xb)
    return jnp.stack(outs, axis=0)
```

```python
import functools

import jax
import jax.numpy as jnp
from jax import lax
from jax.experimental import pallas as pl
from jax.experimental.pallas import tpu as pltpu

F32 = jnp.float32
BF16 = jnp.bfloat16

EPS = 1e-6
NEG_INIT = -1e30
LOG2E = 1.4426950408889634

LANE = 128
VMEM_LIMIT = 56 * 1024 * 1024
PROJ_CHUNK = 512

NSA_HEADS = 16
NSA_GROUPS = 4
NSA_REP = NSA_HEADS // NSA_GROUPS
DH = 128
CMP_LEN = 32
CMP_STRIDE = 16
SEL_BLOCK = 64
N_SELECT = 16
WINDOW = 512
ML_HEADS = 8
ML_DK = 128
ML_DV = 256
ML_CHUNK = 512

MASK_BIAS = -(2.0 ** 100)


def _params(*sem):
    return pltpu.CompilerParams(dimension_semantics=sem, vmem_limit_bytes=VMEM_LIMIT)


def _rms(x, g):
    ms = jnp.mean(x * x, axis=-1, keepdims=True)
    return x * lax.rsqrt(ms + EPS) * g


def _proj_kernel(x_ref, g_ref, w_ref, a_ref, b_ref, wg_ref, bg_ref, o_ref, og_ref, h_scr, *, mode, gate_mode):
    @pl.when(pl.program_id(1) == 0)
    def _():
        h = _rms(x_ref[...], g_ref[...]).astype(BF16)
        h_scr[...] = h
        gl = jnp.dot(h, wg_ref[...], preferred_element_type=F32) + bg_ref[...]
        og_ref[...] = jax.nn.sigmoid(gl) if gate_mode == "sigmoid" else gl

    h = h_scr[...]
    for c in range(o_ref.shape[1] // PROJ_CHUNK):
        cs = slice(c * PROJ_CHUNK, (c + 1) * PROJ_CHUNK)
        y = jnp.dot(h, w_ref[:, cs], preferred_element_type=F32)
        if mode == "headnorm":
            for u in range(PROJ_CHUNK // LANE):
                sl = slice(c * PROJ_CHUNK + u * LANE, c * PROJ_CHUNK + (u + 1) * LANE)
                yc = y[:, u * LANE:(u + 1) * LANE]
                ms = jnp.mean(yc * yc, axis=-1, keepdims=True)
                mult = jnp.where(b_ref[:, sl] > 0.0, lax.rsqrt(ms + EPS), 1.0) * a_ref[:, sl]
                o_ref[:, sl] = (yc * mult).astype(o_ref.dtype)
        elif mode == "scale":
            o_ref[:, cs] = (y * a_ref[:, cs]).astype(o_ref.dtype)
        else:
            raise ValueError(mode)


def _proj(x, g, w, a, b, wg, bg, *, mode, gate_mode, tm, tn, name):
    s, d = x.shape
    n = w.shape[1]
    ng = wg.shape[1]
    assert s % tm == 0 and n % tn == 0 and tn % PROJ_CHUNK == 0
    return pl.pallas_call(
        functools.partial(_proj_kernel, mode=mode, gate_mode=gate_mode),
        grid=(s // tm, n // tn),
        in_specs=[
            pl.BlockSpec((tm, d), lambda i, j: (i, 0)),
            pl.BlockSpec((1, d), lambda i, j: (0, 0)),
            pl.BlockSpec((d, tn), lambda i, j: (0, j)),
            pl.BlockSpec((1, tn), lambda i, j: (0, j)),
            pl.BlockSpec((1, tn), lambda i, j: (0, j)),
            pl.BlockSpec((d, ng), lambda i, j: (0, 0)),
            pl.BlockSpec((1, ng), lambda i, j: (0, 0)),
        ],
        out_specs=[
            pl.BlockSpec((tm, tn), lambda i, j: (i, j)),
            pl.BlockSpec((tm, ng), lambda i, j: (i, 0)),
        ],
        out_shape=[
            jax.ShapeDtypeStruct((s, n), BF16),
            jax.ShapeDtypeStruct((s, ng), F32),
        ],
        scratch_shapes=[pltpu.VMEM((tm, d), BF16)],
        compiler_params=_params("parallel", "arbitrary"),
        name=name,
    )(x, g, w, a, b, wg, bg)


def _matmul_res_kernel(a_ref, w_ref, r_ref, o_ref):
    a = a_ref[...]
    for c in range(o_ref.shape[1] // PROJ_CHUNK):
        cs = slice(c * PROJ_CHUNK, (c + 1) * PROJ_CHUNK)
        o_ref[:, cs] = r_ref[:, cs] + jnp.dot(a, w_ref[:, cs], preferred_element_type=F32)


def _matmul_res(a, w, res, *, tm, name):
    s, k = a.shape
    n = w.shape[1]
    assert s % tm == 0 and n % PROJ_CHUNK == 0
    return pl.pallas_call(
        _matmul_res_kernel,
        grid=(s // tm,),
        in_specs=[
            pl.BlockSpec((tm, k), lambda i: (i, 0)),
            pl.BlockSpec((k, n), lambda i: (0, 0), pipeline_mode=pl.Buffered(1)),
            pl.BlockSpec((tm, n), lambda i: (i, 0)),
        ],
        out_specs=pl.BlockSpec((tm, n), lambda i: (i, 0)),
        out_shape=jax.ShapeDtypeStruct((s, n), F32),
        compiler_params=_params("parallel"),
        name=name,
    )(a, w, res)


def _ffn_kernel(x_ref, g_ref, wg_ref, wu_ref, wd_ref, o_ref, h_scr, acc_scr):
    f = pl.program_id(1)

    @pl.when(f == 0)
    def _():
        h_scr[...] = _rms(x_ref[...], g_ref[...]).astype(BF16)
        acc_scr[...] = jnp.zeros_like(acc_scr)

    h = h_scr[...]
    gate = jnp.dot(h, wg_ref[...], preferred_element_type=F32)
    up = jnp.dot(h, wu_ref[...], preferred_element_type=F32)
    act = (gate * jax.nn.sigmoid(gate) * up).astype(BF16)
    acc_scr[...] += jnp.dot(act, wd_ref[...], preferred_element_type=F32)

    @pl.when(f == pl.num_programs(1) - 1)
    def _():
        o_ref[...] = x_ref[...] + acc_scr[...]


def _ffn(x, g, wg, wu, wd, *, tm, tf):
    s, d = x.shape
    dff = wg.shape[1]
    assert s % tm == 0 and dff % tf == 0
    return pl.pallas_call(
        _ffn_kernel,
        grid=(s // tm, dff // tf),
        in_specs=[
            pl.BlockSpec((tm, d), lambda i, f: (i, 0)),
            pl.BlockSpec((1, d), lambda i, f: (0, 0)),
            pl.BlockSpec((d, tf), lambda i, f: (0, f)),
            pl.BlockSpec((d, tf), lambda i, f: (0, f)),
            pl.BlockSpec((tf, d), lambda i, f: (f, 0)),
        ],
        out_specs=pl.BlockSpec((tm, d), lambda i, f: (i, 0)),
        out_shape=jax.ShapeDtypeStruct((s, d), F32),
        scratch_shapes=[pltpu.VMEM((tm, d), BF16), pltpu.VMEM((tm, d), F32)],
        compiler_params=_params("parallel", "arbitrary"),
        name="ffn",
    )(x, g, wg, wu, wd)


def _compress_kernel(x_ref, w1_ref, b1_ref, w2_ref, pos_ref, kg_ref, o_ref):
    half = CMP_STRIDE * DH
    x = x_ref[0, 0]
    nc = x.shape[0]
    top = jnp.dot(x, w1_ref[0, :half, :], preferred_element_type=F32)
    bot = jnp.dot(x, w1_ref[0, half:, :], preferred_element_type=F32)
    bot = pltpu.roll(bot, nc - 1, 0)
    row = lax.broadcasted_iota(jnp.int32, bot.shape, 0)
    bot = jnp.where(row == nc - 1, 0.0, bot)
    pos8 = jnp.broadcast_to(pos_ref[0], (8, 2 * half))
    posb = jnp.dot(pos8, w1_ref[0], preferred_element_type=F32)[0:1]
    hdn = jax.nn.gelu(top + bot + posb + b1_ref[0])
    y = jnp.dot(hdn.astype(BF16), w2_ref[0], preferred_element_type=F32)
    yn = _rms(y, kg_ref[...])
    o_ref[0, 0] = jnp.where(pl.program_id(0) == 0, yn, y).astype(o_ref.dtype)


def _compress(xc, w1, b1, w2, pos, kg):
    _, g, nc, k = xc.shape
    return pl.pallas_call(
        _compress_kernel,
        grid=(2, g),
        in_specs=[
            pl.BlockSpec((1, 1, nc, k), lambda s, i: (s, i, 0, 0)),
            pl.BlockSpec((1, 2 * k, DH), lambda s, i: (s, 0, 0)),
            pl.BlockSpec((1, 1, DH), lambda s, i: (s, 0, 0)),
            pl.BlockSpec((1, DH, DH), lambda s, i: (s, 0, 0)),
            pl.BlockSpec((1, 1, 2 * k), lambda s, i: (s, 0, 0)),
            pl.BlockSpec((1, DH), lambda s, i: (0, 0)),
        ],
        out_specs=pl.BlockSpec((1, 1, nc, DH), lambda s, i: (s, i, 0, 0)),
        out_shape=jax.ShapeDtypeStruct((2, g, nc, DH), BF16),
        compiler_params=_params("parallel", "parallel"),
        name="nsa_compress",
    )(xc, w1, b1, w2, pos, kg)


def _stack_heads(qb):
    return jnp.concatenate([qb[:, r * DH:(r + 1) * DH] for r in range(NSA_REP)], axis=0)


def _cmp_kernel(q_ref, kc_ref, vc_ref, ovt_ref, o_ref, mn_ref, imp_scr, *, tq, ktop, col_steps):
    qi = pl.program_id(0)
    gw = NSA_REP * DH

    def attend(ncols):
        for g in range(NSA_GROUPS):
            q4 = _stack_heads(q_ref[:, g * gw:(g + 1) * gw])
            s = lax.dot_general(q4, kc_ref[g, 0:ncols, :], (((1,), (1,)), ((), ())),
                                preferred_element_type=F32)
            row = lax.broadcasted_iota(jnp.int32, s.shape, 0)
            col = lax.broadcasted_iota(jnp.int32, s.shape, 1)
            t = qi * tq + (row & (tq - 1))
            s = jnp.where(col * CMP_STRIDE + (CMP_LEN - 1) <= t, s, -jnp.inf)
            mx = jnp.max(s, axis=-1, keepdims=True)
            mx = jnp.where(jnp.abs(mx) < jnp.inf, mx, 0.0)
            p = jnp.exp2(s - mx)
            p = p / jnp.maximum(jnp.sum(p, axis=-1, keepdims=True), 1e-30)
            o = jnp.dot(p.astype(BF16), vc_ref[g, 0:ncols, :], preferred_element_type=F32)
            for r in range(NSA_REP):
                o_ref[:, g * gw + r * DH:g * gw + (r + 1) * DH] = o[r * tq:(r + 1) * tq].astype(o_ref.dtype)
            ps = p[0:tq]
            for r in range(1, NSA_REP):
                ps = ps + p[r * tq:(r + 1) * tq]
            imp_scr[g] = lax.dot_general(ovt_ref[:, 0:ncols], ps.astype(BF16), (((1,), (1,)), ((), ())),
                                         preferred_element_type=F32)

    needed = ((qi + 1) * tq - CMP_LEN) // CMP_STRIDE + 1
    lo = 0
    for ncols in col_steps:
        pl.when((needed > lo) & (needed <= ncols))(functools.partial(attend, ncols))
        lo = ncols

    shape = imp_scr.shape[1:]
    jj = lax.broadcasted_iota(jnp.int32, shape, 0)
    cur = (qi * tq + lax.broadcasted_iota(jnp.int32, shape, 1)) // SEL_BLOCK
    forced = (jj == 0) | (jj == cur) | (jj == cur - 1)
    free = (jj >= 1) & (jj <= cur - 2)
    jjf = jj.astype(F32)
    scores = [jnp.where(free, imp_scr[g], -jnp.inf) for g in range(NSA_GROUPS)]
    for _ in range(ktop - 3):
        for g in range(NSA_GROUPS):
            top = jnp.max(scores[g], axis=0, keepdims=True)
            first = jnp.min(jnp.where(scores[g] == top, jjf, 1e9), axis=0, keepdims=True)
            scores[g] = jnp.where(jjf == first, -jnp.inf, scores[g])
    for g in range(NSA_GROUPS):
        picked = forced | (free & (scores[g] == -jnp.inf))
        mn_ref[g] = jnp.where(picked, 0.0, 1.0).T.astype(mn_ref.dtype)


def _cmp_attention(q, kc, vc, overlap_t, *, tq):
    s = q.shape[0]
    g, nc, _ = kc.shape
    nselp = overlap_t.shape[0]
    ktop = min(N_SELECT, s // SEL_BLOCK)
    assert ktop >= 3
    gw = NSA_REP * DH
    col_steps = tuple(range(2 * LANE, nc + 1, 2 * LANE)) if nc % (2 * LANE) == 0 else (nc,)
    return pl.pallas_call(
        functools.partial(_cmp_kernel, tq=tq, ktop=ktop, col_steps=col_steps),
        grid=(s // tq,),
        in_specs=[
            pl.BlockSpec((tq, g * gw), lambda i: (i, 0)),
            pl.BlockSpec((g, nc, DH), lambda i: (0, 0, 0)),
            pl.BlockSpec((g, nc, DH), lambda i: (0, 0, 0)),
            pl.BlockSpec((nselp, nc), lambda i: (0, 0)),
        ],
        out_specs=[
            pl.BlockSpec((tq, g * gw), lambda i: (i, 0)),
            pl.BlockSpec((g, tq, nselp), lambda i: (0, i, 0)),
        ],
        out_shape=[
            jax.ShapeDtypeStruct((s, g * gw), BF16),
            jax.ShapeDtypeStruct((g, s, nselp), BF16),
        ],
        scratch_shapes=[pltpu.VMEM((g, nselp, tq), F32)],
        compiler_params=_params("parallel"),
        name="nsa_cmp_select",
    )(q, kc, vc, overlap_t)


def _sel_kernel(q_ref, mn_ref, k_ref, v_ref, e_ref, o_ref, ka_scr, va_scr, qa_scr, s_scr, m_scr, acc_scr,
                *, tq, tk):
    qi = pl.program_id(1)
    nhalf = qa_scr.shape[0]
    nper = e_ref.shape[0]

    @pl.when(qi == 0)
    def _():
        for c in range(ka_scr.shape[0] // nper):
            rows = slice(c * nper, (c + 1) * nper)
            ka_scr[rows, 0:DH] = k_ref[rows, :]
            ka_scr[rows, DH:2 * DH] = e_ref[...]
            va_scr[rows, 0:DH] = v_ref[rows, :]
            va_scr[rows, DH:2 * DH] = jnp.ones((nper, DH), BF16)

    q4 = _stack_heads(q_ref[...])
    mn = mn_ref[0]
    for hf in range(nhalf):
        part = mn[:, hf * LANE:(hf + 1) * LANE]
        qa_scr[hf, :, 0:DH] = q4
        qa_scr[hf, :, DH:2 * DH] = jnp.concatenate([part] * NSA_REP, axis=0)
    m_scr[...] = jnp.full_like(m_scr, -jnp.inf)
    acc_scr[...] = jnp.zeros_like(acc_scr)

    def scores(j, slot):
        k0 = pl.multiple_of(j * tk, tk)
        s_scr[slot] = lax.dot_general(qa_scr[k0 // (SEL_BLOCK * LANE)], ka_scr[pl.ds(k0, tk), :],
                                      (((1,), (1,)), ((), ())), preferred_element_type=F32)

    def accumulate(j, slot, masked):
        k0 = pl.multiple_of(j * tk, tk)
        s = s_scr[slot]
        if masked:
            row = lax.broadcasted_iota(jnp.int32, s.shape, 0)
            col = lax.broadcasted_iota(jnp.int32, s.shape, 1)
            s = jnp.where(k0 + col <= qi * tq + (row & (tq - 1)), s, -jnp.inf)
        m_old = m_scr[...]
        m_new = jnp.maximum(m_old, jnp.max(s, axis=-1, keepdims=True))
        p = jnp.exp2(s - m_new).astype(BF16)
        acc_scr[...] = (jnp.exp2(m_old - m_new) * acc_scr[...]
                        + jnp.dot(p, va_scr[pl.ds(k0, tk), :], preferred_element_type=F32))
        m_scr[...] = m_new

    n = (qi * tq + tq - 1) // tk + 1
    npair = (n - 1) // 2
    scores(0, 0)

    def pair(i, c):
        scores(2 * i + 1, 1)
        accumulate(2 * i, 0, False)
        scores(2 * i + 2, 0)
        accumulate(2 * i + 1, 1, False)
        return c

    lax.fori_loop(0, npair, pair, 0)

    @pl.when(n % 2 == 1)
    def _():
        accumulate(n - 1, 0, True)

    @pl.when(n % 2 == 0)
    def _():
        scores(n - 1, 1)
        accumulate(n - 2, 0, False)
        accumulate(n - 1, 1, True)

    acc = acc_scr[...]
    o = acc[:, 0:DH] / jnp.maximum(acc[:, DH:2 * DH], 1e-30)
    for r in range(NSA_REP):
        o_ref[:, r * DH:(r + 1) * DH] = o[r * tq:(r + 1) * tq].astype(o_ref.dtype)


def _sel_attention(q, notsel, kv_arr, expand, *, k_col, v_col, tq, tk):
    s = q.shape[0]
    g = notsel.shape[0]
    nselp = notsel.shape[2]
    gw = NSA_REP * DH
    nper = expand.shape[0]
    assert s % nper == 0 and nper % tk == 0 and tk % tq == 0 and s % tq == 0
    once = pl.Buffered(1)
    return pl.pallas_call(
        functools.partial(_sel_kernel, tq=tq, tk=tk),
        grid=(g, s // tq),
        in_specs=[
            pl.BlockSpec((tq, gw), lambda gi, i: (i, gi)),
            pl.BlockSpec((1, tq, nselp), lambda gi, i: (gi, i, 0)),
            pl.BlockSpec((s, DH), lambda gi, i: (0, k_col + gi), pipeline_mode=once),
            pl.BlockSpec((s, DH), lambda gi, i: (0, v_col + gi), pipeline_mode=once),
            pl.BlockSpec((nper, LANE), lambda gi, i: (0, 0), pipeline_mode=once),
        ],
        out_specs=pl.BlockSpec((tq, gw), lambda gi, i: (i, gi)),
        out_shape=jax.ShapeDtypeStruct((s, g * gw), BF16),
        scratch_shapes=[
            pltpu.VMEM((s, 2 * DH), BF16),
            pltpu.VMEM((s, 2 * DH), BF16),
            pltpu.VMEM((nselp // LANE, NSA_REP * tq, 2 * DH), BF16),
            pltpu.VMEM((2, NSA_REP * tq, tk), F32),
            pltpu.VMEM((NSA_REP * tq, 1), F32),
            pltpu.VMEM((NSA_REP * tq, 2 * DH), F32),
        ],
        compiler_params=_params("arbitrary", "arbitrary"),
        name="nsa_sel_attention",
    )(q, notsel, kv_arr, kv_arr, expand)


def _win_kernel(q_ref, *refs, tq, nback):
    nblk = nback + 1
    k_refs = refs[:nblk]
    v_refs = refs[nblk:2 * nblk]
    band_ref, oc_ref, os_ref, gt_ref, o_ref = refs[2 * nblk:]
    qi = pl.program_id(0)
    gw = NSA_REP * DH
    band = band_ref[...]
    col = lax.broadcasted_iota(jnp.int32, band.shape, 1)
    band = jnp.where(col >= (nback - qi) * tq, band, -jnp.inf)
    for g in range(NSA_GROUPS):
        q4 = _stack_heads(q_ref[:, g * gw:(g + 1) * gw])
        kc = jnp.concatenate([r[:, g * DH:(g + 1) * DH] for r in k_refs], axis=0)
        vc = jnp.concatenate([r[:, g * DH:(g + 1) * DH] for r in v_refs], axis=0)
        s = lax.dot_general(q4, kc, (((1,), (1,)), ((), ())), preferred_element_type=F32) + band
        mx = jnp.max(s, axis=-1, keepdims=True)
        mx = jnp.where(jnp.abs(mx) < jnp.inf, mx, 0.0)
        p = jnp.exp2(s - mx)
        p = p / jnp.maximum(jnp.sum(p, axis=-1, keepdims=True), 1e-30)
        ow = jnp.dot(p.astype(BF16), vc, preferred_element_type=F32)
        gates = gt_ref[...]
        for r in range(NSA_REP):
            sl = slice(g * gw + r * DH, g * gw + (r + 1) * DH)
            c0 = 3 * (g * NSA_REP + r)
            out = (gates[:, c0:c0 + 1] * oc_ref[:, sl].astype(F32)
                   + gates[:, c0 + 1:c0 + 2] * os_ref[:, sl].astype(F32)
                   + gates[:, c0 + 2:c0 + 3] * ow[r * tq:(r + 1) * tq])
            o_ref[:, sl] = out.astype(o_ref.dtype)


def _win_attention(q, kv_arr, o_cmp, o_sel, gates, *, k_blk, v_blk, tq):
    s = q.shape[0]
    g = NSA_GROUPS
    gw = NSA_REP * DH
    nback = WINDOW // tq
    assert nback * tq == WINDOW
    qq = jnp.arange(NSA_REP * tq)[:, None] % tq
    kk = jnp.arange((nback + 1) * tq)[None, :]
    band = jnp.where((kk > qq) & (kk <= qq + WINDOW), 0.0, -jnp.inf).astype(F32)

    def kvmap(blk, b):
        def f(i):
            return (jnp.maximum(i - nback + b, 0), blk)
        return f

    k_specs = [pl.BlockSpec((tq, g * DH), kvmap(k_blk, b)) for b in range(nback + 1)]
    v_specs = [pl.BlockSpec((tq, g * DH), kvmap(v_blk, b)) for b in range(nback + 1)]
    wide = pl.BlockSpec((tq, g * gw), lambda i: (i, 0))
    return pl.pallas_call(
        functools.partial(_win_kernel, tq=tq, nback=nback),
        grid=(s // tq,),
        in_specs=([wide] + k_specs + v_specs
                  + [pl.BlockSpec(band.shape, lambda i: (0, 0)), wide, wide,
                     pl.BlockSpec((tq, LANE), lambda i: (i, 0))]),
        out_specs=wide,
        out_shape=jax.ShapeDtypeStruct((s, g * gw), BF16),
        compiler_params=_params("parallel"),
        name="nsa_win_combine",
    )(q, *([kv_arr] * (2 * (nback + 1))), band, o_cmp, o_sel, gates)


def _mlstm_kernel(q_ref, k_ref, v_ref, o_ref, gt_ref, og_ref, y_ref, c_scr, n_scr, m_scr):
    L = q_ref.shape[0]

    @pl.when(pl.program_id(0) == 0)
    def _():
        c_scr[...] = jnp.zeros_like(c_scr)
        n_scr[...] = jnp.zeros_like(n_scr)
        m_scr[...] = jnp.full_like(m_scr, NEG_INIT)

    ri = lax.broadcasted_iota(jnp.int32, (L, L), 0)
    ci = lax.broadcasted_iota(jnp.int32, (L, L), 1)
    eye = ri == ci
    tril = ci <= ri
    triu = ri <= ci
    gates = gt_ref[...]

    def to_row(col):
        return jnp.sum(jnp.where(eye, col, 0.0), axis=0, keepdims=True)

    for h in range(ML_HEADS):
        qh = q_ref[:, h * ML_DK:(h + 1) * ML_DK]
        kh = k_ref[:, h * ML_DK:(h + 1) * ML_DK]
        vh = v_ref[:, h * ML_DV:(h + 1) * ML_DV]
        ig_col = gates[:, h:h + 1]
        fg_col = gates[:, ML_HEADS + h:ML_HEADS + h + 1]
        lf_col = jnp.minimum(fg_col, 0.0) - jnp.log(1.0 + jnp.exp(-jnp.abs(fg_col)))
        lf_row = to_row(lf_col)
        ig_row = to_row(ig_col)
        b_col = jnp.sum(jnp.where(tril, lf_row, 0.0), axis=1, keepdims=True)
        b_row = jnp.sum(jnp.where(triu, lf_col, 0.0), axis=0, keepdims=True)
        m_old = m_scr[h:h + 1, 0:1]
        dmat = jnp.where(tril, b_col - b_row + ig_row, -jnp.inf)
        m_inter = b_col + m_old
        m_t = jnp.maximum(m_inter, jnp.max(dmat, axis=1, keepdims=True))
        qk = lax.dot_general(qh, kh, (((1,), (1,)), ((), ())), preferred_element_type=F32)
        a = jnp.exp(dmat - m_t) * qk
        dec = jnp.exp(m_inter - m_t)
        c_old = c_scr[h]
        n_old = n_scr[h:h + 1, :]
        num = (jnp.dot(a.astype(BF16), vh, preferred_element_type=F32)
               + dec * jnp.dot(qh, c_old.astype(BF16), preferred_element_type=F32))
        qn = jnp.sum(qh.astype(F32) * n_old, axis=1, keepdims=True)
        den = jnp.sum(a, axis=1, keepdims=True) + dec * qn
        hx = num / jnp.maximum(jnp.abs(den), jnp.exp(-m_t))

        b_last = b_col[L - 1:L, :]
        g_col = b_last - b_col + ig_col
        m_new = jnp.maximum(b_last + m_old, jnp.max(g_col, axis=0, keepdims=True))
        w_col = jnp.exp(g_col - m_new)
        cd = jnp.exp(b_last + m_old - m_new)
        kw = kh.astype(F32) * w_col
        c_scr[h] = cd * c_old + lax.dot_general(kw.astype(BF16), vh, (((0,), (0,)), ((), ())),
                                                preferred_element_type=F32)
        n_scr[h:h + 1, :] = cd * n_old + jnp.sum(kw, axis=0, keepdims=True)
        m_scr[h:h + 1, :] = jnp.broadcast_to(m_new, (1, LANE))

        sl = slice(h * ML_DV, (h + 1) * ML_DV)
        hn = _rms(hx, og_ref[:, sl])
        y_ref[:, sl] = (jax.nn.sigmoid(o_ref[:, sl].astype(F32)) * hn).astype(y_ref.dtype)


def _mlstm(qkvo, gates, out_g):
    s = qkvo.shape[0]
    L = min(ML_CHUNK, s)
    assert s % L == 0
    wq = ML_HEADS * ML_DK
    wv = ML_HEADS * ML_DV
    return pl.pallas_call(
        _mlstm_kernel,
        grid=(s // L,),
        in_specs=[
            pl.BlockSpec((L, wq), lambda c: (c, 0)),
            pl.BlockSpec((L, wq), lambda c: (c, 1)),
            pl.BlockSpec((L, wv), lambda c: (c, 1)),
            pl.BlockSpec((L, wv), lambda c: (c, 2)),
            pl.BlockSpec((L, LANE), lambda c: (c, 0)),
            pl.BlockSpec((1, wv), lambda c: (0, 0)),
        ],
        out_specs=pl.BlockSpec((L, wv), lambda c: (c, 0)),
        out_shape=jax.ShapeDtypeStruct((s, wv), BF16),
        scratch_shapes=[
            pltpu.VMEM((ML_HEADS, ML_DK, ML_DV), F32),
            pltpu.VMEM((ML_HEADS, ML_DK), F32),
            pltpu.VMEM((ML_HEADS, LANE), F32),
        ],
        compiler_params=_params("arbitrary"),
        name="mlstm_scan",
    )(qkvo, qkvo, qkvo, qkvo, gates, out_g)


def _pad_cols(a, n):
    return jnp.pad(a, ((0, 0), (0, n - a.shape[1])))


def _nsa_layer(x, norm_g, w_in, b_gate, q_g, k_g, cmp_pos, cmp_w1, cmp_b1, cmp_w2, w_out, *, tm, tq, tk):
    s, d = x.shape
    qd = NSA_HEADS * DH
    kvd = NSA_GROUPS * DH
    norm_g = norm_g.reshape(1, d)

    w_main = w_in[:, :qd + 6 * kvd].astype(BF16)
    ones = jnp.ones((kvd,), F32)
    gain = jnp.concatenate([jnp.tile(q_g, NSA_HEADS) * (DH ** -0.5 * LOG2E), ones, ones,
                            jnp.tile(k_g[1], NSA_GROUPS), ones, jnp.tile(k_g[2], NSA_GROUPS), ones])
    flag = jnp.concatenate([jnp.ones((qd,), F32), 0 * ones, 0 * ones, ones, 0 * ones, ones, 0 * ones])
    w_gate = _pad_cols(w_in[:, qd + 6 * kvd:], LANE).astype(BF16)
    bias = _pad_cols(b_gate.reshape(1, -1), LANE)
    proj, gates = _proj(x, norm_g, w_main, gain.reshape(1, -1), flag.reshape(1, -1), w_gate, bias,
                        mode="headnorm", gate_mode="sigmoid", tm=min(2 * tm, s), tn=1024, name="nsa_proj")

    nc = s // CMP_STRIDE
    xc = proj[:, qd:qd + 2 * kvd].reshape(nc, CMP_STRIDE, 2, NSA_GROUPS, DH)
    xc = jnp.transpose(xc, (2, 3, 0, 1, 4)).reshape(2, NSA_GROUPS, nc, CMP_STRIDE * DH)
    kvc = _compress(xc, cmp_w1.astype(BF16), cmp_b1.reshape(2, 1, DH), cmp_w2.astype(BF16),
                    cmp_pos.reshape(2, 1, CMP_LEN * DH).astype(BF16), k_g[0].reshape(1, DH))

    n_sel = s // SEL_BLOCK
    nselp = -(-n_sel // LANE) * LANE
    cstart = jnp.arange(nc) * CMP_STRIDE
    sstart = jnp.arange(nselp) * SEL_BLOCK
    overlap_t = ((cstart[None, :] < sstart[:, None] + SEL_BLOCK)
                 & (cstart[None, :] + CMP_LEN > sstart[:, None])
                 & (jnp.arange(nselp)[:, None] < n_sel)
                 & (jnp.arange(nc)[None, :] < nc - 1)).astype(BF16)
    o_cmp, notsel = _cmp_attention(proj, kvc[0], kvc[1], overlap_t, tq=tq)

    blk = jnp.arange(min(s, SEL_BLOCK * LANE)) // SEL_BLOCK
    expand =jnp.where(blk[:, None] == jnp.arange(LANE)[None, :], MASK_BIAS, 0.0).astype(BF16)
    col0 = qd // DH
    o_sel = _sel_attention(proj, notsel, proj, expand, k_col=col0 + 2 * NSA_GROUPS,
                           v_col=col0 + 3 * NSA_GROUPS, tq=min(2 * tq, s), tk=tk)
    mixed = _win_attention(proj, proj, o_cmp, o_sel, gates, k_blk=(qd + 4 * kvd) // kvd,
                           v_blk=(qd + 5 * kvd) // kvd, tq=tq)
    return _matmul_res(mixed, w_out.astype(BF16), x, tm=tm, name="nsa_out_proj")


def _mlstm_layer(x, norm_g, w_in, b_if, out_g, w_out, *, tm):
    s, d = x.shape
    norm_g = norm_g.reshape(1, d)
    wq = ML_HEADS * ML_DK
    wv = ML_HEADS * ML_DV
    nmain = 2 * wq + 2 * wv
    w_main = w_in[:, :nmain].astype(BF16)
    scale = jnp.concatenate([jnp.ones((wq,), F32), jnp.full((wq,), ML_DK ** -0.5, F32),
                             jnp.ones((2 * wv,), F32)]).reshape(1, -1)
    w_gate = _pad_cols(w_in[:, nmain:], LANE).astype(BF16)
    bias = _pad_cols(b_if.reshape(1, -1), LANE)
    qkvo, gates = _proj(x, norm_g, w_main, scale, scale, w_gate, bias, mode="scale", gate_mode="bias",
                        tm=min(2 * tm, s), tn=1024, name="ml_proj")
    y = _mlstm(qkvo, gates, out_g.reshape(1, -1))
    return _matmul_res(y, w_out.astype(BF16), x, tm=tm, name="ml_out_proj")


def _ffn_layer(x, norm_g, wg, wu, wd, *, tm):
    return _ffn(x, norm_g.reshape(1, -1), wg.astype(BF16), wu.astype(BF16), wd.astype(BF16), tm=tm, tf=512)


def kernel(x, norm_mix_g, norm_ffn_g, nsa_w_in, nsa_b_gate, nsa_q_norm_g, nsa_k_norm_g, nsa_cmp_pos,
           nsa_cmp_w1, nsa_cmp_b1, nsa_cmp_w2, nsa_w_out, ml_w_in, ml_b_if, ml_out_norm_g, ml_w_out,
           ffn_w_gate, ffn_w_up, ffn_w_down):
    b, s, d = x.shape
    depth = norm_mix_g.shape[0]
    tm = min(512, s)
    tq = 128
    tk = min(1024, s)
    outs = []
    for bi in range(b):
        xb = x[bi]
        for i in range(depth):
            j = i // 2
            if i % 2 == 0:
                xb = _nsa_layer(xb, norm_mix_g[i], nsa_w_in[j], nsa_b_gate[j], nsa_q_norm_g[j],
                                nsa_k_norm_g[j], nsa_cmp_pos[j], nsa_cmp_w1[j], nsa_cmp_b1[j],
                                nsa_cmp_w2[j], nsa_w_out[j], tm=tm, tq=tq, tk=tk)
            else:
                xb = _mlstm_layer(xb, norm_mix_g[i], ml_w_in[j], ml_b_if[j], ml_out_norm_g[j],
                                  ml_w_out[j], tm=tm)
            xb = _ffn_layer(xb, norm_ffn_g[i], ffn_w_gate[i], ffn_w_up[i], ffn_w_down[i], tm=tm)
        outs.append(xb)
    return jnp.stack(outs, axis=0)
```

```python
import functools

import jax
import jax.numpy as jnp
from jax import lax
from jax.experimental import pallas as pl
from jax.experimental.pallas import tpu as pltpu

F32 = jnp.float32
BF16 = jnp.bfloat16

EPS = 1e-6
NEG_INIT = -1e30
LOG2E = 1.4426950408889634

LANE = 128
VMEM_LIMIT = 56 * 1024 * 1024
PROJ_CHUNK = 512

NSA_HEADS = 16
NSA_GROUPS = 4
NSA_REP = NSA_HEADS // NSA_GROUPS
DH = 128
CMP_LEN = 32
CMP_STRIDE = 16
SEL_BLOCK = 64
N_SELECT = 16
WINDOW = 512
ML_HEADS = 8
ML_DK = 128
ML_DV = 256
ML_CHUNK = 512

MASK_BIAS = -(2.0 ** 100)


def _params(*sem):
    return pltpu.CompilerParams(dimension_semantics=sem, vmem_limit_bytes=VMEM_LIMIT)


def _rms(x, g):
    ms = jnp.mean(x * x, axis=-1, keepdims=True)
    return x * lax.rsqrt(ms + EPS) * g


def _proj_kernel(x_ref, g_ref, w_ref, a_ref, b_ref, wg_ref, bg_ref, o_ref, og_ref, h_scr, *, mode, gate_mode):
    @pl.when(pl.program_id(1) == 0)
    def _():
        h = _rms(x_ref[...], g_ref[...]).astype(BF16)
        h_scr[...] = h
        gl = jnp.dot(h, wg_ref[...], preferred_element_type=F32) + bg_ref[...]
        og_ref[...] = jax.nn.sigmoid(gl) if gate_mode == "sigmoid" else gl

    h = h_scr[...]
    for c in range(o_ref.shape[1] // PROJ_CHUNK):
        cs = slice(c * PROJ_CHUNK, (c + 1) * PROJ_CHUNK)
        y = jnp.dot(h, w_ref[:, cs], preferred_element_type=F32)
        if mode == "headnorm":
            for u in range(PROJ_CHUNK // LANE):
                sl = slice(c * PROJ_CHUNK + u * LANE, c * PROJ_CHUNK + (u + 1) * LANE)
                yc = y[:, u * LANE:(u + 1) * LANE]
                ms = jnp.mean(yc * yc, axis=-1, keepdims=True)
                mult = jnp.where(b_ref[:, sl] > 0.0, lax.rsqrt(ms + EPS), 1.0) * a_ref[:, sl]
                o_ref[:, sl] = (yc * mult).astype(o_ref.dtype)
        elif mode == "scale":
            o_ref[:, cs] = (y * a_ref[:, cs]).astype(o_ref.dtype)
        else:
            raise ValueError(mode)


def _proj(x, g, w, a, b, wg, bg, *, mode, gate_mode, tm, tn, name):
    s, d = x.shape
    n = w.shape[1]
    ng = wg.shape[1]
    assert s % tm == 0 and n % tn == 0 and tn % PROJ_CHUNK == 0
    return pl.pallas_call(
        functools.partial(_proj_kernel, mode=mode, gate_mode=gate_mode),
        grid=(s // tm, n // tn),
        in_specs=[
            pl.BlockSpec((tm, d), lambda i, j: (i, 0)),
            pl.BlockSpec((1, d), lambda i, j: (0, 0)),
            pl.BlockSpec((d, tn), lambda i, j: (0, j)),
            pl.BlockSpec((1, tn), lambda i, j: (0, j)),
            pl.BlockSpec((1, tn), lambda i, j: (0, j)),
            pl.BlockSpec((d, ng), lambda i, j: (0, 0)),
            pl.BlockSpec((1, ng), lambda i, j: (0, 0)),
        ],
        out_specs=[
            pl.BlockSpec((tm, tn), lambda i, j: (i, j)),
            pl.BlockSpec((tm, ng), lambda i, j: (i, 0)),
        ],
        out_shape=[
            jax.ShapeDtypeStruct((s, n), BF16),
            jax.ShapeDtypeStruct((s, ng), F32),
        ],
        scratch_shapes=[pltpu.VMEM((tm, d), BF16)],
        compiler_params=_params("parallel", "arbitrary"),
        name=name,
    )(x, g, w, a, b, wg, bg)


def _matmul_res_kernel(a_ref, w_ref, r_ref, o_ref):
    a = a_ref[...]
    for c in range(o_ref.shape[1] // PROJ_CHUNK):
        cs = slice(c * PROJ_CHUNK, (c + 1) * PROJ_CHUNK)
        o_ref[:, cs] = r_ref[:, cs] + jnp.dot(a, w_ref[:, cs], preferred_element_type=F32)


def _matmul_res(a, w, res, *, tm, name):
    s, k = a.shape
    n = w.shape[1]
    assert s % tm == 0 and n % PROJ_CHUNK == 0
    return pl.pallas_call(
        _matmul_res_kernel,
        grid=(s // tm,),
        in_specs=[
            pl.BlockSpec((tm, k), lambda i: (i, 0)),
            pl.BlockSpec((k, n), lambda i: (0, 0), pipeline_mode=pl.Buffered(1)),
            pl.BlockSpec((tm, n), lambda i: (i, 0)),
        ],
        out_specs=pl.BlockSpec((tm, n), lambda i: (i, 0)),
        out_shape=jax.ShapeDtypeStruct((s, n), F32),
        compiler_params=_params("parallel"),
        name=name,
    )(a, w, res)


def _ffn_kernel(x_ref, g_ref, wg_ref, wu_ref, wd_ref, o_ref, h_scr):
    @pl.when(pl.program_id(1) == 0)
    def _():
        x = x_ref[...]
        h_scr[...] = _rms(x, g_ref[...]).astype(BF16)
        o_ref[...] = x

    half = h_scr.shape[0] // 2
    for u in range(2):
        rs = slice(u * half, (u + 1) * half)
        h = h_scr[rs, :]
        gate = jnp.dot(h, wg_ref[...], preferred_element_type=F32)
        up = jnp.dot(h, wu_ref[...], preferred_element_type=F32)
        act = (gate * jax.nn.sigmoid(gate) * up).astype(BF16)
        o_ref[rs, :] += jnp.dot(act, wd_ref[...], preferred_element_type=F32)


def _ffn(x, g, wg, wu, wd, *, tm, tf):
    s, d = x.shape
    dff = wg.shape[1]
    assert s % tm == 0 and dff % tf == 0
    return pl.pallas_call(
        _ffn_kernel,
        grid=(s // tm, dff // tf),
        in_specs=[
            pl.BlockSpec((tm, d), lambda i, f: (i, 0)),
            pl.BlockSpec((1, d), lambda i, f: (0, 0)),
            pl.BlockSpec((d, tf), lambda i, f: (0, f)),
            pl.BlockSpec((d, tf), lambda i, f: (0, f)),
            pl.BlockSpec((tf, d), lambda i, f: (f, 0)),
        ],
        out_specs=pl.BlockSpec((tm, d), lambda i, f: (i, 0)),
        out_shape=jax.ShapeDtypeStruct((s, d), F32),
        scratch_shapes=[pltpu.VMEM((tm, d), BF16)],
        compiler_params=_params("parallel", "arbitrary"),
        name="ffn",
    )(x, g, wg, wu, wd)


def _compress_kernel(x_ref, w1_ref, b1_ref, w2_ref, pos_ref, kg_ref, o_ref):
    half = CMP_STRIDE * DH
    x = x_ref[0, 0]
    nc = x.shape[0]
    top = jnp.dot(x, w1_ref[0, :half, :], preferred_element_type=F32)
    bot = jnp.dot(x, w1_ref[0, half:, :], preferred_element_type=F32)
    bot = pltpu.roll(bot, nc - 1, 0)
    row = lax.broadcasted_iota(jnp.int32, bot.shape, 0)
    bot = jnp.where(row == nc - 1, 0.0, bot)
    pos8 = jnp.broadcast_to(pos_ref[0], (8, 2 * half))
    posb = jnp.dot(pos8, w1_ref[0], preferred_element_type=F32)[0:1]
    hdn = jax.nn.gelu(top + bot + posb + b1_ref[0])
    y = jnp.dot(hdn.astype(BF16), w2_ref[0], preferred_element_type=F32)
    yn = _rms(y, kg_ref[...])
    o_ref[0, 0] = jnp.where(pl.program_id(0) == 0, yn, y).astype(o_ref.dtype)


def _compress(xc, w1, b1, w2, pos, kg):
    _, g, nc, k = xc.shape
    return pl.pallas_call(
        _compress_kernel,
        grid=(2, g),
        in_specs=[
            pl.BlockSpec((1, 1, nc, k), lambda s, i: (s, i, 0, 0)),
            pl.BlockSpec((1, 2 * k, DH), lambda s, i: (s, 0, 0)),
            pl.BlockSpec((1, 1, DH), lambda s, i: (s, 0, 0)),
            pl.BlockSpec((1, DH, DH), lambda s, i: (s, 0, 0)),
            pl.BlockSpec((1, 1, 2 * k), lambda s, i: (s, 0, 0)),
            pl.BlockSpec((1, DH), lambda s, i: (0, 0)),
        ],
        out_specs=pl.BlockSpec((1, 1, nc, DH), lambda s, i: (s, i, 0, 0)),
        out_shape=jax.ShapeDtypeStruct((2, g, nc, DH), BF16),
        compiler_params=_params("parallel", "parallel"),
        name="nsa_compress",
    )(xc, w1, b1, w2, pos, kg)


def _stack_heads(qb):
    return jnp.concatenate([qb[:, r * DH:(r + 1) * DH] for r in range(NSA_REP)], axis=0)


def _cmp_kernel(q_ref, kc_ref, vc_ref, ovt_ref, o_ref, mn_ref, imp_scr, *, tq, ktop, col_steps):
    qi = pl.program_id(0)
    gw = NSA_REP * DH

    def attend(ncols, nb):
        for g in range(NSA_GROUPS):
            q4 = _stack_heads(q_ref[:, g * gw:(g + 1) * gw])
            s = lax.dot_general(q4, kc_ref[g, 0:ncols, :], (((1,), (1,)), ((), ())),
                                preferred_element_type=F32)
            row = lax.broadcasted_iota(jnp.int32, s.shape, 0)
            col = lax.broadcasted_iota(jnp.int32, s.shape, 1)
            t = qi * tq + (row & (tq - 1))
            s = jnp.where(col * CMP_STRIDE + (CMP_LEN - 1) <= t, s, -jnp.inf)
            mx = jnp.max(s, axis=-1, keepdims=True)
            mx = jnp.where(jnp.abs(mx) < jnp.inf, mx, 0.0)
            p = jnp.exp2(s - mx)
            p = p / jnp.maximum(jnp.sum(p, axis=-1, keepdims=True), 1e-30)
            o = jnp.dot(p.astype(BF16), vc_ref[g, 0:ncols, :], preferred_element_type=F32)
            for r in range(NSA_REP):
                o_ref[:, g * gw + r * DH:g * gw + (r + 1) * DH] = o[r * tq:(r + 1) * tq].astype(o_ref.dtype)
            ps = p[0:tq]
            for r in range(1, NSA_REP):
                ps = ps + p[r * tq:(r + 1) * tq]
            imp_scr[g, 0:nb, :] = lax.dot_general(ovt_ref[0:nb, 0:ncols], ps.astype(BF16),
                                                  (((1,), (1,)), ((), ())), preferred_element_type=F32)

    def select(nb):
        nselp = imp_scr.shape[1]
        jj = lax.broadcasted_iota(jnp.int32, (nb, tq), 0)
        cur = (qi * tq + lax.broadcasted_iota(jnp.int32, (nb, tq), 1)) // SEL_BLOCK
        forced = (jj == 0) | (jj == cur) | (jj == cur - 1)
        free = (jj >= 1) & (jj <= cur - 2)
        jjf = jj.astype(F32)
        scores = [jnp.where(free, imp_scr[g, 0:nb, :], -jnp.inf) for g in range(NSA_GROUPS)]
        for _ in range(ktop - 3):
            for g in range(NSA_GROUPS):
                top = jnp.max(scores[g], axis=0, keepdims=True)
                first = jnp.min(jnp.where(scores[g] == top, jjf, 1e9), axis=0, keepdims=True)
                scores[g] = jnp.where(jjf == first, -jnp.inf, scores[g])
        for g in range(NSA_GROUPS):
            picked = forced | (free & (scores[g] == -jnp.inf))
            notsel = jnp.where(picked, 0.0, 1.0)
            if nb < nselp:
                notsel = jnp.concatenate([notsel, jnp.ones((nselp - nb, tq), F32)], axis=0)
            mn_ref[g] = notsel.T.astype(mn_ref.dtype)

    def variant(ncols):
        nb = min(imp_scr.shape[1], -(-(ncols * CMP_STRIDE // SEL_BLOCK) // 8) * 8)
        attend(ncols, nb)
        select(nb)

    needed = ((qi + 1) * tq - CMP_LEN) // CMP_STRIDE + 1
    lo = 0
    for ncols in col_steps:
        pl.when((needed > lo) & (needed <= ncols))(functools.partial(variant, ncols))
        lo = ncols


def _cmp_attention(q, kc, vc, overlap_t, *, tq):
    s = q.shape[0]
    g, nc, _ = kc.shape
    nselp = overlap_t.shape[0]
    ktop = min(N_SELECT, s // SEL_BLOCK)
    assert ktop >= 3
    gw = NSA_REP * DH
    col_steps = tuple(range(2 * LANE, nc + 1, 2 * LANE)) if nc % (2 * LANE) == 0 else (nc,)
    return pl.pallas_call(
        functools.partial(_cmp_kernel, tq=tq, ktop=ktop, col_steps=col_steps),
        grid=(s // tq,),
        in_specs=[
            pl.BlockSpec((tq, g * gw), lambda i: (i, 0)),
            pl.BlockSpec((g, nc, DH), lambda i: (0, 0, 0)),
            pl.BlockSpec((g, nc, DH), lambda i: (0, 0, 0)),
            pl.BlockSpec((nselp, nc), lambda i: (0, 0)),
        ],
        out_specs=[
            pl.BlockSpec((tq, g * gw), lambda i: (i, 0)),
            pl.BlockSpec((g, tq, nselp), lambda i: (0, i, 0)),
        ],
        out_shape=[
            jax.ShapeDtypeStruct((s, g * gw), BF16),
            jax.ShapeDtypeStruct((g, s, nselp), BF16),
        ],
        scratch_shapes=[pltpu.VMEM((g, nselp, tq), F32)],
        compiler_params=_params("parallel"),
        name="nsa_cmp_select",
    )(q, kc, vc, overlap_t)


def _sel_kernel(q_ref, mn_ref, k_ref, v_ref, e_ref, o_ref, ka_scr, va_scr, qa_scr, s_scr, m_scr, acc_scr,
                *, tq, tk):
    qi = pl.program_id(1)
    nhalf = qa_scr.shape[0]
    nper = e_ref.shape[0]

    @pl.when(qi == 0)
    def _():
        for c in range(ka_scr.shape[0] // nper):
            rows = slice(c * nper, (c + 1) * nper)
            ka_scr[rows, 0:DH] = k_ref[rows, :]
            ka_scr[rows, DH:2 * DH] = e_ref[...]
            va_scr[rows, 0:DH] = v_ref[rows, :]
            va_scr[rows, DH:2 * DH] = jnp.ones((nper, DH), BF16)

    q4 = _stack_heads(q_ref[...])
    mn = mn_ref[0]
    for hf in range(nhalf):
        part = mn[:, hf * LANE:(hf + 1) * LANE]
        qa_scr[hf, :, 0:DH] = q4
        qa_scr[hf, :, DH:2 * DH] = jnp.concatenate([part] * NSA_REP, axis=0)
    m_scr[...] = jnp.full_like(m_scr, -jnp.inf)
    acc_scr[...] = jnp.zeros_like(acc_scr)

    def scores(j, slot):
        k0 = pl.multiple_of(j * tk, tk)
        s_scr[slot] = lax.dot_general(qa_scr[k0 // (SEL_BLOCK * LANE)], ka_scr[pl.ds(k0, tk), :],
                                      (((1,), (1,)), ((), ())), preferred_element_type=F32)

    def accumulate(j, slot, masked):
        k0 = pl.multiple_of(j * tk, tk)
        s = s_scr[slot]
        if masked:
            row = lax.broadcasted_iota(jnp.int32, s.shape, 0)
            col = lax.broadcasted_iota(jnp.int32, s.shape, 1)
            s = jnp.where(k0 + col <= qi * tq + (row & (tq - 1)), s, -jnp.inf)
        m_old = m_scr[...]
        m_new = jnp.maximum(m_old, jnp.max(s, axis=-1, keepdims=True))
        p = jnp.exp2(s - m_new).astype(BF16)
        acc_scr[...] = (jnp.exp2(m_old - m_new) * acc_scr[...]
                        + jnp.dot(p, va_scr[pl.ds(k0, tk), :], preferred_element_type=F32))
        m_scr[...] = m_new

    n = (qi * tq + tq - 1) // tk + 1
    npair = (n - 1) // 2
    scores(0, 0)

    def pair(i, c):
        scores(2 * i + 1, 1)
        accumulate(2 * i, 0, False)
        scores(2 * i + 2, 0)
        accumulate(2 * i + 1, 1, False)
        return c

    lax.fori_loop(0, npair, pair, 0)

    @pl.when(n % 2 == 1)
    def _():
        accumulate(n - 1, 0, True)

    @pl.when(n % 2 == 0)
    def _():
        scores(n - 1, 1)
        accumulate(n - 2, 0, False)
        accumulate(n - 1, 1, True)

    acc = acc_scr[...]
    o = acc[:, 0:DH] / jnp.maximum(acc[:, DH:2 * DH], 1e-30)
    for r in range(NSA_REP):
        o_ref[:, r * DH:(r + 1) * DH] = o[r * tq:(r + 1) * tq].astype(o_ref.dtype)


def _sel_attention(q, notsel, kv_arr, expand, *, k_col, v_col, tq, tk):
    s = q.shape[0]
    g = notsel.shape[0]
    nselp = notsel.shape[2]
    gw = NSA_REP * DH
    nper = expand.shape[0]
    assert s % nper == 0 and nper % tk == 0 and tk % tq == 0 and s % tq == 0
    once = pl.Buffered(1)
    return pl.pallas_call(
        functools.partial(_sel_kernel, tq=tq, tk=tk),
        grid=(g, s // tq),
        in_specs=[
            pl.BlockSpec((tq, gw), lambda gi, i: (i, gi)),
            pl.BlockSpec((1, tq, nselp), lambda gi, i: (gi, i, 0)),
            pl.BlockSpec((s, DH), lambda gi, i: (0, k_col + gi), pipeline_mode=once),
            pl.BlockSpec((s, DH), lambda gi, i: (0, v_col + gi), pipeline_mode=once),
            pl.BlockSpec((nper, LANE), lambda gi, i: (0, 0), pipeline_mode=once),
        ],
        out_specs=pl.BlockSpec((tq, gw), lambda gi, i: (i, gi)),
        out_shape=jax.ShapeDtypeStruct((s, g * gw), BF16),
        scratch_shapes=[
            pltpu.VMEM((s, 2 * DH), BF16),
            pltpu.VMEM((s, 2 * DH), BF16),
            pltpu.VMEM((nselp // LANE, NSA_REP * tq, 2 * DH), BF16),
            pltpu.VMEM((2, NSA_REP * tq, tk), F32),
            pltpu.VMEM((NSA_REP * tq, 1), F32),
            pltpu.VMEM((NSA_REP * tq, 2 * DH), F32),
        ],
        compiler_params=_params("arbitrary", "arbitrary"),
        name="nsa_sel_attention",
    )(q, notsel, kv_arr, kv_arr, expand)


def _win_kernel(q_ref, *refs, tq, nback):
    nblk = nback + 1
    k_refs = refs[:nblk]
    v_refs = refs[nblk:2 * nblk]
    band_ref, oc_ref, os_ref, gt_ref, o_ref = refs[2 * nblk:]
    qi = pl.program_id(0)
    gw = NSA_REP * DH
    band = band_ref[...]
    col = lax.broadcasted_iota(jnp.int32, band.shape, 1)
    band = jnp.where(col >= (nback - qi) * tq, band, -jnp.inf)
    for g in range(NSA_GROUPS):
        q4 = _stack_heads(q_ref[:, g * gw:(g + 1) * gw])
        kc = jnp.concatenate([r[:, g * DH:(g + 1) * DH] for r in k_refs], axis=0)
        vc = jnp.concatenate([r[:, g * DH:(g + 1) * DH] for r in v_refs], axis=0)
        s = lax.dot_general(q4, kc, (((1,), (1,)), ((), ())), preferred_element_type=F32) + band
        mx = jnp.max(s, axis=-1, keepdims=True)
        mx = jnp.where(jnp.abs(mx) < jnp.inf, mx, 0.0)
        p = jnp.exp2(s - mx)
        p = p / jnp.maximum(jnp.sum(p, axis=-1, keepdims=True), 1e-30)
        ow = jnp.dot(p.astype(BF16), vc, preferred_element_type=F32)
        gates = gt_ref[...]
        for r in range(NSA_REP):
            sl = slice(g * gw + r * DH, g * gw + (r + 1) * DH)
            c0 = 3 * (g * NSA_REP + r)
            out = (gates[:, c0:c0 + 1] * oc_ref[:, sl].astype(F32)
                   + gates[:, c0 + 1:c0 + 2] * os_ref[:, sl].astype(F32)
                   + gates[:, c0 + 2:c0 + 3] * ow[r * tq:(r + 1) * tq])
            o_ref[:, sl] = out.astype(o_ref.dtype)


def _win_attention(q, kv_arr, o_cmp, o_sel, gates, *, k_blk, v_blk, tq):
    s = q.shape[0]
    g = NSA_GROUPS
    gw = NSA_REP * DH
    nback = WINDOW // tq
    assert nback * tq == WINDOW
    qq = jnp.arange(NSA_REP * tq)[:, None] % tq
    kk = jnp.arange((nback + 1) * tq)[None, :]
    band = jnp.where((kk > qq) & (kk <= qq + WINDOW), 0.0, -jnp.inf).astype(F32)

    def kvmap(blk, b):
        def f(i):
            return (jnp.maximum(i - nback + b, 0), blk)
        return f

    k_specs = [pl.BlockSpec((tq, g * DH), kvmap(k_blk, b)) for b in range(nback + 1)]
    v_specs = [pl.BlockSpec((tq, g * DH), kvmap(v_blk, b)) for b in range(nback + 1)]
    wide = pl.BlockSpec((tq, g * gw), lambda i: (i, 0))
    return pl.pallas_call(
        functools.partial(_win_kernel, tq=tq, nback=nback),
        grid=(s // tq,),
        in_specs=([wide] + k_specs + v_specs
                  + [pl.BlockSpec(band.shape, lambda i: (0, 0)), wide, wide,
                     pl.BlockSpec((tq, LANE), lambda i: (i, 0))]),
        out_specs=wide,
        out_shape=jax.ShapeDtypeStruct((s, g * gw), BF16),
        compiler_params=_params("parallel"),
        name="nsa_win_combine",
    )(q, *([kv_arr] * (2 * (nback + 1))), band, o_cmp, o_sel, gates)


def _mlstm_kernel(q_ref, k_ref, v_ref, o_ref, gt_ref, og_ref, y_ref, c_scr, n_scr, m_scr):
    L = q_ref.shape[0]

    @pl.when(pl.program_id(0) == 0)
    def _():
        c_scr[...] = jnp.zeros_like(c_scr)
        n_scr[...] = jnp.zeros_like(n_scr)
        m_scr[...] = jnp.full_like(m_scr, NEG_INIT)

    ri = lax.broadcasted_iota(jnp.int32, (L, L), 0)
    ci = lax.broadcasted_iota(jnp.int32, (L, L), 1)
    eye = ri == ci
    tril = ci <= ri
    triu = ri <= ci
    gates = gt_ref[...]

    def to_row(col):
        return jnp.sum(jnp.where(eye, col, 0.0), axis=0, keepdims=True)

    for h in range(ML_HEADS):
        qh = q_ref[:, h * ML_DK:(h + 1) * ML_DK]
        kh = k_ref[:, h * ML_DK:(h + 1) * ML_DK]
        vh = v_ref[:, h * ML_DV:(h + 1) * ML_DV]
        ig_col = gates[:, h:h + 1]
        fg_col = gates[:, ML_HEADS + h:ML_HEADS + h + 1]
        lf_col = jnp.minimum(fg_col, 0.0) - jnp.log(1.0 + jnp.exp(-jnp.abs(fg_col)))
        lf_row = to_row(lf_col)
        ig_row = to_row(ig_col)
        b_col = jnp.sum(jnp.where(tril, lf_row, 0.0), axis=1, keepdims=True)
        b_row = jnp.sum(jnp.where(triu, lf_col, 0.0), axis=0, keepdims=True)
        m_old = m_scr[h:h + 1, 0:1]
        dmat = jnp.where(tril, b_col - b_row + ig_row, -jnp.inf)
        m_inter = b_col + m_old
        m_t = jnp.maximum(m_inter, jnp.max(dmat, axis=1, keepdims=True))
        qk = lax.dot_general(qh, kh, (((1,), (1,)), ((), ())), preferred_element_type=F32)
        a = jnp.exp(dmat - m_t) * qk
        dec = jnp.exp(m_inter - m_t)
        c_old = c_scr[h]
        n_old = n_scr[h:h + 1, :]
        num = (jnp.dot(a.astype(BF16), vh, preferred_element_type=F32)
               + dec * jnp.dot(qh, c_old.astype(BF16), preferred_element_type=F32))
        qn = jnp.sum(qh.astype(F32) * n_old, axis=1, keepdims=True)
        den = jnp.sum(a, axis=1, keepdims=True) + dec * qn
        hx = num / jnp.maximum(jnp.abs(den), jnp.exp(-m_t))

        b_last = b_col[L - 1:L, :]
        g_col = b_last - b_col + ig_col
        m_new = jnp.maximum(b_last + m_old, jnp.max(g_col, axis=0, keepdims=True))
        w_col = jnp.exp(g_col - m_new)
        cd = jnp.exp(b_last + m_old - m_new)
        kw = kh.astype(F32) * w_col
        c_scr[h] = cd * c_old + lax.dot_general(kw.astype(BF16), vh, (((0,), (0,)), ((), ())),
                                                preferred_element_type=F32)
        n_scr[h:h + 1, :] = cd * n_old + jnp.sum(kw, axis=0, keepdims=True)
        m_scr[h:h + 1, :] = jnp.broadcast_to(m_new, (1, LANE))

        sl = slice(h * ML_DV, (h + 1) * ML_DV)
        hn = _rms(hx, og_ref[:, sl])
        y_ref[:, sl] = (jax.nn.sigmoid(o_ref[:, sl].astype(F32)) * hn).astype(y_ref.dtype)


def _mlstm(qkvo, gates, out_g):
    s = qkvo.shape[0]
    L = min(ML_CHUNK, s)
    assert s % L == 0
    wq = ML_HEADS * ML_DK
    wv = ML_HEADS * ML_DV
    return pl.pallas_call(
        _mlstm_kernel,
        grid=(s // L,),
        in_specs=[
            pl.BlockSpec((L, wq), lambda c: (c, 0)),
            pl.BlockSpec((L, wq), lambda c: (c, 1)),
            pl.BlockSpec((L, wv), lambda c: (c, 1)),
            pl.BlockSpec((L, wv), lambda c: (c, 2)),
            pl.BlockSpec((L, LANE), lambda c: (c, 0)),
            pl.BlockSpec((1, wv), lambda c: (0, 0)),
        ],
        out_specs=pl.BlockSpec((L, wv), lambda c: (c, 0)),
        out_shape=jax.ShapeDtypeStruct((s, wv), BF16),
        scratch_shapes=[
            pltpu.VMEM((ML_HEADS, ML_DK, ML_DV), F32),
            pltpu.VMEM((ML_HEADS, ML_DK), F32),
            pltpu.VMEM((ML_HEADS, LANE), F32),
        ],
        compiler_params=_params("arbitrary"),
        name="mlstm_scan",
    )(qkvo, qkvo, qkvo, qkvo, gates, out_g)


def _pad_cols(a, n):
    return jnp.pad(a, ((0, 0), (0, n - a.shape[1])))


def _nsa_layer(x, norm_g, w_in, b_gate, q_g, k_g, cmp_pos, cmp_w1, cmp_b1, cmp_w2, w_out, *, tm, tq, tk):
    s, d = x.shape
    qd = NSA_HEADS * DH
    kvd = NSA_GROUPS * DH
    norm_g = norm_g.reshape(1, d)

    w_main = w_in[:, :qd + 6 * kvd].astype(BF16)
    ones = jnp.ones((kvd,), F32)
    gain = jnp.concatenate([jnp.tile(q_g, NSA_HEADS) * (DH ** -0.5 * LOG2E), ones, ones,
                            jnp.tile(k_g[1], NSA_GROUPS), ones, jnp.tile(k_g[2], NSA_GROUPS), ones])
    flag = jnp.concatenate([jnp.ones((qd,), F32), 0 * ones, 0 * ones, ones, 0 * ones, ones, 0 * ones])
    w_gate = _pad_cols(w_in[:, qd + 6 * kvd:], LANE).astype(BF16)
    bias = _pad_cols(b_gate.reshape(1, -1), LANE)
    proj, gates = _proj(x, norm_g, w_main, gain.reshape(1, -1), flag.reshape(1, -1), w_gate, bias,
                        mode="headnorm", gate_mode="sigmoid", tm=min(2 * tm, s), tn=1024, name="nsa_proj")

    nc = s // CMP_STRIDE
    xc = proj[:, qd:qd + 2 * kvd].reshape(nc, CMP_STRIDE, 2, NSA_GROUPS, DH)
    xc = jnp.transpose(xc, (2, 3, 0, 1, 4)).reshape(2, NSA_GROUPS, nc, CMP_STRIDE * DH)
    kvc = _compress(xc, cmp_w1.astype(BF16), cmp_b1.reshape(2, 1, DH), cmp_w2.astype(BF16),
                    cmp_pos.reshape(2, 1, CMP_LEN * DH).astype(BF16), k_g[0].reshape(1, DH))

    n_sel = s // SEL_BLOCK
    nselp = -(-n_sel // LANE) * LANE
    cstart = jnp.arange(nc) * CMP_STRIDE
    sstart = jnp.arange(nselp) * SEL_BLOCK
    overlap_t = ((cstart[None, :] < sstart[:, None] + SEL_BLOCK)
                 & (cstart[None, :] + CMP_LEN > sstart[:, None])
                 & (jnp.arange(nselp)[:, None] < n_sel)
                 & (jnp.arange(nc)[None, :] < nc - 1)).astype(BF16)
    o_cmp, notsel = _cmp_attention(proj, kvc[0], kvc[1], overlap_t, tq=tq)

    blk = jnp.arange(min(s, SEL_BLOCK * LANE)) // SEL_BLOCK
    expand =jnp.where(blk[:, None] == jnp.arange(LANE)[None, :], MASK_BIAS, 0.0).astype(BF16)
    col0 = qd // DH
    o_sel = _sel_attention(proj, notsel, proj, expand, k_col=col0 + 2 * NSA_GROUPS,
                           v_col=col0 + 3 * NSA_GROUPS, tq=min(2 * tq, s), tk=tk)
    mixed = _win_attention(proj, proj, o_cmp, o_sel, gates, k_blk=(qd + 4 * kvd) // kvd,
                           v_blk=(qd + 5 * kvd) // kvd, tq=tq)
    return _matmul_res(mixed, w_out.astype(BF16), x, tm=tm, name="nsa_out_proj")


def _mlstm_layer(x, norm_g, w_in, b_if, out_g, w_out, *, tm):
    s, d = x.shape
    norm_g = norm_g.reshape(1, d)
    wq = ML_HEADS * ML_DK
    wv = ML_HEADS * ML_DV
    nmain = 2 * wq + 2 * wv
    w_main = w_in[:, :nmain].astype(BF16)
    scale = jnp.concatenate([jnp.ones((wq,), F32), jnp.full((wq,), ML_DK ** -0.5, F32),
                             jnp.ones((2 * wv,), F32)]).reshape(1, -1)
    w_gate = _pad_cols(w_in[:, nmain:], LANE).astype(BF16)
    bias = _pad_cols(b_if.reshape(1, -1), LANE)
    qkvo, gates = _proj(x, norm_g, w_main, scale, scale, w_gate, bias, mode="scale", gate_mode="bias",
                        tm=min(2 * tm, s), tn=1024, name="ml_proj")
    y = _mlstm(qkvo, gates, out_g.reshape(1, -1))
    return _matmul_res(y, w_out.astype(BF16), x, tm=tm, name="ml_out_proj")


def _ffn_layer(x, norm_g, wg, wu, wd, *, tm):
    return _ffn(x, norm_g.reshape(1, -1), wg.astype(BF16), wu.astype(BF16), wd.astype(BF16), tm=tm, tf=512)


def kernel(x, norm_mix_g, norm_ffn_g, nsa_w_in, nsa_b_gate, nsa_q_norm_g, nsa_k_norm_g, nsa_cmp_pos,
           nsa_cmp_w1, nsa_cmp_b1, nsa_cmp_w2, nsa_w_out, ml_w_in, ml_b_if, ml_out_norm_g, ml_w_out,
           ffn_w_gate, ffn_w_up, ffn_w_down):
    b, s, d = x.shape
    depth = norm_mix_g.shape[0]
    tm = min(512, s)
    tq = 128
    tk = min(1024, s)
    outs = []
    for bi in range(b):
        xb = x[bi]
        for i in range(depth):
            j = i // 2
            if i % 2 == 0:
                xb = _nsa_layer(xb, norm_mix_g[i], nsa_w_in[j], nsa_b_gate[j], nsa_q_norm_g[j],
                                nsa_k_norm_g[j], nsa_cmp_pos[j], nsa_cmp_w1[j], nsa_cmp_b1[j],
                                nsa_cmp_w2[j], nsa_w_out[j], tm=tm, tq=tq, tk=tk)
            else:
                xb = _mlstm_layer(xb, norm_mix_g[i], ml_w_in[j], ml_b_if[j], ml_out_norm_g[j],
                                  ml_w_out[j], tm=tm)
            xb = _ffn_layer(xb, norm_ffn_g[i], ffn_w_gate[i], ffn_w_up[i], ffn_w_down[i], tm=min(2 * tm, s))
        outs.append(xb)
    return jnp.stack(outs, axis=0)
```

```python
import functools

import jax
import jax.numpy as jnp
from jax import lax
from jax.experimental import pallas as pl
from jax.experimental.pallas import tpu as pltpu

F32 = jnp.float32
BF16 = jnp.bfloat16

EPS = 1e-6
NEG_INIT = -1e30
LOG2E = 1.4426950408889634

LANE = 128
VMEM_LIMIT = 56 * 1024 * 1024
PROJ_CHUNK = 512

NSA_HEADS = 16
NSA_GROUPS = 4
NSA_REP = NSA_HEADS // NSA_GROUPS
DH = 128
CMP_LEN = 32
CMP_STRIDE = 16
SEL_BLOCK = 64
N_SELECT = 16
WINDOW = 512
ML_HEADS = 8
ML_DK = 128
ML_DV = 256
ML_CHUNK = 512

MASK_BIAS = -(2.0 ** 100)


def _params(*sem):
    return pltpu.CompilerParams(dimension_semantics=sem, vmem_limit_bytes=VMEM_LIMIT)


def _rms(x, g):
    ms = jnp.mean(x * x, axis=-1, keepdims=True)
    return x * lax.rsqrt(ms + EPS) * g


def _proj_kernel(x_ref, g_ref, w_ref, a_ref, b_ref, wg_ref, bg_ref, o_ref, og_ref, h_scr, *, mode, gate_mode):
    @pl.when(pl.program_id(1) == 0)
    def _():
        h = _rms(x_ref[...], g_ref[...]).astype(BF16)
        h_scr[...] = h
        gl = jnp.dot(h, wg_ref[...], preferred_element_type=F32) + bg_ref[...]
        og_ref[...] = jax.nn.sigmoid(gl) if gate_mode == "sigmoid" else gl

    h = h_scr[...]
    for c in range(o_ref.shape[1] // PROJ_CHUNK):
        cs = slice(c * PROJ_CHUNK, (c + 1) * PROJ_CHUNK)
        y = jnp.dot(h, w_ref[:, cs], preferred_element_type=F32)
        if mode == "headnorm":
            for u in range(PROJ_CHUNK // LANE):
                sl = slice(c * PROJ_CHUNK + u * LANE, c * PROJ_CHUNK + (u + 1) * LANE)
                yc = y[:, u * LANE:(u + 1) * LANE]
                ms = jnp.mean(yc * yc, axis=-1, keepdims=True)
                mult = jnp.where(b_ref[:, sl] > 0.0, lax.rsqrt(ms + EPS), 1.0) * a_ref[:, sl]
                o_ref[:, sl] = (yc * mult).astype(o_ref.dtype)
        elif mode == "scale":
            o_ref[:, cs] = (y * a_ref[:, cs]).astype(o_ref.dtype)
        else:
            raise ValueError(mode)


def _proj(x, g, w, a, b, wg, bg, *, mode, gate_mode, tm, tn, name):
    s, d = x.shape
    n = w.shape[1]
    ng = wg.shape[1]
    assert s % tm == 0 and n % tn == 0 and tn % PROJ_CHUNK == 0
    return pl.pallas_call(
        functools.partial(_proj_kernel, mode=mode, gate_mode=gate_mode),
        grid=(s // tm, n // tn),
        in_specs=[
            pl.BlockSpec((tm, d), lambda i, j: (i, 0)),
            pl.BlockSpec((1, d), lambda i, j: (0, 0)),
            pl.BlockSpec((d, tn), lambda i, j: (0, j)),
            pl.BlockSpec((1, tn), lambda i, j: (0, j)),
            pl.BlockSpec((1, tn), lambda i, j: (0, j)),
            pl.BlockSpec((d, ng), lambda i, j: (0, 0)),
            pl.BlockSpec((1, ng), lambda i, j: (0, 0)),
        ],
        out_specs=[
            pl.BlockSpec((tm, tn), lambda i, j: (i, j)),
            pl.BlockSpec((tm, ng), lambda i, j: (i, 0)),
        ],
        out_shape=[
            jax.ShapeDtypeStruct((s, n), BF16),
            jax.ShapeDtypeStruct((s, ng), F32),
        ],
        scratch_shapes=[pltpu.VMEM((tm, d), BF16)],
        compiler_params=_params("parallel", "arbitrary"),
        name=name,
    )(x, g, w, a, b, wg, bg)


def _matmul_res_kernel(a_ref, w_ref, r_ref, o_ref):
    a = a_ref[...]
    for c in range(o_ref.shape[1] // PROJ_CHUNK):
        cs = slice(c * PROJ_CHUNK, (c + 1) * PROJ_CHUNK)
        o_ref[:, cs] = r_ref[:, cs] + jnp.dot(a, w_ref[:, cs], preferred_element_type=F32)


def _matmul_res(a, w, res, *, tm, name):
    s, k = a.shape
    n = w.shape[1]
    assert s % tm == 0 and n % PROJ_CHUNK == 0
    return pl.pallas_call(
        _matmul_res_kernel,
        grid=(s // tm,),
        in_specs=[
            pl.BlockSpec((tm, k), lambda i: (i, 0)),
            pl.BlockSpec((k, n), lambda i: (0, 0), pipeline_mode=pl.Buffered(1)),
            pl.BlockSpec((tm, n), lambda i: (i, 0)),
        ],
        out_specs=pl.BlockSpec((tm, n), lambda i: (i, 0)),
        out_shape=jax.ShapeDtypeStruct((s, n), F32),
        compiler_params=_params("parallel"),
        name=name,
    )(a, w, res)


def _ffn_kernel(x_ref, g_ref, wg_ref, wu_ref, wd_ref, o_ref, h_scr):
    @pl.when(pl.program_id(1) == 0)
    def _():
        x = x_ref[...]
        h_scr[...] = _rms(x, g_ref[...]).astype(BF16)
        o_ref[...] = x

    half = h_scr.shape[0] // 2
    for u in range(2):
        rs = slice(u * half, (u + 1) * half)
        h = h_scr[rs, :]
        gate = jnp.dot(h, wg_ref[...], preferred_element_type=F32)
        up = jnp.dot(h, wu_ref[...], preferred_element_type=F32)
        act = (gate * jax.nn.sigmoid(gate) * up).astype(BF16)
        o_ref[rs, :] += jnp.dot(act, wd_ref[...], preferred_element_type=F32)


def _ffn(x, g, wg, wu, wd, *, tm, tf):
    s, d = x.shape
    dff = wg.shape[1]
    assert s % tm == 0 and dff % tf == 0
    return pl.pallas_call(
        _ffn_kernel,
        grid=(s // tm, dff // tf),
        in_specs=[
            pl.BlockSpec((tm, d), lambda i, f: (i, 0)),
            pl.BlockSpec((1, d), lambda i, f: (0, 0)),
            pl.BlockSpec((d, tf), lambda i, f: (0, f)),
            pl.BlockSpec((d, tf), lambda i, f: (0, f)),
            pl.BlockSpec((tf, d), lambda i, f: (f, 0)),
        ],
        out_specs=pl.BlockSpec((tm, d), lambda i, f: (i, 0)),
        out_shape=jax.ShapeDtypeStruct((s, d), F32),
        scratch_shapes=[pltpu.VMEM((tm, d), BF16)],
        compiler_params=_params("parallel", "arbitrary"),
        name="ffn",
    )(x, g, wg, wu, wd)


def _compress_kernel(x_ref, w1_ref, b1_ref, w2_ref, pos_ref, kg_ref, o_ref):
    half = CMP_STRIDE * DH
    x = x_ref[0, 0]
    nc = x.shape[0]
    top = jnp.dot(x, w1_ref[0, :half, :], preferred_element_type=F32)
    bot = jnp.dot(x, w1_ref[0, half:, :], preferred_element_type=F32)
    bot = pltpu.roll(bot, nc - 1, 0)
    row = lax.broadcasted_iota(jnp.int32, bot.shape, 0)
    bot = jnp.where(row == nc - 1, 0.0, bot)
    pos8 = jnp.broadcast_to(pos_ref[0], (8, 2 * half))
    posb = jnp.dot(pos8, w1_ref[0], preferred_element_type=F32)[0:1]
    hdn = jax.nn.gelu(top + bot + posb + b1_ref[0])
    y = jnp.dot(hdn.astype(BF16), w2_ref[0], preferred_element_type=F32)
    yn = _rms(y, kg_ref[...])
    o_ref[0, 0] = jnp.where(pl.program_id(0) == 0, yn, y).astype(o_ref.dtype)


def _compress(xc, w1, b1, w2, pos, kg):
    _, g, nc, k = xc.shape
    return pl.pallas_call(
        _compress_kernel,
        grid=(2, g),
        in_specs=[
            pl.BlockSpec((1, 1, nc, k), lambda s, i: (s, i, 0, 0)),
            pl.BlockSpec((1, 2 * k, DH), lambda s, i: (s, 0, 0)),
            pl.BlockSpec((1, 1, DH), lambda s, i: (s, 0, 0)),
            pl.BlockSpec((1, DH, DH), lambda s, i: (s, 0, 0)),
            pl.BlockSpec((1, 1, 2 * k), lambda s, i: (s, 0, 0)),
            pl.BlockSpec((1, DH), lambda s, i: (0, 0)),
        ],
        out_specs=pl.BlockSpec((1, 1, nc, DH), lambda s, i: (s, i, 0, 0)),
        out_shape=jax.ShapeDtypeStruct((2, g, nc, DH), BF16),
        compiler_params=_params("parallel", "parallel"),
        name="nsa_compress",
    )(xc, w1, b1, w2, pos, kg)


def _stack_heads(qb):
    return jnp.concatenate([qb[:, r * DH:(r + 1) * DH] for r in range(NSA_REP)], axis=0)


def _cmp_kernel(q_ref, kc_ref, vc_ref, ovt_ref, o_ref, mn_ref, imp_scr, *, tq, ktop, col_steps):
    qi = pl.program_id(0)
    gw = NSA_REP * DH

    def attend(ncols, nb):
        for g in range(NSA_GROUPS):
            q4 = _stack_heads(q_ref[:, g * gw:(g + 1) * gw])
            s = lax.dot_general(q4, kc_ref[g, 0:ncols, :], (((1,), (1,)), ((), ())),
                                preferred_element_type=F32)
            row = lax.broadcasted_iota(jnp.int32, s.shape, 0)
            col = lax.broadcasted_iota(jnp.int32, s.shape, 1)
            t = qi * tq + (row & (tq - 1))
            s = jnp.where(col * CMP_STRIDE + (CMP_LEN - 1) <= t, s, -jnp.inf)
            mx = jnp.max(s, axis=-1, keepdims=True)
            mx = jnp.where(jnp.abs(mx) < jnp.inf, mx, 0.0)
            p = jnp.exp2(s - mx)
            p = p / jnp.maximum(jnp.sum(p, axis=-1, keepdims=True), 1e-30)
            o = jnp.dot(p.astype(BF16), vc_ref[g, 0:ncols, :], preferred_element_type=F32)
            for r in range(NSA_REP):
                o_ref[:, g * gw + r * DH:g * gw + (r + 1) * DH] = o[r * tq:(r + 1) * tq].astype(o_ref.dtype)
            ps = p[0:tq]
            for r in range(1, NSA_REP):
                ps = ps + p[r * tq:(r + 1) * tq]
            imp_scr[g, 0:nb, :] = lax.dot_general(ovt_ref[0:nb, 0:ncols], ps.astype(BF16),
                                                  (((1,), (1,)), ((), ())), preferred_element_type=F32)

    def select(nb):
        nselp = imp_scr.shape[1]
        jj = lax.broadcasted_iota(jnp.int32, (nb, tq), 0)
        cur = (qi * tq + lax.broadcasted_iota(jnp.int32, (nb, tq), 1)) // SEL_BLOCK
        forced = (jj == 0) | (jj == cur) | (jj == cur - 1)
        free = (jj >= 1) & (jj <= cur - 2)
        jjf = jj.astype(F32)
        scores = [jnp.where(free, imp_scr[g, 0:nb, :], -jnp.inf) for g in range(NSA_GROUPS)]
        for _ in range(ktop - 3):
            for g in range(NSA_GROUPS):
                top = jnp.max(scores[g], axis=0, keepdims=True)
                first = jnp.min(jnp.where(scores[g] == top, jjf, 1e9), axis=0, keepdims=True)
                scores[g] = jnp.where(jjf == first, -jnp.inf, scores[g])
        for g in range(NSA_GROUPS):
            picked = forced | (free & (scores[g] == -jnp.inf))
            notsel = jnp.where(picked, 0.0, 1.0)
            if nb < nselp:
                notsel = jnp.concatenate([notsel, jnp.ones((nselp - nb, tq), F32)], axis=0)
            mn_ref[g] = notsel.T.astype(mn_ref.dtype)

    def variant(ncols):
        nb = min(imp_scr.shape[1], -(-(ncols * CMP_STRIDE // SEL_BLOCK) // 8) * 8)
        attend(ncols, nb)
        select(nb)

    needed = ((qi + 1) * tq - CMP_LEN) // CMP_STRIDE + 1
    lo = 0
    for ncols in col_steps:
        pl.when((needed > lo) & (needed <= ncols))(functools.partial(variant, ncols))
        lo = ncols


def _cmp_attention(q, kc, vc, overlap_t, *, tq):
    s = q.shape[0]
    g, nc, _ = kc.shape
    nselp = overlap_t.shape[0]
    ktop = min(N_SELECT, s // SEL_BLOCK)
    assert ktop >= 3
    gw = NSA_REP * DH
    col_steps = tuple(range(2 * LANE, nc + 1, 2 * LANE)) if nc % (2 * LANE) == 0 else (nc,)
    return pl.pallas_call(
        functools.partial(_cmp_kernel, tq=tq, ktop=ktop, col_steps=col_steps),
        grid=(s // tq,),
        in_specs=[
            pl.BlockSpec((tq, g * gw), lambda i: (i, 0)),
            pl.BlockSpec((g, nc, DH), lambda i: (0, 0, 0)),
            pl.BlockSpec((g, nc, DH), lambda i: (0, 0, 0)),
            pl.BlockSpec((nselp, nc), lambda i: (0, 0)),
        ],
        out_specs=[
            pl.BlockSpec((tq, g * gw), lambda i: (i, 0)),
            pl.BlockSpec((g, tq, nselp), lambda i: (0, i, 0)),
        ],
        out_shape=[
            jax.ShapeDtypeStruct((s, g * gw), BF16),
            jax.ShapeDtypeStruct((g, s, nselp), BF16),
        ],
        scratch_shapes=[pltpu.VMEM((g, nselp, tq), F32)],
        compiler_params=_params("parallel"),
        name="nsa_cmp_select",
    )(q, kc, vc, overlap_t)


def _sel_kernel(q_ref, mn_ref, k_ref, v_ref, e_ref, o_ref, ka_scr, va_scr, qa_scr, s_scr, m_scr, acc_scr,
                *, tq, tk):
    qi = pl.program_id(1)
    nhalf = qa_scr.shape[0]
    nper = e_ref.shape[0]

    @pl.when(qi == 0)
    def _():
        for c in range(ka_scr.shape[0] // nper):
            rows = slice(c * nper, (c + 1) * nper)
            ka_scr[rows, 0:DH] = k_ref[rows, :]
            ka_scr[rows, DH:2 * DH] = e_ref[...]
            va_scr[rows, 0:DH] = v_ref[rows, :]
            va_scr[rows, DH:2 * DH] = jnp.ones((nper, DH), BF16)

    q4 = _stack_heads(q_ref[...])
    mn = mn_ref[0]
    for hf in range(nhalf):
        part = mn[:, hf * LANE:(hf + 1) * LANE]
        qa_scr[hf, :, 0:DH] = q4
        qa_scr[hf, :, DH:2 * DH] = jnp.concatenate([part] * NSA_REP, axis=0)
    m_scr[...] = jnp.full_like(m_scr, -jnp.inf)
    acc_scr[...] = jnp.zeros_like(acc_scr)

    def scores(j, slot):
        k0 = pl.multiple_of(j * tk, tk)
        s_scr[slot] = lax.dot_general(qa_scr[k0 // (SEL_BLOCK * LANE)], ka_scr[pl.ds(k0, tk), :],
                                      (((1,), (1,)), ((), ())), preferred_element_type=F32)

    def accumulate(j, slot, masked):
        k0 = pl.multiple_of(j * tk, tk)
        s = s_scr[slot]
        if masked:
            row = lax.broadcasted_iota(jnp.int32, s.shape, 0)
            col = lax.broadcasted_iota(jnp.int32, s.shape, 1)
            s = jnp.where(k0 + col <= qi * tq + (row & (tq - 1)), s, -jnp.inf)
        m_old = m_scr[...]
        m_new = jnp.maximum(m_old, jnp.max(s, axis=-1, keepdims=True))
        p = jnp.exp2(s - m_new).astype(BF16)
        acc_scr[...] = (jnp.exp2(m_old - m_new) * acc_scr[...]
                        + jnp.dot(p, va_scr[pl.ds(k0, tk), :], preferred_element_type=F32))
        m_scr[...] = m_new

    n = (qi * tq + tq - 1) // tk + 1
    unroll = 4
    ntrip = (n - 1) // unroll
    scores(0, 0)

    def trip(i, c):
        for u in range(unroll):
            scores(unroll * i + u + 1, (u + 1) % 2)
            accumulate(unroll * i + u, u % 2, False)
        return c

    lax.fori_loop(0, ntrip, trip, 0)

    first = ntrip * unroll
    for rest in range(1, unroll + 1):
        @pl.when(n - first == rest)
        def _(rest=rest):
            for u in range(rest):
                if u + 1 < rest:
                    scores(first + u + 1, (u + 1) % 2)
                accumulate(first + u, u % 2, u + 1 == rest)

    acc = acc_scr[...]
    o = acc[:, 0:DH] / jnp.maximum(acc[:, DH:2 * DH], 1e-30)
    for r in range(NSA_REP):
        o_ref[:, r * DH:(r + 1) * DH] = o[r * tq:(r + 1) * tq].astype(o_ref.dtype)


def _sel_attention(q, notsel, kv_arr, expand, *, k_col, v_col, tq, tk):
    s = q.shape[0]
    g = notsel.shape[0]
    nselp = notsel.shape[2]
    gw = NSA_REP * DH
    nper = expand.shape[0]
    assert s % nper == 0 and nper % tk == 0 and tk % tq == 0 and s % tq == 0
    once = pl.Buffered(1)
    return pl.pallas_call(
        functools.partial(_sel_kernel, tq=tq, tk=tk),
        grid=(g, s // tq),
        in_specs=[
            pl.BlockSpec((tq, gw), lambda gi, i: (i, gi)),
            pl.BlockSpec((1, tq, nselp), lambda gi, i: (gi, i, 0)),
            pl.BlockSpec((s, DH), lambda gi, i: (0, k_col + gi), pipeline_mode=once),
            pl.BlockSpec((s, DH), lambda gi, i: (0, v_col + gi), pipeline_mode=once),
            pl.BlockSpec((nper, LANE), lambda gi, i: (0, 0), pipeline_mode=once),
        ],
        out_specs=pl.BlockSpec((tq, gw), lambda gi, i: (i, gi)),
        out_shape=jax.ShapeDtypeStruct((s, g * gw), BF16),
        scratch_shapes=[
            pltpu.VMEM((s, 2 * DH), BF16),
            pltpu.VMEM((s, 2 * DH), BF16),
            pltpu.VMEM((nselp // LANE, NSA_REP * tq, 2 * DH), BF16),
            pltpu.VMEM((2, NSA_REP * tq, tk), F32),
            pltpu.VMEM((NSA_REP * tq, 1), F32),
            pltpu.VMEM((NSA_REP * tq, 2 * DH), F32),
        ],
        compiler_params=_params("arbitrary", "arbitrary"),
        name="nsa_sel_attention",
    )(q, notsel, kv_arr, kv_arr, expand)


def _win_kernel(q_ref, *refs, tq, nback):
    nblk = nback + 1
    k_refs = refs[:nblk]
    v_refs = refs[nblk:2 * nblk]
    band_ref, oc_ref, os_ref, gt_ref, o_ref = refs[2 * nblk:]
    qi = pl.program_id(0)
    gw = NSA_REP * DH
    band = band_ref[...]
    col = lax.broadcasted_iota(jnp.int32, band.shape, 1)
    band = jnp.where(col >= (nback - qi) * tq, band, -jnp.inf)
    for g in range(NSA_GROUPS):
        q4 = _stack_heads(q_ref[:, g * gw:(g + 1) * gw])
        kc = jnp.concatenate([r[:, g * DH:(g + 1) * DH] for r in k_refs], axis=0)
        vc = jnp.concatenate([r[:, g * DH:(g + 1) * DH] for r in v_refs], axis=0)
        s = lax.dot_general(q4, kc, (((1,), (1,)), ((), ())), preferred_element_type=F32) + band
        mx = jnp.max(s, axis=-1, keepdims=True)
        mx = jnp.where(jnp.abs(mx) < jnp.inf, mx, 0.0)
        p = jnp.exp2(s - mx)
        p = p / jnp.maximum(jnp.sum(p, axis=-1, keepdims=True), 1e-30)
        ow = jnp.dot(p.astype(BF16), vc, preferred_element_type=F32)
        gates = gt_ref[...]
        for r in range(NSA_REP):
            sl = slice(g * gw + r * DH, g * gw + (r + 1) * DH)
            c0 = 3 * (g * NSA_REP + r)
            out = (gates[:, c0:c0 + 1] * oc_ref[:, sl].astype(F32)
                   + gates[:, c0 + 1:c0 + 2] * os_ref[:, sl].astype(F32)
                   + gates[:, c0 + 2:c0 + 3] * ow[r * tq:(r + 1) * tq])
            o_ref[:, sl] = out.astype(o_ref.dtype)


def _win_attention(q, kv_arr, o_cmp, o_sel, gates, *, k_blk, v_blk, tq):
    s = q.shape[0]
    g = NSA_GROUPS
    gw = NSA_REP * DH
    nback = WINDOW // tq
    assert nback * tq == WINDOW
    qq = jnp.arange(NSA_REP * tq)[:, None] % tq
    kk = jnp.arange((nback + 1) * tq)[None, :]
    band = jnp.where((kk > qq) & (kk <= qq + WINDOW), 0.0, -jnp.inf).astype(F32)

    def kvmap(blk, b):
        def f(i):
            return (jnp.maximum(i - nback + b, 0), blk)
        return f

    k_specs = [pl.BlockSpec((tq, g * DH), kvmap(k_blk, b)) for b in range(nback + 1)]
    v_specs = [pl.BlockSpec((tq, g * DH), kvmap(v_blk, b)) for b in range(nback + 1)]
    wide = pl.BlockSpec((tq, g * gw), lambda i: (i, 0))
    return pl.pallas_call(
        functools.partial(_win_kernel, tq=tq, nback=nback),
        grid=(s // tq,),
        in_specs=([wide] + k_specs + v_specs
                  + [pl.BlockSpec(band.shape, lambda i: (0, 0)), wide, wide,
                     pl.BlockSpec((tq, LANE), lambda i: (i, 0))]),
        out_specs=wide,
        out_shape=jax.ShapeDtypeStruct((s, g * gw), BF16),
        compiler_params=_params("parallel"),
        name="nsa_win_combine",
    )(q, *([kv_arr] * (2 * (nback + 1))), band, o_cmp, o_sel, gates)


def _mlstm_kernel(q_ref, k_ref, v_ref, o_ref, gt_ref, og_ref, y_ref, c_scr, n_scr, m_scr):
    L = q_ref.shape[0]

    @pl.when(pl.program_id(0) == 0)
    def _():
        c_scr[...] = jnp.zeros_like(c_scr)
        n_scr[...] = jnp.zeros_like(n_scr)
        m_scr[...] = jnp.full_like(m_scr, NEG_INIT)

    ri = lax.broadcasted_iota(jnp.int32, (L, L), 0)
    ci = lax.broadcasted_iota(jnp.int32, (L, L), 1)
    eye = ri == ci
    tril = ci <= ri
    triu = ri <= ci
    gates = gt_ref[...]

    def to_row(col):
        return jnp.sum(jnp.where(eye, col, 0.0), axis=0, keepdims=True)

    for h in range(ML_HEADS):
        qh = q_ref[:, h * ML_DK:(h + 1) * ML_DK]
        kh = k_ref[:, h * ML_DK:(h + 1) * ML_DK]
        vh = v_ref[:, h * ML_DV:(h + 1) * ML_DV]
        ig_col = gates[:, h:h + 1]
        fg_col = gates[:, ML_HEADS + h:ML_HEADS + h + 1]
        lf_col = jnp.minimum(fg_col, 0.0) - jnp.log(1.0 + jnp.exp(-jnp.abs(fg_col)))
        lf_row = to_row(lf_col)
        ig_row = to_row(ig_col)
        b_col = jnp.sum(jnp.where(tril, lf_row, 0.0), axis=1, keepdims=True)
        b_row = jnp.sum(jnp.where(triu, lf_col, 0.0), axis=0, keepdims=True)
        m_old = m_scr[h:h + 1, 0:1]
        dmat = jnp.where(tril, b_col - b_row + ig_row, -jnp.inf)
        m_inter = b_col + m_old
        m_t = jnp.maximum(m_inter, jnp.max(dmat, axis=1, keepdims=True))
        qk = lax.dot_general(qh, kh, (((1,), (1,)), ((), ())), preferred_element_type=F32)
        a = jnp.exp(dmat - m_t) * qk
        dec = jnp.exp(m_inter - m_t)
        c_old = c_scr[h]
        n_old = n_scr[h:h + 1, :]
        num = (jnp.dot(a.astype(BF16), vh, preferred_element_type=F32)
               + dec * jnp.dot(qh, c_old.astype(BF16), preferred_element_type=F32))
        qn = jnp.sum(qh.astype(F32) * n_old, axis=1, keepdims=True)
        den = jnp.sum(a, axis=1, keepdims=True) + dec * qn
        hx = num / jnp.maximum(jnp.abs(den), jnp.exp(-m_t))

        b_last = b_col[L - 1:L, :]
        g_col = b_last - b_col + ig_col
        m_new = jnp.maximum(b_last + m_old, jnp.max(g_col, axis=0, keepdims=True))
        w_col = jnp.exp(g_col - m_new)
        cd = jnp.exp(b_last + m_old - m_new)
        kw = kh.astype(F32) * w_col
        c_scr[h] = cd * c_old + lax.dot_general(kw.astype(BF16), vh, (((0,), (0,)), ((), ())),
                                                preferred_element_type=F32)
        n_scr[h:h + 1, :] = cd * n_old + jnp.sum(kw, axis=0, keepdims=True)
        m_scr[h:h + 1, :] = jnp.broadcast_to(m_new, (1, LANE))

        sl = slice(h * ML_DV, (h + 1) * ML_DV)
        hn = _rms(hx, og_ref[:, sl])
        y_ref[:, sl] = (jax.nn.sigmoid(o_ref[:, sl].astype(F32)) * hn).astype(y_ref.dtype)


def _mlstm(qkvo, gates, out_g):
    s = qkvo.shape[0]
    L = min(ML_CHUNK, s)
    assert s % L == 0
    wq = ML_HEADS * ML_DK
    wv = ML_HEADS * ML_DV
    return pl.pallas_call(
        _mlstm_kernel,
        grid=(s // L,),
        in_specs=[
            pl.BlockSpec((L, wq), lambda c: (c, 0)),
            pl.BlockSpec((L, wq), lambda c: (c, 1)),
            pl.BlockSpec((L, wv), lambda c: (c, 1)),
            pl.BlockSpec((L, wv), lambda c: (c, 2)),
            pl.BlockSpec((L, LANE), lambda c: (c, 0)),
            pl.BlockSpec((1, wv), lambda c: (0, 0)),
        ],
        out_specs=pl.BlockSpec((L, wv), lambda c: (c, 0)),
        out_shape=jax.ShapeDtypeStruct((s, wv), BF16),
        scratch_shapes=[
            pltpu.VMEM((ML_HEADS, ML_DK, ML_DV), F32),
            pltpu.VMEM((ML_HEADS, ML_DK), F32),
            pltpu.VMEM((ML_HEADS, LANE), F32),
        ],
        compiler_params=_params("arbitrary"),
        name="mlstm_scan",
    )(qkvo, qkvo, qkvo, qkvo, gates, out_g)


def _pad_cols(a, n):
    return jnp.pad(a, ((0, 0), (0, n - a.shape[1])))


def _nsa_layer(x, norm_g, w_in, b_gate, q_g, k_g, cmp_pos, cmp_w1, cmp_b1, cmp_w2, w_out, *, tm, tq, tk):
    s, d = x.shape
    qd = NSA_HEADS * DH
    kvd = NSA_GROUPS * DH
    norm_g = norm_g.reshape(1, d)

    w_main = w_in[:, :qd + 6 * kvd].astype(BF16)
    ones = jnp.ones((kvd,), F32)
    gain = jnp.concatenate([jnp.tile(q_g, NSA_HEADS) * (DH ** -0.5 * LOG2E), ones, ones,
                            jnp.tile(k_g[1], NSA_GROUPS), ones, jnp.tile(k_g[2], NSA_GROUPS), ones])
    flag = jnp.concatenate([jnp.ones((qd,), F32), 0 * ones, 0 * ones, ones, 0 * ones, ones, 0 * ones])
    w_gate = _pad_cols(w_in[:, qd + 6 * kvd:], LANE).astype(BF16)
    bias = _pad_cols(b_gate.reshape(1, -1), LANE)
    proj, gates = _proj(x, norm_g, w_main, gain.reshape(1, -1), flag.reshape(1, -1), w_gate, bias,
                        mode="headnorm", gate_mode="sigmoid", tm=min(2 * tm, s), tn=1024, name="nsa_proj")

    nc = s // CMP_STRIDE
    xc = proj[:, qd:qd + 2 * kvd].reshape(nc, CMP_STRIDE, 2, NSA_GROUPS, DH)
    xc = jnp.transpose(xc, (2, 3, 0, 1, 4)).reshape(2, NSA_GROUPS, nc, CMP_STRIDE * DH)
    kvc = _compress(xc, cmp_w1.astype(BF16), cmp_b1.reshape(2, 1, DH), cmp_w2.astype(BF16),
                    cmp_pos.reshape(2, 1, CMP_LEN * DH).astype(BF16), k_g[0].reshape(1, DH))

    n_sel = s // SEL_BLOCK
    nselp = -(-n_sel // LANE) * LANE
    cstart = jnp.arange(nc) * CMP_STRIDE
    sstart = jnp.arange(nselp) * SEL_BLOCK
    overlap_t = ((cstart[None, :] < sstart[:, None] + SEL_BLOCK)
                 & (cstart[None, :] + CMP_LEN > sstart[:, None])
                 & (jnp.arange(nselp)[:, None] < n_sel)
                 & (jnp.arange(nc)[None, :] < nc - 1)).astype(BF16)
    o_cmp, notsel = _cmp_attention(proj, kvc[0], kvc[1], overlap_t, tq=tq)

    blk = jnp.arange(min(s, SEL_BLOCK * LANE)) // SEL_BLOCK
    expand =jnp.where(blk[:, None] == jnp.arange(LANE)[None, :], MASK_BIAS, 0.0).astype(BF16)
    col0 = qd // DH
    o_sel = _sel_attention(proj, notsel, proj, expand, k_col=col0 + 2 * NSA_GROUPS,
                           v_col=col0 + 3 * NSA_GROUPS, tq=min(2 * tq, s), tk=tk)
    mixed = _win_attention(proj, proj, o_cmp, o_sel, gates, k_blk=(qd + 4 * kvd) // kvd,
                           v_blk=(qd + 5 * kvd) // kvd, tq=tq)
    return _matmul_res(mixed, w_out.astype(BF16), x, tm=tm, name="nsa_out_proj")


def _mlstm_layer(x, norm_g, w_in, b_if, out_g, w_out, *, tm):
    s, d = x.shape
    norm_g = norm_g.reshape(1, d)
    wq = ML_HEADS * ML_DK
    wv = ML_HEADS * ML_DV
    nmain = 2 * wq + 2 * wv
    w_main = w_in[:, :nmain].astype(BF16)
    scale = jnp.concatenate([jnp.ones((wq,), F32), jnp.full((wq,), ML_DK ** -0.5, F32),
                             jnp.ones((2 * wv,), F32)]).reshape(1, -1)
    w_gate = _pad_cols(w_in[:, nmain:], LANE).astype(BF16)
    bias = _pad_cols(b_if.reshape(1, -1), LANE)
    qkvo, gates = _proj(x, norm_g, w_main, scale, scale, w_gate, bias, mode="scale", gate_mode="bias",
                        tm=min(2 * tm, s), tn=1024, name="ml_proj")
    y = _mlstm(qkvo, gates, out_g.reshape(1, -1))
    return _matmul_res(y, w_out.astype(BF16), x, tm=tm, name="ml_out_proj")


def _ffn_layer(x, norm_g, wg, wu, wd, *, tm):
    return _ffn(x, norm_g.reshape(1, -1), wg.astype(BF16), wu.astype(BF16), wd.astype(BF16), tm=tm, tf=512)


def kernel(x, norm_mix_g, norm_ffn_g, nsa_w_in, nsa_b_gate, nsa_q_norm_g, nsa_k_norm_g, nsa_cmp_pos,
           nsa_cmp_w1, nsa_cmp_b1, nsa_cmp_w2, nsa_w_out, ml_w_in, ml_b_if, ml_out_norm_g, ml_w_out,
           ffn_w_gate, ffn_w_up, ffn_w_down):
    b, s, d = x.shape
    depth = norm_mix_g.shape[0]
    tm = min(512, s)
    tq = 128
    tk = min(1024, s)
    outs = []
    for bi in range(b):
        xb = x[bi]
        for i in range(depth):
            j = i // 2
            if i % 2 == 0:
                xb = _nsa_layer(xb, norm_mix_g[i], nsa_w_in[j], nsa_b_gate[j], nsa_q_norm_g[j],
                                nsa_k_norm_g[j], nsa_cmp_pos[j], nsa_cmp_w1[j], nsa_cmp_b1[j],
                                nsa_cmp_w2[j], nsa_w_out[j], tm=tm, tq=tq, tk=tk)
            else:
                xb = _mlstm_layer(xb, norm_mix_g[i], ml_w_in[j], ml_b_if[j], ml_out_norm_g[j],
                                  ml_w_out[j], tm=tm)
            xb = _ffn_layer(xb, norm_ffn_g[i], ffn_w_gate[i], ffn_w_up[i], ffn_w_down[i], tm=min(2 * tm, s))
        outs.append(xb)
    return jnp.stack(outs, axis=0)
```

```python
import functools

import jax
import jax.numpy as jnp
from jax import lax
from jax.experimental import pallas as pl
from jax.experimental.pallas import tpu as pltpu

F32 = jnp.float32
BF16 = jnp.bfloat16

EPS = 1e-6
NEG_INIT = -1e30
LOG2E = 1.4426950408889634

LANE = 128
VMEM_LIMIT = 56 * 1024 * 1024
PROJ_CHUNK = 512

NSA_HEADS = 16
NSA_GROUPS = 4
NSA_REP = NSA_HEADS // NSA_GROUPS
DH = 128
CMP_LEN = 32
CMP_STRIDE = 16
SEL_BLOCK = 64
N_SELECT = 16
WINDOW = 512
ML_HEADS = 8
ML_DK = 128
ML_DV = 256
ML_CHUNK = 512

MASK_BIAS = -(2.0 ** 100)


def _params(*sem):
    return pltpu.CompilerParams(dimension_semantics=sem, vmem_limit_bytes=VMEM_LIMIT)


def _rms(x, g):
    ms = jnp.mean(x * x, axis=-1, keepdims=True)
    return x * lax.rsqrt(ms + EPS) * g


def _proj_kernel(x_ref, g_ref, w_ref, a_ref, b_ref, wg_ref, bg_ref, o_ref, og_ref, *rest,
                 mode, gate_mode, chunk_step):
    if chunk_step is None:
        (h_scr,) = rest
    else:
        oc_ref, h_scr, y_scr = rest

    @pl.when(pl.program_id(1) == 0)
    def _():
        h = _rms(x_ref[...], g_ref[...]).astype(BF16)
        h_scr[...] = h
        gl = jnp.dot(h, wg_ref[...], preferred_element_type=F32) + bg_ref[...]
        og_ref[...] = jax.nn.sigmoid(gl) if gate_mode == "sigmoid" else gl

    h = h_scr[...]
    for c in range(o_ref.shape[1] // PROJ_CHUNK):
        cs = slice(c * PROJ_CHUNK, (c + 1) * PROJ_CHUNK)
        y = jnp.dot(h, w_ref[:, cs], preferred_element_type=F32)
        if mode == "headnorm":
            for u in range(PROJ_CHUNK // LANE):
                sl = slice(c * PROJ_CHUNK + u * LANE, c * PROJ_CHUNK + (u + 1) * LANE)
                yc = y[:, u * LANE:(u + 1) * LANE]
                ms = jnp.mean(yc * yc, axis=-1, keepdims=True)
                mult = jnp.where(b_ref[:, sl] > 0.0, lax.rsqrt(ms + EPS), 1.0) * a_ref[:, sl]
                o_ref[:, sl] = (yc * mult).astype(o_ref.dtype)
        elif mode == "scale":
            o_ref[:, cs] = (y * a_ref[:, cs]).astype(o_ref.dtype)
        else:
            raise ValueError(mode)

    if chunk_step is not None:
        @pl.when(pl.program_id(1) == chunk_step)
        def _():
            nchunk = y_scr.shape[0] // CMP_STRIDE
            for blk in range(o_ref.shape[1] // LANE):
                y_scr[...] = o_ref[:, blk * LANE:(blk + 1) * LANE].astype(F32)
                for l in range(CMP_STRIDE):
                    rows = y_scr[pl.ds(l, nchunk, stride=CMP_STRIDE), :]
                    oc_ref[blk, :, l * LANE:(l + 1) * LANE] = rows.astype(oc_ref.dtype)


def _proj(x, g, w, a, b, wg, bg, *, mode, gate_mode, tm, tn, name, chunk_step=None):
    s, d = x.shape
    n = w.shape[1]
    ng = wg.shape[1]
    assert s % tm == 0 and n % tn == 0 and tn % PROJ_CHUNK == 0
    out_specs = [
        pl.BlockSpec((tm, tn), lambda i, j: (i, j)),
        pl.BlockSpec((tm, ng), lambda i, j: (i, 0)),
    ]
    out_shape = [
        jax.ShapeDtypeStruct((s, n), BF16),
        jax.ShapeDtypeStruct((s, ng), F32),
    ]
    scratch = [pltpu.VMEM((tm, d), BF16)]
    if chunk_step is not None:
        assert tm % (8 * CMP_STRIDE) == 0
        nblk = tn // LANE
        out_specs.append(pl.BlockSpec((nblk, tm // CMP_STRIDE, CMP_STRIDE * LANE), lambda i, j: (0, i, 0)))
        out_shape.append(jax.ShapeDtypeStruct((nblk, s // CMP_STRIDE, CMP_STRIDE * LANE), BF16))
        scratch.append(pltpu.VMEM((tm, LANE), F32))
    return pl.pallas_call(
        functools.partial(_proj_kernel, mode=mode, gate_mode=gate_mode, chunk_step=chunk_step),
        grid=(s // tm, n // tn),
        in_specs=[
            pl.BlockSpec((tm, d), lambda i, j: (i, 0)),
            pl.BlockSpec((1, d), lambda i, j: (0, 0)),
            pl.BlockSpec((d, tn), lambda i, j: (0, j)),
            pl.BlockSpec((1, tn), lambda i, j: (0, j)),
            pl.BlockSpec((1, tn), lambda i, j: (0, j)),
            pl.BlockSpec((d, ng), lambda i, j: (0, 0)),
            pl.BlockSpec((1, ng), lambda i, j: (0, 0)),
        ],
        out_specs=out_specs,
        out_shape=out_shape,
        scratch_shapes=scratch,
        compiler_params=_params("parallel", "arbitrary"),
        name=name,
    )(x, g, w, a, b, wg, bg)


def _matmul_res_kernel(a_ref, w_ref, r_ref, o_ref):
    a = a_ref[...]
    for c in range(o_ref.shape[1] // PROJ_CHUNK):
        cs = slice(c * PROJ_CHUNK, (c + 1) * PROJ_CHUNK)
        o_ref[:, cs] = r_ref[:, cs] + jnp.dot(a, w_ref[:, cs], preferred_element_type=F32)


def _matmul_res(a, w, layer, res, *, tm, name):
    s, k = a.shape
    n = w.shape[2]
    assert s % tm == 0 and n % PROJ_CHUNK == 0
    return pl.pallas_call(
        _matmul_res_kernel,
        grid=(s // tm,),
        in_specs=[
            pl.BlockSpec((tm, k), lambda i: (i, 0)),
            pl.BlockSpec((None, k, n), lambda i: (layer, 0, 0), pipeline_mode=pl.Buffered(1)),
            pl.BlockSpec((tm, n), lambda i: (i, 0)),
        ],
        out_specs=pl.BlockSpec((tm, n), lambda i: (i, 0)),
        out_shape=jax.ShapeDtypeStruct((s, n), F32),
        compiler_params=_params("parallel"),
        name=name,
    )(a, w, res)


def _ffn_kernel(x_ref, g_ref, wg_ref, wu_ref, wd_ref, o_ref, h_scr):
    @pl.when(pl.program_id(1) == 0)
    def _():
        x = x_ref[...]
        h_scr[...] = _rms(x, g_ref[...]).astype(BF16)
        o_ref[...] = x

    half = h_scr.shape[0] // 2
    for u in range(2):
        rs = slice(u * half, (u + 1) * half)
        h = h_scr[rs, :]
        gate = jnp.dot(h, wg_ref[...], preferred_element_type=F32)
        up = jnp.dot(h, wu_ref[...], preferred_element_type=F32)
        act = (gate * jax.nn.sigmoid(gate) * up).astype(BF16)
        o_ref[rs, :] += jnp.dot(act, wd_ref[...], preferred_element_type=F32)


def _ffn(x, g, wg, wu, wd, layer, *, tm, tf):
    s, d = x.shape
    dff = wg.shape[2]
    assert s % tm == 0 and dff % tf == 0
    return pl.pallas_call(
        _ffn_kernel,
        grid=(s // tm, dff // tf),
        in_specs=[
            pl.BlockSpec((tm, d), lambda i, f: (i, 0)),
            pl.BlockSpec((1, d), lambda i, f: (0, 0)),
            pl.BlockSpec((None, d, tf), lambda i, f: (layer, 0, f)),
            pl.BlockSpec((None, d, tf), lambda i, f: (layer, 0, f)),
            pl.BlockSpec((None, tf, d), lambda i, f: (layer, f, 0)),
        ],
        out_specs=pl.BlockSpec((tm, d), lambda i, f: (i, 0)),
        out_shape=jax.ShapeDtypeStruct((s, d), F32),
        scratch_shapes=[pltpu.VMEM((tm, d), BF16)],
        compiler_params=_params("parallel", "arbitrary"),
        name="ffn",
    )(x, g, wg, wu, wd)


def _compress_kernel(x_ref, w1_ref, b1_ref, w2_ref, pos_ref, kg_ref, o_ref):
    half = CMP_STRIDE * DH
    x = x_ref[0, 0]
    nc = x.shape[0]
    top = jnp.dot(x, w1_ref[0, :half, :], preferred_element_type=F32)
    bot = jnp.dot(x, w1_ref[0, half:, :], preferred_element_type=F32)
    bot = pltpu.roll(bot, nc - 1, 0)
    row = lax.broadcasted_iota(jnp.int32, bot.shape, 0)
    bot = jnp.where(row == nc - 1, 0.0, bot)
    pos8 = jnp.broadcast_to(pos_ref[0], (8, 2 * half))
    posb = jnp.dot(pos8, w1_ref[0], preferred_element_type=F32)[0:1]
    hdn = jax.nn.gelu(top + bot + posb + b1_ref[0])
    y = jnp.dot(hdn.astype(BF16), w2_ref[0], preferred_element_type=F32)
    yn = _rms(y, kg_ref[...])
    o_ref[0, 0] = jnp.where(pl.program_id(0) == 0, yn, y).astype(o_ref.dtype)


def _compress(xc, w1, b1, w2, pos, kg):
    _, g, nc, k = xc.shape
    return pl.pallas_call(
        _compress_kernel,
        grid=(2, g),
        in_specs=[
            pl.BlockSpec((1, 1, nc, k), lambda s, i: (s, i, 0, 0)),
            pl.BlockSpec((1, 2 * k, DH), lambda s, i: (s, 0, 0)),
            pl.BlockSpec((1, 1, DH), lambda s, i: (s, 0, 0)),
            pl.BlockSpec((1, DH, DH), lambda s, i: (s, 0, 0)),
            pl.BlockSpec((1, 1, 2 * k), lambda s, i: (s, 0, 0)),
            pl.BlockSpec((1, DH), lambda s, i: (0, 0)),
        ],
        out_specs=pl.BlockSpec((1, 1, nc, DH), lambda s, i: (s, i, 0, 0)),
        out_shape=jax.ShapeDtypeStruct((2, g, nc, DH), BF16),
        compiler_params=_params("parallel", "parallel"),
        name="nsa_compress",
    )(xc, w1, b1, w2, pos, kg)


def _stack_heads(qb):
    return jnp.concatenate([qb[:, r * DH:(r + 1) * DH] for r in range(NSA_REP)], axis=0)


def _cmp_kernel(q_ref, kc_ref, vc_ref, ovt_ref, o_ref, mn_ref, imp_scr, *, tq, ktop, col_steps):
    qi = pl.program_id(0)
    gw = NSA_REP * DH

    def attend(ncols, nb):
        for g in range(NSA_GROUPS):
            q4 = _stack_heads(q_ref[:, g * gw:(g + 1) * gw])
            s = lax.dot_general(q4, kc_ref[g, 0:ncols, :], (((1,), (1,)), ((), ())),
                                preferred_element_type=F32)
            row = lax.broadcasted_iota(jnp.int32, s.shape, 0)
            col = lax.broadcasted_iota(jnp.int32, s.shape, 1)
            t = qi * tq + (row & (tq - 1))
            s = jnp.where(col * CMP_STRIDE + (CMP_LEN - 1) <= t, s, -jnp.inf)
            mx = jnp.max(s, axis=-1, keepdims=True)
            mx = jnp.where(jnp.abs(mx) < jnp.inf, mx, 0.0)
            p = jnp.exp2(s - mx)
            p = p / jnp.maximum(jnp.sum(p, axis=-1, keepdims=True), 1e-30)
            o = jnp.dot(p.astype(BF16), vc_ref[g, 0:ncols, :], preferred_element_type=F32)
            for r in range(NSA_REP):
                o_ref[:, g * gw + r * DH:g * gw + (r + 1) * DH] = o[r * tq:(r + 1) * tq].astype(o_ref.dtype)
            ps = p[0:tq]
            for r in range(1, NSA_REP):
                ps = ps + p[r * tq:(r + 1) * tq]
            imp_scr[g, 0:nb, :] = lax.dot_general(ovt_ref[0:nb, 0:ncols], ps.astype(BF16),
                                                  (((1,), (1,)), ((), ())), preferred_element_type=F32)

    def select(nb):
        nselp = imp_scr.shape[1]
        jj = lax.broadcasted_iota(jnp.int32, (nb, tq), 0)
        cur = (qi * tq + lax.broadcasted_iota(jnp.int32, (nb, tq), 1)) // SEL_BLOCK
        forced = (jj == 0) | (jj == cur) | (jj == cur - 1)
        free = (jj >= 1) & (jj <= cur - 2)
        jjf = jj.astype(F32)
        scores = [jnp.where(free, imp_scr[g, 0:nb, :], -jnp.inf) for g in range(NSA_GROUPS)]
        for _ in range(ktop - 3):
            for g in range(NSA_GROUPS):
                top = jnp.max(scores[g], axis=0, keepdims=True)
                first = jnp.min(jnp.where(scores[g] == top, jjf, 1e9), axis=0, keepdims=True)
                scores[g] = jnp.where(jjf == first, -jnp.inf, scores[g])
        for g in range(NSA_GROUPS):
            picked = forced | (free & (scores[g] == -jnp.inf))
            notsel = jnp.where(picked, 0.0, 1.0)
            if nb < nselp:
                notsel = jnp.concatenate([notsel, jnp.ones((nselp - nb, tq), F32)], axis=0)
            mn_ref[g] = notsel.T.astype(mn_ref.dtype)

    def variant(ncols):
        nb = min(imp_scr.shape[1], -(-(ncols * CMP_STRIDE // SEL_BLOCK) // 8) * 8)
        attend(ncols, nb)
        select(nb)

    needed = ((qi + 1) * tq - CMP_LEN) // CMP_STRIDE + 1
    lo = 0
    for ncols in col_steps:
        pl.when((needed > lo) & (needed <= ncols))(functools.partial(variant, ncols))
        lo = ncols


def _cmp_attention(q, kc, vc, overlap_t, *, tq):
    s = q.shape[0]
    g, nc, _ = kc.shape
    nselp = overlap_t.shape[0]
    ktop = min(N_SELECT, s // SEL_BLOCK)
    assert ktop >= 3
    gw = NSA_REP * DH
    col_steps = tuple(range(2 * LANE, nc + 1, 2 * LANE)) if nc % (2 * LANE) == 0 else (nc,)
    return pl.pallas_call(
        functools.partial(_cmp_kernel, tq=tq, ktop=ktop, col_steps=col_steps),
        grid=(s // tq,),
        in_specs=[
            pl.BlockSpec((tq, g * gw), lambda i: (i, 0)),
            pl.BlockSpec((g, nc, DH), lambda i: (0, 0, 0)),
            pl.BlockSpec((g, nc, DH), lambda i: (0, 0, 0)),
            pl.BlockSpec((nselp, nc), lambda i: (0, 0)),
        ],
        out_specs=[
            pl.BlockSpec((tq, g * gw), lambda i: (i, 0)),
            pl.BlockSpec((g, tq, nselp), lambda i: (0, i, 0)),
        ],
        out_shape=[
            jax.ShapeDtypeStruct((s, g * gw), BF16),
            jax.ShapeDtypeStruct((g, s, nselp), BF16),
        ],
        scratch_shapes=[pltpu.VMEM((g, nselp, tq), F32)],
        compiler_params=_params("parallel"),
        name="nsa_cmp_select",
    )(q, kc, vc, overlap_t)


def _sel_kernel(q_ref, mn_ref, k_ref, v_ref, e_ref, o_ref, ka_scr, va_scr, qa_scr, s_scr, m_scr, acc_scr,
                *, tq, tk):
    qi = pl.program_id(1)
    nhalf = qa_scr.shape[0]
    nper = e_ref.shape[0]

    @pl.when(qi == 0)
    def _():
        for c in range(ka_scr.shape[0] // nper):
            rows = slice(c * nper, (c + 1) * nper)
            ka_scr[rows, 0:DH] = k_ref[rows, :]
            ka_scr[rows, DH:2 * DH] = e_ref[...]
            va_scr[rows, 0:DH] = v_ref[rows, :]
            va_scr[rows, DH:2 * DH] = jnp.ones((nper, DH), BF16)

    q4 = _stack_heads(q_ref[...])
    mn = mn_ref[0]
    for hf in range(nhalf):
        part = mn[:, hf * LANE:(hf + 1) * LANE]
        qa_scr[hf, :, 0:DH] = q4
        qa_scr[hf, :, DH:2 * DH] = jnp.concatenate([part] * NSA_REP, axis=0)
    m_scr[...] = jnp.full_like(m_scr, -jnp.inf)
    acc_scr[...] = jnp.zeros_like(acc_scr)

    def scores(j, slot):
        k0 = pl.multiple_of(j * tk, tk)
        s_scr[slot] = lax.dot_general(qa_scr[k0 // (SEL_BLOCK * LANE)], ka_scr[pl.ds(k0, tk), :],
                                      (((1,), (1,)), ((), ())), preferred_element_type=F32)

    def accumulate(j, slot, masked):
        k0 = pl.multiple_of(j * tk, tk)
        s = s_scr[slot]
        if masked:
            row = lax.broadcasted_iota(jnp.int32, s.shape, 0)
            col = lax.broadcasted_iota(jnp.int32, s.shape, 1)
            s = jnp.where(k0 + col <= qi * tq + (row & (tq - 1)), s, -jnp.inf)
        m_old = m_scr[...]
        m_new = jnp.maximum(m_old, jnp.max(s, axis=-1, keepdims=True))
        p = jnp.exp2(s - m_new).astype(BF16)
        acc_scr[...] = (jnp.exp2(m_old - m_new) * acc_scr[...]
                        + jnp.dot(p, va_scr[pl.ds(k0, tk), :], preferred_element_type=F32))
        m_scr[...] = m_new

    n = (qi * tq + tq - 1) // tk + 1
    unroll = 4
    ntrip = (n - 1) // unroll
    scores(0, 0)

    def trip(i, c):
        for u in range(unroll):
            scores(unroll * i + u + 1, (u + 1) % 2)
            accumulate(unroll * i + u, u % 2, False)
        return c

    lax.fori_loop(0, ntrip, trip, 0)

    first = ntrip * unroll
    for rest in range(1, unroll + 1):
        @pl.when(n - first == rest)
        def _(rest=rest):
            for u in range(rest):
                if u + 1 < rest:
                    scores(first + u + 1, (u + 1) % 2)
                accumulate(first + u, u % 2, u + 1 == rest)

    acc = acc_scr[...]
    o = acc[:, 0:DH] / jnp.maximum(acc[:, DH:2 * DH], 1e-30)
    for r in range(NSA_REP):
        o_ref[:, r * DH:(r + 1) * DH] = o[r * tq:(r + 1) * tq].astype(o_ref.dtype)


def _sel_attention(q, notsel, kv_arr, expand, *, k_col, v_col, tq, tk):
    s = q.shape[0]
    g = notsel.shape[0]
    nselp = notsel.shape[2]
    gw = NSA_REP * DH
    nper = expand.shape[0]
    assert s % nper == 0 and nper % tk == 0 and tk % tq == 0 and s % tq == 0
    once = pl.Buffered(1)
    return pl.pallas_call(
        functools.partial(_sel_kernel, tq=tq, tk=tk),
        grid=(g, s // tq),
        in_specs=[
            pl.BlockSpec((tq, gw), lambda gi, i: (i, gi)),
            pl.BlockSpec((1, tq, nselp), lambda gi, i: (gi, i, 0)),
            pl.BlockSpec((s, DH), lambda gi, i: (0, k_col + gi), pipeline_mode=once),
            pl.BlockSpec((s, DH), lambda gi, i: (0, v_col + gi), pipeline_mode=once),
            pl.BlockSpec((nper, LANE), lambda gi, i: (0, 0), pipeline_mode=once),
        ],
        out_specs=pl.BlockSpec((tq, gw), lambda gi, i: (i, gi)),
        out_shape=jax.ShapeDtypeStruct((s, g * gw), BF16),
        scratch_shapes=[
            pltpu.VMEM((s, 2 * DH), BF16),
            pltpu.VMEM((s, 2 * DH), BF16),
            pltpu.VMEM((nselp // LANE, NSA_REP * tq, 2 * DH), BF16),
            pltpu.VMEM((2, NSA_REP * tq, tk), F32),
            pltpu.VMEM((NSA_REP * tq, 1), F32),
            pltpu.VMEM((NSA_REP * tq, 2 * DH), F32),
        ],
        compiler_params=_params("arbitrary", "arbitrary"),
        name="nsa_sel_attention",
    )(q, notsel, kv_arr, kv_arr, expand)


def _win_kernel(q_ref, *refs, tq, nback):
    nblk = nback + 1
    k_refs = refs[:nblk]
    v_refs = refs[nblk:2 * nblk]
    band_ref, oc_ref, os_ref, gt_ref, o_ref = refs[2 * nblk:]
    qi = pl.program_id(0)
    gw = NSA_REP * DH
    band = band_ref[...]
    col = lax.broadcasted_iota(jnp.int32, band.shape, 1)
    band = jnp.where(col >= (nback - qi) * tq, band, -jnp.inf)
    for g in range(NSA_GROUPS):
        q4 = _stack_heads(q_ref[:, g * gw:(g + 1) * gw])
        kc = jnp.concatenate([r[:, g * DH:(g + 1) * DH] for r in k_refs], axis=0)
        vc = jnp.concatenate([r[:, g * DH:(g + 1) * DH] for r in v_refs], axis=0)
        s = lax.dot_general(q4, kc, (((1,), (1,)), ((), ())), preferred_element_type=F32) + band
        mx = jnp.max(s, axis=-1, keepdims=True)
        mx = jnp.where(jnp.abs(mx) < jnp.inf, mx, 0.0)
        p = jnp.exp2(s - mx)
        p = p / jnp.maximum(jnp.sum(p, axis=-1, keepdims=True), 1e-30)
        ow = jnp.dot(p.astype(BF16), vc, preferred_element_type=F32)
        gates = gt_ref[...]
        for r in range(NSA_REP):
            sl = slice(g * gw + r * DH, g * gw + (r + 1) * DH)
            c0 = 3 * (g * NSA_REP + r)
            out = (gates[:, c0:c0 + 1] * oc_ref[:, sl].astype(F32)
                   + gates[:, c0 + 1:c0 + 2] * os_ref[:, sl].astype(F32)
                   + gates[:, c0 + 2:c0 + 3] * ow[r * tq:(r + 1) * tq])
            o_ref[:, sl] = out.astype(o_ref.dtype)


def _win_attention(q, kv_arr, o_cmp, o_sel, gates, *, k_blk, v_blk, tq):
    s = q.shape[0]
    g = NSA_GROUPS
    gw = NSA_REP * DH
    nback = WINDOW // tq
    assert nback * tq == WINDOW
    qq = jnp.arange(NSA_REP * tq)[:, None] % tq
    kk = jnp.arange((nback + 1) * tq)[None, :]
    band = jnp.where((kk > qq) & (kk <= qq + WINDOW), 0.0, -jnp.inf).astype(F32)

    def kvmap(blk, b):
        def f(i):
            return (jnp.maximum(i - nback + b, 0), blk)
        return f

    k_specs = [pl.BlockSpec((tq, g * DH), kvmap(k_blk, b)) for b in range(nback + 1)]
    v_specs = [pl.BlockSpec((tq, g * DH), kvmap(v_blk, b)) for b in range(nback + 1)]
    wide = pl.BlockSpec((tq, g * gw), lambda i: (i, 0))
    return pl.pallas_call(
        functools.partial(_win_kernel, tq=tq, nback=nback),
        grid=(s // tq,),
        in_specs=([wide] + k_specs + v_specs
                  + [pl.BlockSpec(band.shape, lambda i: (0, 0)), wide, wide,
                     pl.BlockSpec((tq, LANE), lambda i: (i, 0))]),
        out_specs=wide,
        out_shape=jax.ShapeDtypeStruct((s, g * gw), BF16),
        compiler_params=_params("parallel"),
        name="nsa_win_combine",
    )(q, *([kv_arr] * (2 * (nback + 1))), band, o_cmp, o_sel, gates)


def _mlstm_kernel(q_ref, k_ref, v_ref, o_ref, gt_ref, og_ref, y_ref, c_scr, n_scr, m_scr):
    L = q_ref.shape[0]

    @pl.when(pl.program_id(0) == 0)
    def _():
        c_scr[...] = jnp.zeros_like(c_scr)
        n_scr[...] = jnp.zeros_like(n_scr)
        m_scr[...] = jnp.full_like(m_scr, NEG_INIT)

    ri = lax.broadcasted_iota(jnp.int32, (L, L), 0)
    ci = lax.broadcasted_iota(jnp.int32, (L, L), 1)
    eye = ri == ci
    tril = ci <= ri
    triu = ri <= ci
    gates = gt_ref[...]

    def to_row(col):
        return jnp.sum(jnp.where(eye, col, 0.0), axis=0, keepdims=True)

    for h in range(ML_HEADS):
        qh = q_ref[:, h * ML_DK:(h + 1) * ML_DK]
        kh = k_ref[:, h * ML_DK:(h + 1) * ML_DK]
        vh = v_ref[:, h * ML_DV:(h + 1) * ML_DV]
        ig_col = gates[:, h:h + 1]
        fg_col = gates[:, ML_HEADS + h:ML_HEADS + h + 1]
        lf_col = jnp.minimum(fg_col, 0.0) - jnp.log(1.0 + jnp.exp(-jnp.abs(fg_col)))
        lf_row = to_row(lf_col)
        ig_row = to_row(ig_col)
        b_col = jnp.sum(jnp.where(tril, lf_row, 0.0), axis=1, keepdims=True)
        b_row = jnp.sum(jnp.where(triu, lf_col, 0.0), axis=0, keepdims=True)
        m_old = m_scr[h:h + 1, 0:1]
        dmat = jnp.where(tril, b_col - b_row + ig_row, -jnp.inf)
        m_inter = b_col + m_old
        m_t = jnp.maximum(m_inter, jnp.max(dmat, axis=1, keepdims=True))
        qk = lax.dot_general(qh, kh, (((1,), (1,)), ((), ())), preferred_element_type=F32)
        a = jnp.exp(dmat - m_t) * qk
        dec = jnp.exp(m_inter - m_t)
        c_old = c_scr[h]
        n_old = n_scr[h:h + 1, :]
        num = (jnp.dot(a.astype(BF16), vh, preferred_element_type=F32)
               + dec * jnp.dot(qh, c_old.astype(BF16), preferred_element_type=F32))
        qn = jnp.sum(qh.astype(F32) * n_old, axis=1, keepdims=True)
        den = jnp.sum(a, axis=1, keepdims=True) + dec * qn
        hx = num / jnp.maximum(jnp.abs(den), jnp.exp(-m_t))

        b_last = b_col[L - 1:L, :]
        g_col = b_last - b_col + ig_col
        m_new = jnp.maximum(b_last + m_old, jnp.max(g_col, axis=0, keepdims=True))
        w_col = jnp.exp(g_col - m_new)
        cd = jnp.exp(b_last + m_old - m_new)
        kw = kh.astype(F32) * w_col
        c_scr[h] = cd * c_old + lax.dot_general(kw.astype(BF16), vh, (((0,), (0,)), ((), ())),
                                                preferred_element_type=F32)
        n_scr[h:h + 1, :] = cd * n_old + jnp.sum(kw, axis=0, keepdims=True)
        m_scr[h:h + 1, :] = jnp.broadcast_to(m_new, (1, LANE))

        sl = slice(h * ML_DV, (h + 1) * ML_DV)
        hn = _rms(hx, og_ref[:, sl])
        y_ref[:, sl] = (jax.nn.sigmoid(o_ref[:, sl].astype(F32)) * hn).astype(y_ref.dtype)


def _mlstm(qkvo, gates, out_g):
    s = qkvo.shape[0]
    L = min(ML_CHUNK, s)
    assert s % L == 0
    wq = ML_HEADS * ML_DK
    wv = ML_HEADS * ML_DV
    return pl.pallas_call(
        _mlstm_kernel,
        grid=(s // L,),
        in_specs=[
            pl.BlockSpec((L, wq), lambda c: (c, 0)),
            pl.BlockSpec((L, wq), lambda c: (c, 1)),
            pl.BlockSpec((L, wv), lambda c: (c, 1)),
            pl.BlockSpec((L, wv), lambda c: (c, 2)),
            pl.BlockSpec((L, LANE), lambda c: (c, 0)),
            pl.BlockSpec((1, wv), lambda c: (0, 0)),
        ],
        out_specs=pl.BlockSpec((L, wv), lambda c: (c, 0)),
        out_shape=jax.ShapeDtypeStruct((s, wv), BF16),
        scratch_shapes=[
            pltpu.VMEM((ML_HEADS, ML_DK, ML_DV), F32),
            pltpu.VMEM((ML_HEADS, ML_DK), F32),
            pltpu.VMEM((ML_HEADS, LANE), F32),
        ],
        compiler_params=_params("arbitrary"),
        name="mlstm_scan",
    )(qkvo, qkvo, qkvo, qkvo, gates, out_g)


def _pad_cols(a, n):
    return jnp.pad(a, ((0, 0), (0, n - a.shape[1])))


def _nsa_layer(x, norm_g, w_in, b_gate, q_g, k_g, cmp_pos, cmp_w1, cmp_b1, cmp_w2, w_out, layer,
               *, tm, tq, tk):
    s, d = x.shape
    qd = NSA_HEADS * DH
    kvd = NSA_GROUPS * DH
    norm_g = norm_g.reshape(1, d)

    w_main = w_in[:, :qd + 6 * kvd].astype(BF16)
    ones = jnp.ones((kvd,), F32)
    gain = jnp.concatenate([jnp.tile(q_g, NSA_HEADS) * (DH ** -0.5 * LOG2E), ones, ones,
                            jnp.tile(k_g[1], NSA_GROUPS), ones, jnp.tile(k_g[2], NSA_GROUPS), ones])
    flag = jnp.concatenate([jnp.ones((qd,), F32), 0 * ones, 0 * ones, ones, 0 * ones, ones, 0 * ones])
    w_gate = _pad_cols(w_in[:, qd + 6 * kvd:], LANE).astype(BF16)
    bias = _pad_cols(b_gate.reshape(1, -1), LANE)
    tn = 2 * kvd
    assert qd % tn == 0
    proj, gates, xc = _proj(x, norm_g, w_main, gain.reshape(1, -1), flag.reshape(1, -1), w_gate, bias,
                            mode="headnorm", gate_mode="sigmoid", tm=min(2 * tm, s), tn=tn, name="nsa_proj",
                            chunk_step=qd // tn)
    nc = s // CMP_STRIDE
    xc = xc.reshape(2, NSA_GROUPS, nc, CMP_STRIDE * DH)
    kvc = _compress(xc, cmp_w1.astype(BF16), cmp_b1.reshape(2, 1, DH), cmp_w2.astype(BF16),
                    cmp_pos.reshape(2, 1, CMP_LEN * DH).astype(BF16), k_g[0].reshape(1, DH))

    n_sel = s // SEL_BLOCK
    nselp = -(-n_sel // LANE) * LANE
    cstart = jnp.arange(nc) * CMP_STRIDE
    sstart = jnp.arange(nselp) * SEL_BLOCK
    overlap_t = ((cstart[None, :] < sstart[:, None] + SEL_BLOCK)
                 & (cstart[None, :] + CMP_LEN > sstart[:, None])
                 & (jnp.arange(nselp)[:, None] < n_sel)
                 & (jnp.arange(nc)[None, :] < nc - 1)).astype(BF16)
    o_cmp, notsel = _cmp_attention(proj, kvc[0], kvc[1], overlap_t, tq=tq)

    blk = jnp.arange(min(s, SEL_BLOCK * LANE)) // SEL_BLOCK
    expand =jnp.where(blk[:, None] == jnp.arange(LANE)[None, :], MASK_BIAS, 0.0).astype(BF16)
    col0 = qd // DH
    o_sel = _sel_attention(proj, notsel, proj, expand, k_col=col0 + 2 * NSA_GROUPS,
                           v_col=col0 + 3 * NSA_GROUPS, tq=min(2 * tq, s), tk=tk)
    mixed = _win_attention(proj, proj, o_cmp, o_sel, gates, k_blk=(qd + 4 * kvd) // kvd,
                           v_blk=(qd + 5 * kvd) // kvd, tq=tq)
    return _matmul_res(mixed, w_out, layer, x, tm=tm, name="nsa_out_proj")


def _mlstm_layer(x, norm_g, w_in, b_if, out_g, w_out, layer, *, tm):
    s, d = x.shape
    norm_g = norm_g.reshape(1, d)
    wq = ML_HEADS * ML_DK
    wv = ML_HEADS * ML_DV
    nmain = 2 * wq + 2 * wv
    w_main = w_in[:, :nmain].astype(BF16)
    scale = jnp.concatenate([jnp.ones((wq,), F32), jnp.full((wq,), ML_DK ** -0.5, F32),
                             jnp.ones((2 * wv,), F32)]).reshape(1, -1)
    w_gate = _pad_cols(w_in[:, nmain:], LANE).astype(BF16)
    bias = _pad_cols(b_if.reshape(1, -1), LANE)
    qkvo, gates = _proj(x, norm_g, w_main, scale, scale, w_gate, bias, mode="scale", gate_mode="bias",
                        tm=min(2 * tm, s), tn=1024, name="ml_proj")
    y = _mlstm(qkvo, gates, out_g.reshape(1, -1))
    return _matmul_res(y, w_out, layer, x, tm=tm, name="ml_out_proj")


def _ffn_layer(x, norm_g, wg, wu, wd, layer, *, tm):
    return _ffn(x, norm_g.reshape(1, -1), wg, wu, wd, layer, tm=tm, tf=512)


def kernel(x, norm_mix_g, norm_ffn_g, nsa_w_in, nsa_b_gate, nsa_q_norm_g, nsa_k_norm_g, nsa_cmp_pos,
           nsa_cmp_w1, nsa_cmp_b1, nsa_cmp_w2, nsa_w_out, ml_w_in, ml_b_if, ml_out_norm_g, ml_w_out,
           ffn_w_gate, ffn_w_up, ffn_w_down):
    b, s, d = x.shape
    depth = norm_mix_g.shape[0]
    tm = min(512, s)
    tq = 128
    tk = min(1024, s)
    nsa_w_out, ml_w_out = nsa_w_out.astype(BF16), ml_w_out.astype(BF16)
    ffn_w = (ffn_w_gate.astype(BF16), ffn_w_up.astype(BF16), ffn_w_down.astype(BF16))
    outs = []
    for bi in range(b):
        xb = x[bi]
        for i in range(depth):
            j = i // 2
            if i % 2 == 0:
                xb = _nsa_layer(xb, norm_mix_g[i], nsa_w_in[j], nsa_b_gate[j], nsa_q_norm_g[j],
                                nsa_k_norm_g[j], nsa_cmp_pos[j], nsa_cmp_w1[j], nsa_cmp_b1[j],
                                nsa_cmp_w2[j], nsa_w_out, j, tm=tm, tq=tq, tk=tk)
            else:
                xb = _mlstm_layer(xb, norm_mix_g[i], ml_w_in[j], ml_b_if[j], ml_out_norm_g[j],
                                  ml_w_out, j, tm=tm)
            xb = _ffn_layer(xb, norm_ffn_g[i], *ffn_w, i, tm=min(2 * tm, s))
        outs.append(xb)
    return jnp.stack(outs, axis=0)
```

```python
import functools

import jax
import jax.numpy as jnp
from jax import lax
from jax.experimental import pallas as pl
from jax.experimental.pallas import tpu as pltpu

F32 = jnp.float32
BF16 = jnp.bfloat16

EPS = 1e-6
NEG_INIT = -1e30
LOG2E = 1.4426950408889634

LANE = 128
VMEM_LIMIT = 56 * 1024 * 1024
PROJ_CHUNK = 512

NSA_HEADS = 16
NSA_GROUPS = 4
NSA_REP = NSA_HEADS // NSA_GROUPS
DH = 128
CMP_LEN = 32
CMP_STRIDE = 16
SEL_BLOCK = 64
N_SELECT = 16
WINDOW = 512
ML_HEADS = 8
ML_DK = 128
ML_DV = 256
ML_CHUNK = 512

MASK_BIAS = -(2.0 ** 100)


def _params(*sem):
    return pltpu.CompilerParams(dimension_semantics=sem, vmem_limit_bytes=VMEM_LIMIT)


def _rms(x, g):
    ms = jnp.mean(x * x, axis=-1, keepdims=True)
    return x * lax.rsqrt(ms + EPS) * g


def _proj_kernel(x_ref, g_ref, w_ref, a_ref, b_ref, wg_ref, bg_ref, o_ref, og_ref, *rest,
                 mode, gate_mode, chunk_step):
    if chunk_step is None:
        (h_scr,) = rest
    else:
        oc_ref, h_scr, y_scr = rest

    @pl.when(pl.program_id(1) == 0)
    def _():
        h = _rms(x_ref[...], g_ref[...]).astype(BF16)
        h_scr[...] = h
        gl = jnp.dot(h, wg_ref[...], preferred_element_type=F32) + bg_ref[...]
        og_ref[...] = jax.nn.sigmoid(gl) if gate_mode == "sigmoid" else gl

    h = h_scr[...]
    for c in range(o_ref.shape[1] // PROJ_CHUNK):
        cs = slice(c * PROJ_CHUNK, (c + 1) * PROJ_CHUNK)
        y = jnp.dot(h, w_ref[:, cs], preferred_element_type=F32)
        if mode == "headnorm":
            for u in range(PROJ_CHUNK // LANE):
                sl = slice(c * PROJ_CHUNK + u * LANE, c * PROJ_CHUNK + (u + 1) * LANE)
                yc = y[:, u * LANE:(u + 1) * LANE]
                ms = jnp.mean(yc * yc, axis=-1, keepdims=True)
                mult = jnp.where(b_ref[:, sl] > 0.0, lax.rsqrt(ms + EPS), 1.0) * a_ref[:, sl]
                o_ref[:, sl] = (yc * mult).astype(o_ref.dtype)
        elif mode == "scale":
            o_ref[:, cs] = (y * a_ref[:, cs]).astype(o_ref.dtype)
        else:
            raise ValueError(mode)

    if chunk_step is not None:
        @pl.when(pl.program_id(1) == chunk_step)
        def _():
            nchunk = y_scr.shape[0] // CMP_STRIDE
            for blk in range(o_ref.shape[1] // LANE):
                y_scr[...] = o_ref[:, blk * LANE:(blk + 1) * LANE].astype(F32)
                for l in range(CMP_STRIDE):
                    rows = y_scr[pl.ds(l, nchunk, stride=CMP_STRIDE), :]
                    oc_ref[blk, :, l * LANE:(l + 1) * LANE] = rows.astype(oc_ref.dtype)


def _proj(x, g, w, a, b, wg, bg, *, mode, gate_mode, tm, tn, name, chunk_step=None):
    s, d = x.shape
    n = w.shape[1]
    ng = wg.shape[1]
    assert s % tm == 0 and n % tn == 0 and tn % PROJ_CHUNK == 0
    out_specs = [
        pl.BlockSpec((tm, tn), lambda i, j: (i, j)),
        pl.BlockSpec((tm, ng), lambda i, j: (i, 0)),
    ]
    out_shape = [
        jax.ShapeDtypeStruct((s, n), BF16),
        jax.ShapeDtypeStruct((s, ng), F32),
    ]
    scratch = [pltpu.VMEM((tm, d), BF16)]
    if chunk_step is not None:
        assert tm % (8 * CMP_STRIDE) == 0
        nblk = tn // LANE
        out_specs.append(pl.BlockSpec((nblk, tm // CMP_STRIDE, CMP_STRIDE * LANE), lambda i, j: (0, i, 0)))
        out_shape.append(jax.ShapeDtypeStruct((nblk, s // CMP_STRIDE, CMP_STRIDE * LANE), BF16))
        scratch.append(pltpu.VMEM((tm, LANE), F32))
    return pl.pallas_call(
        functools.partial(_proj_kernel, mode=mode, gate_mode=gate_mode, chunk_step=chunk_step),
        grid=(s // tm, n // tn),
        in_specs=[
            pl.BlockSpec((tm, d), lambda i, j: (i, 0)),
            pl.BlockSpec((1, d), lambda i, j: (0, 0)),
            pl.BlockSpec((d, tn), lambda i, j: (0, j)),
            pl.BlockSpec((1, tn), lambda i, j: (0, j)),
            pl.BlockSpec((1, tn), lambda i, j: (0, j)),
            pl.BlockSpec((d, ng), lambda i, j: (0, 0)),
            pl.BlockSpec((1, ng), lambda i, j: (0, 0)),
        ],
        out_specs=out_specs,
        out_shape=out_shape,
        scratch_shapes=scratch,
        compiler_params=_params("parallel", "arbitrary"),
        name=name,
    )(x, g, w, a, b, wg, bg)


def _matmul_res_kernel(a_ref, w_ref, r_ref, o_ref):
    a = a_ref[...]
    for c in range(o_ref.shape[1] // PROJ_CHUNK):
        cs = slice(c * PROJ_CHUNK, (c + 1) * PROJ_CHUNK)
        o_ref[:, cs] = r_ref[:, cs] + jnp.dot(a, w_ref[:, cs], preferred_element_type=F32)


def _matmul_res(a, w, layer, res, *, tm, name):
    s, k = a.shape
    n = w.shape[2]
    assert s % tm == 0 and n % PROJ_CHUNK == 0
    return pl.pallas_call(
        _matmul_res_kernel,
        grid=(s // tm,),
        in_specs=[
            pl.BlockSpec((tm, k), lambda i: (i, 0)),
            pl.BlockSpec((None, k, n), lambda i: (layer, 0, 0), pipeline_mode=pl.Buffered(1)),
            pl.BlockSpec((tm, n), lambda i: (i, 0)),
        ],
        out_specs=pl.BlockSpec((tm, n), lambda i: (i, 0)),
        out_shape=jax.ShapeDtypeStruct((s, n), F32),
        compiler_params=_params("parallel"),
        name=name,
    )(a, w, res)


def _ffn_kernel(x_ref, g_ref, wg_ref, wu_ref, wd_ref, o_ref, h_scr):
    @pl.when(pl.program_id(1) == 0)
    def _():
        x = x_ref[...]
        h_scr[...] = _rms(x, g_ref[...]).astype(BF16)
        o_ref[...] = x

    half = h_scr.shape[0] // 2
    for u in range(2):
        rs = slice(u * half, (u + 1) * half)
        h = h_scr[rs, :]
        gate = jnp.dot(h, wg_ref[...], preferred_element_type=F32)
        up = jnp.dot(h, wu_ref[...], preferred_element_type=F32)
        act = (gate * jax.nn.sigmoid(gate) * up).astype(BF16)
        o_ref[rs, :] += jnp.dot(act, wd_ref[...], preferred_element_type=F32)


def _ffn(x, g, wg, wu, wd, layer, *, tm, tf):
    s, d = x.shape
    dff = wg.shape[2]
    assert s % tm == 0 and dff % tf == 0
    return pl.pallas_call(
        _ffn_kernel,
        grid=(s // tm, dff // tf),
        in_specs=[
            pl.BlockSpec((tm, d), lambda i, f: (i, 0)),
            pl.BlockSpec((1, d), lambda i, f: (0, 0)),
            pl.BlockSpec((None, d, tf), lambda i, f: (layer, 0, f)),
            pl.BlockSpec((None, d, tf), lambda i, f: (layer, 0, f)),
            pl.BlockSpec((None, tf, d), lambda i, f: (layer, f, 0)),
        ],
        out_specs=pl.BlockSpec((tm, d), lambda i, f: (i, 0)),
        out_shape=jax.ShapeDtypeStruct((s, d), F32),
        scratch_shapes=[pltpu.VMEM((tm, d), BF16)],
        compiler_params=_params("parallel", "arbitrary"),
        name="ffn",
    )(x, g, wg, wu, wd)


def _compress_kernel(x_ref, w1_ref, b1_ref, w2_ref, pos_ref, kg_ref, o_ref):
    half = CMP_STRIDE * DH
    x = x_ref[0, 0]
    nc = x.shape[0]
    top = jnp.dot(x, w1_ref[0, :half, :], preferred_element_type=F32)
    bot = jnp.dot(x, w1_ref[0, half:, :], preferred_element_type=F32)
    bot = pltpu.roll(bot, nc - 1, 0)
    row = lax.broadcasted_iota(jnp.int32, bot.shape, 0)
    bot = jnp.where(row == nc - 1, 0.0, bot)
    pos8 = jnp.broadcast_to(pos_ref[0], (8, 2 * half))
    posb = jnp.dot(pos8, w1_ref[0], preferred_element_type=F32)[0:1]
    hdn = jax.nn.gelu(top + bot + posb + b1_ref[0])
    y = jnp.dot(hdn.astype(BF16), w2_ref[0], preferred_element_type=F32)
    yn = _rms(y, kg_ref[...])
    o_ref[0, 0] = jnp.where(pl.program_id(0) == 0, yn, y).astype(o_ref.dtype)


def _compress(xc, w1, b1, w2, pos, kg):
    _, g, nc, k = xc.shape
    return pl.pallas_call(
        _compress_kernel,
        grid=(2, g),
        in_specs=[
            pl.BlockSpec((1, 1, nc, k), lambda s, i: (s, i, 0, 0)),
            pl.BlockSpec((1, 2 * k, DH), lambda s, i: (s, 0, 0)),
            pl.BlockSpec((1, 1, DH), lambda s, i: (s, 0, 0)),
            pl.BlockSpec((1, DH, DH), lambda s, i: (s, 0, 0)),
            pl.BlockSpec((1, 1, 2 * k), lambda s, i: (s, 0, 0)),
            pl.BlockSpec((1, DH), lambda s, i: (0, 0)),
        ],
        out_specs=pl.BlockSpec((1, 1, nc, DH), lambda s, i: (s, i, 0, 0)),
        out_shape=jax.ShapeDtypeStruct((2, g, nc, DH), BF16),
        compiler_params=_params("parallel", "parallel"),
        name="nsa_compress",
    )(xc, w1, b1, w2, pos, kg)


def _stack_heads(qb):
    return jnp.concatenate([qb[:, r * DH:(r + 1) * DH] for r in range(NSA_REP)], axis=0)


def _cmp_kernel(q_ref, kc_ref, vc_ref, ovt_ref, o_ref, mn_ref, imp_scr, *, tq, ktop, col_steps):
    qi = pl.program_id(0)
    gw = NSA_REP * DH

    def attend(ncols, nb):
        for g in range(NSA_GROUPS):
            q4 = _stack_heads(q_ref[:, g * gw:(g + 1) * gw])
            s = lax.dot_general(q4, kc_ref[g, 0:ncols, :], (((1,), (1,)), ((), ())),
                                preferred_element_type=F32)
            row = lax.broadcasted_iota(jnp.int32, s.shape, 0)
            col = lax.broadcasted_iota(jnp.int32, s.shape, 1)
            t = qi * tq + (row & (tq - 1))
            s = jnp.where(col * CMP_STRIDE + (CMP_LEN - 1) <= t, s, -jnp.inf)
            mx = jnp.max(s, axis=-1, keepdims=True)
            mx = jnp.where(jnp.abs(mx) < jnp.inf, mx, 0.0)
            p = jnp.exp2(s - mx)
            p = p / jnp.maximum(jnp.sum(p, axis=-1, keepdims=True), 1e-30)
            o = jnp.dot(p.astype(BF16), vc_ref[g, 0:ncols, :], preferred_element_type=F32)
            for r in range(NSA_REP):
                o_ref[:, g * gw + r * DH:g * gw + (r + 1) * DH] = o[r * tq:(r + 1) * tq].astype(o_ref.dtype)
            ps = p[0:tq]
            for r in range(1, NSA_REP):
                ps = ps + p[r * tq:(r + 1) * tq]
            imp_scr[g, 0:nb, :] = lax.dot_general(ovt_ref[0:nb, 0:ncols], ps.astype(BF16),
                                                  (((1,), (1,)), ((), ())), preferred_element_type=F32)

    def select(nb):
        nselp = imp_scr.shape[1]
        jj = lax.broadcasted_iota(jnp.int32, (nb, tq), 0)
        cur = (qi * tq + lax.broadcasted_iota(jnp.int32, (nb, tq), 1)) // SEL_BLOCK
        forced = (jj == 0) | (jj == cur) | (jj == cur - 1)
        free = (jj >= 1) & (jj <= cur - 2)
        jjf = jj.astype(F32)
        scores = [jnp.where(free, imp_scr[g, 0:nb, :], -jnp.inf) for g in range(NSA_GROUPS)]
        for _ in range(ktop - 3):
            for g in range(NSA_GROUPS):
                top = jnp.max(scores[g], axis=0, keepdims=True)
                first = jnp.min(jnp.where(scores[g] == top, jjf, 1e9), axis=0, keepdims=True)
                scores[g] = jnp.where(jjf == first, -jnp.inf, scores[g])
        for g in range(NSA_GROUPS):
            picked = forced | (free & (scores[g] == -jnp.inf))
            notsel = jnp.where(picked, 0.0, 1.0)
            if nb < nselp:
                notsel = jnp.concatenate([notsel, jnp.ones((nselp - nb, tq), F32)], axis=0)
            mn_ref[g] = notsel.T.astype(mn_ref.dtype)

    def variant(ncols):
        nb = min(imp_scr.shape[1], -(-(ncols * CMP_STRIDE // SEL_BLOCK) // 8) * 8)
        attend(ncols, nb)
        select(nb)

    needed = ((qi + 1) * tq - CMP_LEN) // CMP_STRIDE + 1
    lo = 0
    for ncols in col_steps:
        pl.when((needed > lo) & (needed <= ncols))(functools.partial(variant, ncols))
        lo = ncols


def _cmp_attention(q, kc, vc, overlap_t, *, tq):
    s = q.shape[0]
    g, nc, _ = kc.shape
    nselp = overlap_t.shape[0]
    ktop = min(N_SELECT, s // SEL_BLOCK)
    assert ktop >= 3
    gw = NSA_REP * DH
    col_steps = tuple(range(LANE, nc + 1, LANE)) if nc % LANE == 0 else (nc,)
    return pl.pallas_call(
        functools.partial(_cmp_kernel, tq=tq, ktop=ktop, col_steps=col_steps),
        grid=(s // tq,),
        in_specs=[
            pl.BlockSpec((tq, g * gw), lambda i: (i, 0)),
            pl.BlockSpec((g, nc, DH), lambda i: (0, 0, 0)),
            pl.BlockSpec((g, nc, DH), lambda i: (0, 0, 0)),
            pl.BlockSpec((nselp, nc), lambda i: (0, 0)),
        ],
        out_specs=[
            pl.BlockSpec((tq, g * gw), lambda i: (i, 0)),
            pl.BlockSpec((g, tq, nselp), lambda i: (0, i, 0)),
        ],
        out_shape=[
            jax.ShapeDtypeStruct((s, g * gw), BF16),
            jax.ShapeDtypeStruct((g, s, nselp), BF16),
        ],
        scratch_shapes=[pltpu.VMEM((g, nselp, tq), F32)],
        compiler_params=_params("parallel"),
        name="nsa_cmp_select",
    )(q, kc, vc, overlap_t)


def _sel_kernel(q_ref, mn_ref, k_ref, v_ref, e_ref, o_ref, ka_scr, va_scr, qa_scr, s_scr, m_scr, acc_scr,
                *, tq, tk):
    qi = pl.program_id(1)
    nhalf = qa_scr.shape[0]
    nper = e_ref.shape[0]

    @pl.when(qi == 0)
    def _():
        for c in range(ka_scr.shape[0] // nper):
            rows = slice(c * nper, (c + 1) * nper)
            ka_scr[rows, 0:DH] = k_ref[rows, :]
            ka_scr[rows, DH:2 * DH] = e_ref[...]
            va_scr[rows, 0:DH] = v_ref[rows, :]
            va_scr[rows, DH:2 * DH] = jnp.ones((nper, DH), BF16)

    q4 = _stack_heads(q_ref[...])
    mn = mn_ref[0]
    for hf in range(nhalf):
        part = mn[:, hf * LANE:(hf + 1) * LANE]
        qa_scr[hf, :, 0:DH] = q4
        qa_scr[hf, :, DH:2 * DH] = jnp.concatenate([part] * NSA_REP, axis=0)
    m_scr[...] = jnp.full_like(m_scr, -jnp.inf)
    acc_scr[...] = jnp.zeros_like(acc_scr)

    def scores(j, slot):
        k0 = pl.multiple_of(j * tk, tk)
        s_scr[slot] = lax.dot_general(qa_scr[k0 // (SEL_BLOCK * LANE)], ka_scr[pl.ds(k0, tk), :],
                                      (((1,), (1,)), ((), ())), preferred_element_type=F32)

    def accumulate(j, slot, masked):
        k0 = pl.multiple_of(j * tk, tk)
        s = s_scr[slot]
        if masked:
            row = lax.broadcasted_iota(jnp.int32, s.shape, 0)
            col = lax.broadcasted_iota(jnp.int32, s.shape, 1)
            s = jnp.where(k0 + col <= qi * tq + (row & (tq - 1)), s, -jnp.inf)
        m_old = m_scr[...]
        m_new = jnp.maximum(m_old, jnp.max(s, axis=-1, keepdims=True))
        p = jnp.exp2(s - m_new).astype(BF16)
        acc_scr[...] = (jnp.exp2(m_old - m_new) * acc_scr[...]
                        + jnp.dot(p, va_scr[pl.ds(k0, tk), :], preferred_element_type=F32))
        m_scr[...] = m_new

    n = (qi * tq + tq - 1) // tk + 1
    unroll = 4
    ntrip = (n - 1) // unroll
    scores(0, 0)

    def trip(i, c):
        for u in range(unroll):
            scores(unroll * i + u + 1, (u + 1) % 2)
            accumulate(unroll * i + u, u % 2, False)
        return c

    lax.fori_loop(0, ntrip, trip, 0)

    first = ntrip * unroll
    for rest in range(1, unroll + 1):
        @pl.when(n - first == rest)
        def _(rest=rest):
            for u in range(rest):
                if u + 1 < rest:
                    scores(first + u + 1, (u + 1) % 2)
                accumulate(first + u, u % 2, u + 1 == rest)

    acc = acc_scr[...]
    o = acc[:, 0:DH] / jnp.maximum(acc[:, DH:2 * DH], 1e-30)
    for r in range(NSA_REP):
        o_ref[:, r * DH:(r + 1) * DH] = o[r * tq:(r + 1) * tq].astype(o_ref.dtype)


def _sel_attention(q, notsel, kv_arr, expand, *, k_col, v_col, tq, tk):
    s = q.shape[0]
    g = notsel.shape[0]
    nselp = notsel.shape[2]
    gw = NSA_REP * DH
    nper = expand.shape[0]
    assert s % nper == 0 and nper % tk == 0 and tk % tq == 0 and s % tq == 0
    once = pl.Buffered(1)
    return pl.pallas_call(
        functools.partial(_sel_kernel, tq=tq, tk=tk),
        grid=(g, s // tq),
        in_specs=[
            pl.BlockSpec((tq, gw), lambda gi, i: (i, gi)),
            pl.BlockSpec((1, tq, nselp), lambda gi, i: (gi, i, 0)),
            pl.BlockSpec((s, DH), lambda gi, i: (0, k_col + gi), pipeline_mode=once),
            pl.BlockSpec((s, DH), lambda gi, i: (0, v_col + gi), pipeline_mode=once),
            pl.BlockSpec((nper, LANE), lambda gi, i: (0, 0), pipeline_mode=once),
        ],
        out_specs=pl.BlockSpec((tq, gw), lambda gi, i: (i, gi)),
        out_shape=jax.ShapeDtypeStruct((s, g * gw), BF16),
        scratch_shapes=[
            pltpu.VMEM((s, 2 * DH), BF16),
            pltpu.VMEM((s, 2 * DH), BF16),
            pltpu.VMEM((nselp // LANE, NSA_REP * tq, 2 * DH), BF16),
            pltpu.VMEM((2, NSA_REP * tq, tk), F32),
            pltpu.VMEM((NSA_REP * tq, 1), F32),
            pltpu.VMEM((NSA_REP * tq, 2 * DH), F32),
        ],
        compiler_params=_params("arbitrary", "arbitrary"),
        name="nsa_sel_attention",
    )(q, notsel, kv_arr, kv_arr, expand)


def _win_kernel(q_ref, *refs, tq, nback):
    nblk = nback + 1
    k_refs = refs[:nblk]
    v_refs = refs[nblk:2 * nblk]
    band_ref, oc_ref, os_ref, gt_ref, o_ref = refs[2 * nblk:]
    qi = pl.program_id(0)
    gw = NSA_REP * DH
    band = band_ref[...]
    col = lax.broadcasted_iota(jnp.int32, band.shape, 1)
    band = jnp.where(col >= (nback - qi) * tq, band, -jnp.inf)
    for g in range(NSA_GROUPS):
        q4 = _stack_heads(q_ref[:, g * gw:(g + 1) * gw])
        kc = jnp.concatenate([r[:, g * DH:(g + 1) * DH] for r in k_refs], axis=0)
        vc = jnp.concatenate([r[:, g * DH:(g + 1) * DH] for r in v_refs], axis=0)
        s = lax.dot_general(q4, kc, (((1,), (1,)), ((), ())), preferred_element_type=F32) + band
        mx = jnp.max(s, axis=-1, keepdims=True)
        mx = jnp.where(jnp.abs(mx) < jnp.inf, mx, 0.0)
        p = jnp.exp2(s - mx)
        p = p / jnp.maximum(jnp.sum(p, axis=-1, keepdims=True), 1e-30)
        ow = jnp.dot(p.astype(BF16), vc, preferred_element_type=F32)
        gates = gt_ref[...]
        for r in range(NSA_REP):
            sl = slice(g * gw + r * DH, g * gw + (r + 1) * DH)
            c0 = 3 * (g * NSA_REP + r)
            out = (gates[:, c0:c0 + 1] * oc_ref[:, sl].astype(F32)
                   + gates[:, c0 + 1:c0 + 2] * os_ref[:, sl].astype(F32)
                   + gates[:, c0 + 2:c0 + 3] * ow[r * tq:(r + 1) * tq])
            o_ref[:, sl] = out.astype(o_ref.dtype)


def _win_attention(q, kv_arr, o_cmp, o_sel, gates, *, k_blk, v_blk, tq):
    s = q.shape[0]
    g = NSA_GROUPS
    gw = NSA_REP * DH
    nback = WINDOW // tq
    assert nback * tq == WINDOW
    qq = jnp.arange(NSA_REP * tq)[:, None] % tq
    kk = jnp.arange((nback + 1) * tq)[None, :]
    band = jnp.where((kk > qq) & (kk <= qq + WINDOW), 0.0, -jnp.inf).astype(F32)

    def kvmap(blk, b):
        def f(i):
            return (jnp.maximum(i - nback + b, 0), blk)
        return f

    k_specs = [pl.BlockSpec((tq, g * DH), kvmap(k_blk, b)) for b in range(nback + 1)]
    v_specs = [pl.BlockSpec((tq, g * DH), kvmap(v_blk, b)) for b in range(nback + 1)]
    wide = pl.BlockSpec((tq, g * gw), lambda i: (i, 0))
    return pl.pallas_call(
        functools.partial(_win_kernel, tq=tq, nback=nback),
        grid=(s // tq,),
        in_specs=([wide] + k_specs + v_specs
                  + [pl.BlockSpec(band.shape, lambda i: (0, 0)), wide, wide,
                     pl.BlockSpec((tq, LANE), lambda i: (i, 0))]),
        out_specs=wide,
        out_shape=jax.ShapeDtypeStruct((s, g * gw), BF16),
        compiler_params=_params("parallel"),
        name="nsa_win_combine",
    )(q, *([kv_arr] * (2 * (nback + 1))), band, o_cmp, o_sel, gates)


def _mlstm_kernel(q_ref, k_ref, v_ref, o_ref, gt_ref, og_ref, y_ref, c_scr, n_scr, m_scr):
    L = q_ref.shape[0]

    @pl.when(pl.program_id(0) == 0)
    def _():
        c_scr[...] = jnp.zeros_like(c_scr)
        n_scr[...] = jnp.zeros_like(n_scr)
        m_scr[...] = jnp.full_like(m_scr, NEG_INIT)

    ri = lax.broadcasted_iota(jnp.int32, (L, L), 0)
    ci = lax.broadcasted_iota(jnp.int32, (L, L), 1)
    eye = ri == ci
    tril = ci <= ri
    triu = ri <= ci
    gates = gt_ref[...]

    def to_row(col):
        return jnp.sum(jnp.where(eye, col, 0.0), axis=0, keepdims=True)

    for h in range(ML_HEADS):
        qh = q_ref[:, h * ML_DK:(h + 1) * ML_DK]
        kh = k_ref[:, h * ML_DK:(h + 1) * ML_DK]
        vh = v_ref[:, h * ML_DV:(h + 1) * ML_DV]
        ig_col = gates[:, h:h + 1]
        fg_col = gates[:, ML_HEADS + h:ML_HEADS + h + 1]
        lf_col = jnp.minimum(fg_col, 0.0) - jnp.log(1.0 + jnp.exp(-jnp.abs(fg_col)))
        lf_row = to_row(lf_col)
        ig_row = to_row(ig_col)
        b_col = jnp.sum(jnp.where(tril, lf_row, 0.0), axis=1, keepdims=True)
        b_row = jnp.sum(jnp.where(triu, lf_col, 0.0), axis=0, keepdims=True)
        m_old = m_scr[h:h + 1, 0:1]
        dmat = jnp.where(tril, b_col - b_row + ig_row, -jnp.inf)
        m_inter = b_col + m_old
        m_t = jnp.maximum(m_inter, jnp.max(dmat, axis=1, keepdims=True))
        qk = lax.dot_general(qh, kh, (((1,), (1,)), ((), ())), preferred_element_type=F32)
        a = jnp.exp(dmat - m_t) * qk
        dec = jnp.exp(m_inter - m_t)
        c_old = c_scr[h]
        n_old = n_scr[h:h + 1, :]
        num = (jnp.dot(a.astype(BF16), vh, preferred_element_type=F32)
               + dec * jnp.dot(qh, c_old.astype(BF16), preferred_element_type=F32))
        qn = jnp.sum(qh.astype(F32) * n_old, axis=1, keepdims=True)
        den = jnp.sum(a, axis=1, keepdims=True) + dec * qn
        hx = num / jnp.maximum(jnp.abs(den), jnp.exp(-m_t))

        b_last = b_col[L - 1:L, :]
        g_col = b_last - b_col + ig_col
        m_new = jnp.maximum(b_last + m_old, jnp.max(g_col, axis=0, keepdims=True))
        w_col = jnp.exp(g_col - m_new)
        cd = jnp.exp(b_last + m_old - m_new)
        kw = kh.astype(F32) * w_col
        c_scr[h] = cd * c_old + lax.dot_general(kw.astype(BF16), vh, (((0,), (0,)), ((), ())),
                                                preferred_element_type=F32)
        n_scr[h:h + 1, :] = cd * n_old + jnp.sum(kw, axis=0, keepdims=True)
        m_scr[h:h + 1, :] = jnp.broadcast_to(m_new, (1, LANE))

        sl = slice(h * ML_DV, (h + 1) * ML_DV)
        hn = _rms(hx, og_ref[:, sl])
        y_ref[:, sl] = (jax.nn.sigmoid(o_ref[:, sl].astype(F32)) * hn).astype(y_ref.dtype)


def _mlstm(qkvo, gates, out_g):
    s = qkvo.shape[0]
    L = min(ML_CHUNK, s)
    assert s % L == 0
    wq = ML_HEADS * ML_DK
    wv = ML_HEADS * ML_DV
    return pl.pallas_call(
        _mlstm_kernel,
        grid=(s // L,),
        in_specs=[
            pl.BlockSpec((L, wq), lambda c: (c, 0)),
            pl.BlockSpec((L, wq), lambda c: (c, 1)),
            pl.BlockSpec((L, wv), lambda c: (c, 1)),
            pl.BlockSpec((L, wv), lambda c: (c, 2)),
            pl.BlockSpec((L, LANE), lambda c: (c, 0)),
            pl.BlockSpec((1, wv), lambda c: (0, 0)),
        ],
        out_specs=pl.BlockSpec((L, wv), lambda c: (c, 0)),
        out_shape=jax.ShapeDtypeStruct((s, wv), BF16),
        scratch_shapes=[
            pltpu.VMEM((ML_HEADS, ML_DK, ML_DV), F32),
            pltpu.VMEM((ML_HEADS, ML_DK), F32),
            pltpu.VMEM((ML_HEADS, LANE), F32),
        ],
        compiler_params=_params("arbitrary"),
        name="mlstm_scan",
    )(qkvo, qkvo, qkvo, qkvo, gates, out_g)


def _pad_cols(a, n):
    return jnp.pad(a, ((0, 0), (0, n - a.shape[1])))


def _nsa_layer(x, norm_g, w_in, b_gate, q_g, k_g, cmp_pos, cmp_w1, cmp_b1, cmp_w2, w_out, layer,
               *, tm, tq, tk):
    s, d = x.shape
    qd = NSA_HEADS * DH
    kvd = NSA_GROUPS * DH
    norm_g = norm_g.reshape(1, d)

    w_main = w_in[layer, :, :qd + 6 * kvd].astype(BF16)
    ones = jnp.ones((kvd,), F32)
    gain = jnp.concatenate([jnp.tile(q_g, NSA_HEADS) * (DH ** -0.5 * LOG2E), ones, ones,
                            jnp.tile(k_g[1], NSA_GROUPS), ones, jnp.tile(k_g[2], NSA_GROUPS), ones])
    flag = jnp.concatenate([jnp.ones((qd,), F32), 0 * ones, 0 * ones, ones, 0 * ones, ones, 0 * ones])
    w_gate = _pad_cols(w_in[layer, :, qd + 6 * kvd:], LANE).astype(BF16)
    bias = _pad_cols(b_gate.reshape(1, -1), LANE)
    tn = 2 * kvd
    assert qd % tn == 0
    proj, gates, xc = _proj(x, norm_g, w_main, gain.reshape(1, -1), flag.reshape(1, -1), w_gate, bias,
                            mode="headnorm", gate_mode="sigmoid", tm=min(2 * tm, s), tn=tn, name="nsa_proj",
                            chunk_step=qd // tn)
    nc = s // CMP_STRIDE
    xc = xc.reshape(2, NSA_GROUPS, nc, CMP_STRIDE * DH)
    kvc = _compress(xc, cmp_w1.astype(BF16), cmp_b1.reshape(2, 1, DH), cmp_w2.astype(BF16),
                    cmp_pos.reshape(2, 1, CMP_LEN * DH).astype(BF16), k_g[0].reshape(1, DH))

    n_sel = s // SEL_BLOCK
    nselp = -(-n_sel // LANE) * LANE
    cstart = jnp.arange(nc) * CMP_STRIDE
    sstart = jnp.arange(nselp) * SEL_BLOCK
    overlap_t = ((cstart[None, :] < sstart[:, None] + SEL_BLOCK)
                 & (cstart[None, :] + CMP_LEN > sstart[:, None])
                 & (jnp.arange(nselp)[:, None] < n_sel)
                 & (jnp.arange(nc)[None, :] < nc - 1)).astype(BF16)
    o_cmp, notsel = _cmp_attention(proj, kvc[0], kvc[1], overlap_t, tq=min(2 * tq, s))

    blk = jnp.arange(min(s, SEL_BLOCK * LANE)) // SEL_BLOCK
    expand =jnp.where(blk[:, None] == jnp.arange(LANE)[None, :], MASK_BIAS, 0.0).astype(BF16)
    col0 = qd // DH
    o_sel = _sel_attention(proj, notsel, proj, expand, k_col=col0 + 2 * NSA_GROUPS,
                           v_col=col0 + 3 * NSA_GROUPS, tq=min(2 * tq, s), tk=tk)
    mixed = _win_attention(proj, proj, o_cmp, o_sel, gates, k_blk=(qd + 4 * kvd) // kvd,
                           v_blk=(qd + 5 * kvd) // kvd, tq=min(2 * tq, s))
    return _matmul_res(mixed, w_out, layer, x, tm=tm, name="nsa_out_proj")


def _mlstm_layer(x, norm_g, w_in, b_if, out_g, w_out, layer, *, tm):
    s, d = x.shape
    norm_g = norm_g.reshape(1, d)
    wq = ML_HEADS * ML_DK
    wv = ML_HEADS * ML_DV
    nmain = 2 * wq + 2 * wv
    w_main = w_in[layer, :, :nmain].astype(BF16)
    scale = jnp.concatenate([jnp.ones((wq,), F32), jnp.full((wq,), ML_DK ** -0.5, F32),
                             jnp.ones((2 * wv,), F32)]).reshape(1, -1)
    w_gate = _pad_cols(w_in[layer, :, nmain:], LANE).astype(BF16)
    bias = _pad_cols(b_if.reshape(1, -1), LANE)
    qkvo, gates = _proj(x, norm_g, w_main, scale, scale, w_gate, bias, mode="scale", gate_mode="bias",
                        tm=min(2 * tm, s), tn=1024, name="ml_proj")
    y = _mlstm(qkvo, gates, out_g.reshape(1, -1))
    return _matmul_res(y, w_out, layer, x, tm=tm, name="ml_out_proj")


def _ffn_layer(x, norm_g, wg, wu, wd, layer, *, tm):
    return _ffn(x, norm_g.reshape(1, -1), wg, wu, wd, layer, tm=tm, tf=512)


def kernel(x, norm_mix_g, norm_ffn_g, nsa_w_in, nsa_b_gate, nsa_q_norm_g, nsa_k_norm_g, nsa_cmp_pos,
           nsa_cmp_w1, nsa_cmp_b1, nsa_cmp_w2, nsa_w_out, ml_w_in, ml_b_if, ml_out_norm_g, ml_w_out,
           ffn_w_gate, ffn_w_up, ffn_w_down):
    b, s, d = x.shape
    depth = norm_mix_g.shape[0]
    tm = min(512, s)
    tq = 128
    tk = min(1024, s)
    nsa_w_out, ml_w_out = nsa_w_out.astype(BF16), ml_w_out.astype(BF16)
    ffn_w = (ffn_w_gate.astype(BF16), ffn_w_up.astype(BF16), ffn_w_down.astype(BF16))
    outs = []
    for bi in range(b):
        xb = x[bi]
        for i in range(depth):
            j = i // 2
            if i % 2 == 0:
                xb = _nsa_layer(xb, norm_mix_g[i], nsa_w_in, nsa_b_gate[j], nsa_q_norm_g[j],
                                nsa_k_norm_g[j], nsa_cmp_pos[j], nsa_cmp_w1[j], nsa_cmp_b1[j],
                                nsa_cmp_w2[j], nsa_w_out, j, tm=tm, tq=tq, tk=tk)
            else:
                xb = _mlstm_layer(xb, norm_mix_g[i], ml_w_in, ml_b_if[j], ml_out_norm_g[j],
                                  ml_w_out, j, tm=tm)
            xb = _ffn_layer(xb, norm_ffn_g[i], *ffn_w, i, tm=min(2 * tm, s))
        outs.append(xb)
    return jnp.stack(outs, axis=0)
```

```python
import functools

import jax
import jax.numpy as jnp
from jax import lax
from jax.experimental import pallas as pl
from jax.experimental.pallas import tpu as pltpu

F32 = jnp.float32
BF16 = jnp.bfloat16

EPS = 1e-6
NEG_INIT = -1e30
LOG2E = 1.4426950408889634

LANE = 128
VMEM_LIMIT = 56 * 1024 * 1024
PROJ_CHUNK = 512

NSA_HEADS = 16
NSA_GROUPS = 4
NSA_REP = NSA_HEADS // NSA_GROUPS
DH = 128
CMP_LEN = 32
CMP_STRIDE = 16
SEL_BLOCK = 64
N_SELECT = 16
WINDOW = 512
ML_HEADS = 8
ML_DK = 128
ML_DV = 256
ML_CHUNK = 512

MASK_BIAS = -(2.0 ** 100)


def _params(*sem):
    return pltpu.CompilerParams(dimension_semantics=sem, vmem_limit_bytes=VMEM_LIMIT)


def _rms(x, g):
    ms = jnp.mean(x * x, axis=-1, keepdims=True)
    return x * lax.rsqrt(ms + EPS) * g


def _proj_kernel(x_ref, g_ref, w_ref, a_ref, b_ref, wg_ref, bg_ref, o_ref, og_ref, *rest,
                 mode, gate_mode, chunk_step):
    if chunk_step is None:
        (h_scr,) = rest
    else:
        oc_ref, h_scr, y_scr = rest

    @pl.when(pl.program_id(1) == 0)
    def _():
        h = _rms(x_ref[...], g_ref[...]).astype(BF16)
        h_scr[...] = h
        gl = jnp.dot(h, wg_ref[...], preferred_element_type=F32) + bg_ref[...]
        og_ref[...] = jax.nn.sigmoid(gl) if gate_mode == "sigmoid" else gl

    h = h_scr[...]
    for c in range(o_ref.shape[1] // PROJ_CHUNK):
        cs = slice(c * PROJ_CHUNK, (c + 1) * PROJ_CHUNK)
        y = jnp.dot(h, w_ref[:, cs], preferred_element_type=F32)
        if mode == "headnorm":
            for u in range(PROJ_CHUNK // LANE):
                sl = slice(c * PROJ_CHUNK + u * LANE, c * PROJ_CHUNK + (u + 1) * LANE)
                yc = y[:, u * LANE:(u + 1) * LANE]
                ms = jnp.mean(yc * yc, axis=-1, keepdims=True)
                mult = jnp.where(b_ref[:, sl] > 0.0, lax.rsqrt(ms + EPS), 1.0) * a_ref[:, sl]
                o_ref[:, sl] = (yc * mult).astype(o_ref.dtype)
        elif mode == "scale":
            o_ref[:, cs] = (y * a_ref[:, cs]).astype(o_ref.dtype)
        else:
            raise ValueError(mode)

    if chunk_step is not None:
        @pl.when(pl.program_id(1) == chunk_step)
        def _():
            nchunk = y_scr.shape[0] // CMP_STRIDE
            for blk in range(o_ref.shape[1] // LANE):
                y_scr[...] = o_ref[:, blk * LANE:(blk + 1) * LANE].astype(F32)
                for l in range(CMP_STRIDE):
                    rows = y_scr[pl.ds(l, nchunk, stride=CMP_STRIDE), :]
                    oc_ref[blk, :, l * LANE:(l + 1) * LANE] = rows.astype(oc_ref.dtype)


def _proj(x, g, w, layer, a, b, wg, bg, *, mode, gate_mode, tm, tn, name, chunk_step=None):
    s, d = x.shape
    n = a.shape[1]
    ng = wg.shape[1]
    assert s % tm == 0 and n % tn == 0 and tn % PROJ_CHUNK == 0 and n <= w.shape[2]
    out_specs = [
        pl.BlockSpec((tm, tn), lambda i, j: (i, j)),
        pl.BlockSpec((tm, ng), lambda i, j: (i, 0)),
    ]
    out_shape = [
        jax.ShapeDtypeStruct((s, n), BF16),
        jax.ShapeDtypeStruct((s, ng), F32),
    ]
    scratch = [pltpu.VMEM((tm, d), BF16)]
    if chunk_step is not None:
        assert tm % (8 * CMP_STRIDE) == 0
        nblk = tn // LANE
        out_specs.append(pl.BlockSpec((nblk, tm // CMP_STRIDE, CMP_STRIDE * LANE), lambda i, j: (0, i, 0)))
        out_shape.append(jax.ShapeDtypeStruct((nblk, s // CMP_STRIDE, CMP_STRIDE * LANE), BF16))
        scratch.append(pltpu.VMEM((tm, LANE), F32))
    return pl.pallas_call(
        functools.partial(_proj_kernel, mode=mode, gate_mode=gate_mode, chunk_step=chunk_step),
        grid=(s // tm, n // tn),
        in_specs=[
            pl.BlockSpec((tm, d), lambda i, j: (i, 0)),
            pl.BlockSpec((1, d), lambda i, j: (0, 0)),
            pl.BlockSpec((None, d, tn), lambda i, j: (layer, 0, j)),
            pl.BlockSpec((1, tn), lambda i, j: (0, j)),
            pl.BlockSpec((1, tn), lambda i, j: (0, j)),
            pl.BlockSpec((d, ng), lambda i, j: (0, 0)),
            pl.BlockSpec((1, ng), lambda i, j: (0, 0)),
        ],
        out_specs=out_specs,
        out_shape=out_shape,
        scratch_shapes=scratch,
        compiler_params=_params("parallel", "arbitrary"),
        name=name,
    )(x, g, w, a, b, wg, bg)


def _matmul_res_kernel(a_ref, w_ref, r_ref, o_ref):
    a = a_ref[...]
    for c in range(o_ref.shape[1] // PROJ_CHUNK):
        cs = slice(c * PROJ_CHUNK, (c + 1) * PROJ_CHUNK)
        o_ref[:, cs] = r_ref[:, cs] + jnp.dot(a, w_ref[:, cs], preferred_element_type=F32)


def _matmul_res(a, w, layer, res, *, tm, name):
    s, k = a.shape
    n = w.shape[2]
    assert s % tm == 0 and n % PROJ_CHUNK == 0
    return pl.pallas_call(
        _matmul_res_kernel,
        grid=(s // tm,),
        in_specs=[
            pl.BlockSpec((tm, k), lambda i: (i, 0)),
            pl.BlockSpec((None, k, n), lambda i: (layer, 0, 0), pipeline_mode=pl.Buffered(1)),
            pl.BlockSpec((tm, n), lambda i: (i, 0)),
        ],
        out_specs=pl.BlockSpec((tm, n), lambda i: (i, 0)),
        out_shape=jax.ShapeDtypeStruct((s, n), F32),
        compiler_params=_params("parallel"),
        name=name,
    )(a, w, res)


def _ffn_kernel(x_ref, g_ref, wg_ref, wu_ref, wd_ref, o_ref, h_scr):
    @pl.when(pl.program_id(1) == 0)
    def _():
        x = x_ref[...]
        h_scr[...] = _rms(x, g_ref[...]).astype(BF16)
        o_ref[...] = x

    half = h_scr.shape[0] // 2
    for u in range(2):
        rs = slice(u * half, (u + 1) * half)
        h = h_scr[rs, :]
        gate = jnp.dot(h, wg_ref[...], preferred_element_type=F32)
        up = jnp.dot(h, wu_ref[...], preferred_element_type=F32)
        act = (gate * jax.nn.sigmoid(gate) * up).astype(BF16)
        o_ref[rs, :] += jnp.dot(act, wd_ref[...], preferred_element_type=F32)


def _ffn(x, g, wg, wu, wd, layer, *, tm, tf):
    s, d = x.shape
    dff = wg.shape[2]
    assert s % tm == 0 and dff % tf == 0
    return pl.pallas_call(
        _ffn_kernel,
        grid=(s // tm, dff // tf),
        in_specs=[
            pl.BlockSpec((tm, d), lambda i, f: (i, 0)),
            pl.BlockSpec((1, d), lambda i, f: (0, 0)),
            pl.BlockSpec((None, d, tf), lambda i, f: (layer, 0, f)),
            pl.BlockSpec((None, d, tf), lambda i, f: (layer, 0, f)),
            pl.BlockSpec((None, tf, d), lambda i, f: (layer, f, 0)),
        ],
        out_specs=pl.BlockSpec((tm, d), lambda i, f: (i, 0)),
        out_shape=jax.ShapeDtypeStruct((s, d), F32),
        scratch_shapes=[pltpu.VMEM((tm, d), BF16)],
        compiler_params=_params("parallel", "arbitrary"),
        name="ffn",
    )(x, g, wg, wu, wd)


def _compress_kernel(x_ref, w1_ref, b1_ref, w2_ref, pos_ref, kg_ref, o_ref):
    half = CMP_STRIDE * DH
    x = x_ref[0, 0]
    nc = x.shape[0]
    top = jnp.dot(x, w1_ref[0, :half, :], preferred_element_type=F32)
    bot = jnp.dot(x, w1_ref[0, half:, :], preferred_element_type=F32)
    bot = pltpu.roll(bot, nc - 1, 0)
    row = lax.broadcasted_iota(jnp.int32, bot.shape, 0)
    bot = jnp.where(row == nc - 1, 0.0, bot)
    pos8 = jnp.broadcast_to(pos_ref[0], (8, 2 * half))
    posb = jnp.dot(pos8, w1_ref[0], preferred_element_type=F32)[0:1]
    hdn = jax.nn.gelu(top + bot + posb + b1_ref[0])
    y = jnp.dot(hdn.astype(BF16), w2_ref[0], preferred_element_type=F32)
    yn = _rms(y, kg_ref[...])
    o_ref[0, 0] = jnp.where(pl.program_id(0) == 0, yn, y).astype(o_ref.dtype)


def _compress(xc, w1, b1, w2, pos, kg):
    _, g, nc, k = xc.shape
    return pl.pallas_call(
        _compress_kernel,
        grid=(2, g),
        in_specs=[
            pl.BlockSpec((1, 1, nc, k), lambda s, i: (s, i, 0, 0)),
            pl.BlockSpec((1, 2 * k, DH), lambda s, i: (s, 0, 0)),
            pl.BlockSpec((1, 1, DH), lambda s, i: (s, 0, 0)),
            pl.BlockSpec((1, DH, DH), lambda s, i: (s, 0, 0)),
            pl.BlockSpec((1, 1, 2 * k), lambda s, i: (s, 0, 0)),
            pl.BlockSpec((1, DH), lambda s, i: (0, 0)),
        ],
        out_specs=pl.BlockSpec((1, 1, nc, DH), lambda s, i: (s, i, 0, 0)),
        out_shape=jax.ShapeDtypeStruct((2, g, nc, DH), BF16),
        compiler_params=_params("parallel", "parallel"),
        name="nsa_compress",
    )(xc, w1, b1, w2, pos, kg)


def _stack_heads(qb):
    return jnp.concatenate([qb[:, r * DH:(r + 1) * DH] for r in range(NSA_REP)], axis=0)


def _cmp_kernel(q_ref, kc_ref, vc_ref, ovt_ref, o_ref, mn_ref, imp_scr, *, tq, ktop, col_steps):
    qi = pl.program_id(0)
    gw = NSA_REP * DH

    def attend(ncols, nb):
        for g in range(NSA_GROUPS):
            q4 = _stack_heads(q_ref[:, g * gw:(g + 1) * gw])
            s = lax.dot_general(q4, kc_ref[g, 0:ncols, :], (((1,), (1,)), ((), ())),
                                preferred_element_type=F32)
            row = lax.broadcasted_iota(jnp.int32, s.shape, 0)
            col = lax.broadcasted_iota(jnp.int32, s.shape, 1)
            t = qi * tq + (row & (tq - 1))
            s = jnp.where(col * CMP_STRIDE + (CMP_LEN - 1) <= t, s, -jnp.inf)
            mx = jnp.max(s, axis=-1, keepdims=True)
            mx = jnp.where(jnp.abs(mx) < jnp.inf, mx, 0.0)
            p = jnp.exp2(s - mx)
            p = p / jnp.maximum(jnp.sum(p, axis=-1, keepdims=True), 1e-30)
            o = jnp.dot(p.astype(BF16), vc_ref[g, 0:ncols, :], preferred_element_type=F32)
            for r in range(NSA_REP):
                o_ref[:, g * gw + r * DH:g * gw + (r + 1) * DH] = o[r * tq:(r + 1) * tq].astype(o_ref.dtype)
            ps = p[0:tq]
            for r in range(1, NSA_REP):
                ps = ps + p[r * tq:(r + 1) * tq]
            imp_scr[g, 0:nb, :] = lax.dot_general(ovt_ref[0:nb, 0:ncols], ps.astype(BF16),
                                                  (((1,), (1,)), ((), ())), preferred_element_type=F32)

    def select(nb):
        nselp = imp_scr.shape[1]
        jj = lax.broadcasted_iota(jnp.int32, (nb, tq), 0)
        cur = (qi * tq + lax.broadcasted_iota(jnp.int32, (nb, tq), 1)) // SEL_BLOCK
        forced = (jj == 0) | (jj == cur) | (jj == cur - 1)
        free = (jj >= 1) & (jj <= cur - 2)
        jjf = jj.astype(F32)
        scores = [jnp.where(free, imp_scr[g, 0:nb, :], -jnp.inf) for g in range(NSA_GROUPS)]
        for _ in range(ktop - 3):
            for g in range(NSA_GROUPS):
                top = jnp.max(scores[g], axis=0, keepdims=True)
                first = jnp.min(jnp.where(scores[g] == top, jjf, 1e9), axis=0, keepdims=True)
                scores[g] = jnp.where(jjf == first, -jnp.inf, scores[g])
        for g in range(NSA_GROUPS):
            picked = forced | (free & (scores[g] == -jnp.inf))
            notsel = jnp.where(picked, 0.0, 1.0)
            if nb < nselp:
                notsel = jnp.concatenate([notsel, jnp.ones((nselp - nb, tq), F32)], axis=0)
            mn_ref[g] = notsel.T.astype(mn_ref.dtype)

    def variant(ncols):
        nb = min(imp_scr.shape[1], -(-(ncols * CMP_STRIDE // SEL_BLOCK) // 8) * 8)
        attend(ncols, nb)
        select(nb)

    needed = ((qi + 1) * tq - CMP_LEN) // CMP_STRIDE + 1
    lo = 0
    for ncols in col_steps:
        pl.when((needed > lo) & (needed <= ncols))(functools.partial(variant, ncols))
        lo = ncols


def _cmp_attention(q, kc, vc, overlap_t, *, tq):
    s = q.shape[0]
    g, nc, _ = kc.shape
    nselp = overlap_t.shape[0]
    ktop = min(N_SELECT, s // SEL_BLOCK)
    assert ktop >= 3
    gw = NSA_REP * DH
    col_steps = tuple(range(2 * LANE, nc + 1, 2 * LANE)) if nc % (2 * LANE) == 0 else (nc,)
    return pl.pallas_call(
        functools.partial(_cmp_kernel, tq=tq, ktop=ktop, col_steps=col_steps),
        grid=(s // tq,),
        in_specs=[
            pl.BlockSpec((tq, g * gw), lambda i: (i, 0)),
            pl.BlockSpec((g, nc, DH), lambda i: (0, 0, 0)),
            pl.BlockSpec((g, nc, DH), lambda i: (0, 0, 0)),
            pl.BlockSpec((nselp, nc), lambda i: (0, 0)),
        ],
        out_specs=[
            pl.BlockSpec((tq, g * gw), lambda i: (i, 0)),
            pl.BlockSpec((g, tq, nselp), lambda i: (0, i, 0)),
        ],
        out_shape=[
            jax.ShapeDtypeStruct((s, g * gw), BF16),
            jax.ShapeDtypeStruct((g, s, nselp), BF16),
        ],
        scratch_shapes=[pltpu.VMEM((g, nselp, tq), F32)],
        compiler_params=_params("parallel"),
        name="nsa_cmp_select",
    )(q, kc, vc, overlap_t)


def _sel_kernel(q_ref, mn_ref, k_ref, v_ref, e_ref, o_ref, ka_scr, va_scr, qa_scr, s_scr, m_scr, acc_scr,
                *, tq, tk):
    qi = pl.program_id(1)
    nhalf = qa_scr.shape[0]
    nper = e_ref.shape[0]

    @pl.when(qi == 0)
    def _():
        for c in range(ka_scr.shape[0] // nper):
            rows = slice(c * nper, (c + 1) * nper)
            ka_scr[rows, 0:DH] = k_ref[rows, :]
            ka_scr[rows, DH:2 * DH] = e_ref[...]
            va_scr[rows, 0:DH] = v_ref[rows, :]
            va_scr[rows, DH:2 * DH] = jnp.ones((nper, DH), BF16)

    q4 = _stack_heads(q_ref[...])
    mn = mn_ref[0]
    for hf in range(nhalf):
        part = mn[:, hf * LANE:(hf + 1) * LANE]
        qa_scr[hf, :, 0:DH] = q4
        qa_scr[hf, :, DH:2 * DH] = jnp.concatenate([part] * NSA_REP, axis=0)
    m_scr[...] = jnp.full_like(m_scr, -jnp.inf)
    acc_scr[...] = jnp.zeros_like(acc_scr)

    def scores(j, slot):
        k0 = pl.multiple_of(j * tk, tk)
        s_scr[slot] = lax.dot_general(qa_scr[k0 // (SEL_BLOCK * LANE)], ka_scr[pl.ds(k0, tk), :],
                                      (((1,), (1,)), ((), ())), preferred_element_type=F32)

    def accumulate(j, slot, masked):
        k0 = pl.multiple_of(j * tk, tk)
        s = s_scr[slot]
        if masked:
            row = lax.broadcasted_iota(jnp.int32, s.shape, 0)
            col = lax.broadcasted_iota(jnp.int32, s.shape, 1)
            s = jnp.where(k0 + col <= qi * tq + (row & (tq - 1)), s, -jnp.inf)
        m_old = m_scr[...]
        m_new = jnp.maximum(m_old, jnp.max(s, axis=-1, keepdims=True))
        p = jnp.exp2(s - m_new).astype(BF16)
        acc_scr[...] = (jnp.exp2(m_old - m_new) * acc_scr[...]
                        + jnp.dot(p, va_scr[pl.ds(k0, tk), :], preferred_element_type=F32))
        m_scr[...] = m_new

    n = (qi * tq + tq - 1) // tk + 1
    unroll = 4
    ntrip = (n - 1) // unroll
    scores(0, 0)

    def trip(i, c):
        for u in range(unroll):
            scores(unroll * i + u + 1, (u + 1) % 2)
            accumulate(unroll * i + u, u % 2, False)
        return c

    lax.fori_loop(0, ntrip, trip, 0)

    first = ntrip * unroll
    for rest in range(1, unroll + 1):
        @pl.when(n - first == rest)
        def _(rest=rest):
            for u in range(rest):
                if u + 1 < rest:
                    scores(first + u + 1, (u + 1) % 2)
                accumulate(first + u, u % 2, u + 1 == rest)

    acc = acc_scr[...]
    o = acc[:, 0:DH] / jnp.maximum(acc[:, DH:2 * DH], 1e-30)
    for r in range(NSA_REP):
        o_ref[:, r * DH:(r + 1) * DH] = o[r * tq:(r + 1) * tq].astype(o_ref.dtype)


def _sel_attention(q, notsel, kv_arr, expand, *, k_col, v_col, tq, tk):
    s = q.shape[0]
    g = notsel.shape[0]
    nselp = notsel.shape[2]
    gw = NSA_REP * DH
    nper = expand.shape[0]
    assert s % nper == 0 and nper % tk == 0 and tk % tq == 0 and s % tq == 0
    once = pl.Buffered(1)
    return pl.pallas_call(
        functools.partial(_sel_kernel, tq=tq, tk=tk),
        grid=(g, s // tq),
        in_specs=[
            pl.BlockSpec((tq, gw), lambda gi, i: (i, gi)),
            pl.BlockSpec((1, tq, nselp), lambda gi, i: (gi, i, 0)),
            pl.BlockSpec((s, DH), lambda gi, i: (0, k_col + gi), pipeline_mode=once),
            pl.BlockSpec((s, DH), lambda gi, i: (0, v_col + gi), pipeline_mode=once),
            pl.BlockSpec((nper, LANE), lambda gi, i: (0, 0), pipeline_mode=once),
        ],
        out_specs=pl.BlockSpec((tq, gw), lambda gi, i: (i, gi)),
        out_shape=jax.ShapeDtypeStruct((s, g * gw), BF16),
        scratch_shapes=[
            pltpu.VMEM((s, 2 * DH), BF16),
            pltpu.VMEM((s, 2 * DH), BF16),
            pltpu.VMEM((nselp // LANE, NSA_REP * tq, 2 * DH), BF16),
            pltpu.VMEM((2, NSA_REP * tq, tk), F32),
            pltpu.VMEM((NSA_REP * tq, 1), F32),
            pltpu.VMEM((NSA_REP * tq, 2 * DH), F32),
        ],
        compiler_params=_params("arbitrary", "arbitrary"),
        name="nsa_sel_attention",
    )(q, notsel, kv_arr, kv_arr, expand)


def _win_kernel(q_ref, *refs, tq, nback):
    nblk = nback + 1
    k_refs = refs[:nblk]
    v_refs = refs[nblk:2 * nblk]
    band_ref, oc_ref, os_ref, gt_ref, o_ref = refs[2 * nblk:]
    qi = pl.program_id(0)
    gw = NSA_REP * DH
    band = band_ref[...]
    col = lax.broadcasted_iota(jnp.int32, band.shape, 1)
    band = jnp.where(col >= (nback - qi) * tq, band, -jnp.inf)
    for g in range(NSA_GROUPS):
        q4 = _stack_heads(q_ref[:, g * gw:(g + 1) * gw])
        kc = jnp.concatenate([r[:, g * DH:(g + 1) * DH] for r in k_refs], axis=0)
        vc = jnp.concatenate([r[:, g * DH:(g + 1) * DH] for r in v_refs], axis=0)
        s = lax.dot_general(q4, kc, (((1,), (1,)), ((), ())), preferred_element_type=F32) + band
        mx = jnp.max(s, axis=-1, keepdims=True)
        mx = jnp.where(jnp.abs(mx) < jnp.inf, mx, 0.0)
        p = jnp.exp2(s - mx)
        p = p / jnp.maximum(jnp.sum(p, axis=-1, keepdims=True), 1e-30)
        ow = jnp.dot(p.astype(BF16), vc, preferred_element_type=F32)
        gates = gt_ref[...]
        for r in range(NSA_REP):
            sl = slice(g * gw + r * DH, g * gw + (r + 1) * DH)
            c0 = 3 * (g * NSA_REP + r)
            out = (gates[:, c0:c0 + 1] * oc_ref[:, sl].astype(F32)
                   + gates[:, c0 + 1:c0 + 2] * os_ref[:, sl].astype(F32)
                   + gates[:, c0 + 2:c0 + 3] * ow[r * tq:(r + 1) * tq])
            o_ref[:, sl] = out.astype(o_ref.dtype)


def _win_attention(q, kv_arr, o_cmp, o_sel, gates, *, k_blk, v_blk, tq):
    s = q.shape[0]
    g = NSA_GROUPS
    gw = NSA_REP * DH
    nback = WINDOW // tq
    assert nback * tq == WINDOW
    qq = jnp.arange(NSA_REP * tq)[:, None] % tq
    kk = jnp.arange((nback + 1) * tq)[None, :]
    band = jnp.where((kk > qq) & (kk <= qq + WINDOW), 0.0, -jnp.inf).astype(F32)

    def kvmap(blk, b):
        def f(i):
            return (jnp.maximum(i - nback + b, 0), blk)
        return f

    k_specs = [pl.BlockSpec((tq, g * DH), kvmap(k_blk, b)) for b in range(nback + 1)]
    v_specs = [pl.BlockSpec((tq, g * DH), kvmap(v_blk, b)) for b in range(nback + 1)]
    wide = pl.BlockSpec((tq, g * gw), lambda i: (i, 0))
    return pl.pallas_call(
        functools.partial(_win_kernel, tq=tq, nback=nback),
        grid=(s // tq,),
        in_specs=([wide] + k_specs + v_specs
                  + [pl.BlockSpec(band.shape, lambda i: (0, 0)), wide, wide,
                     pl.BlockSpec((tq, LANE), lambda i: (i, 0))]),
        out_specs=wide,
        out_shape=jax.ShapeDtypeStruct((s, g * gw), BF16),
        compiler_params=_params("parallel"),
        name="nsa_win_combine",
    )(q, *([kv_arr] * (2 * (nback + 1))), band, o_cmp, o_sel, gates)


def _mlstm_kernel(q_ref, k_ref, v_ref, o_ref, gt_ref, og_ref, y_ref, c_scr, n_scr, m_scr):
    L = q_ref.shape[0]

    @pl.when(pl.program_id(0) == 0)
    def _():
        c_scr[...] = jnp.zeros_like(c_scr)
        n_scr[...] = jnp.zeros_like(n_scr)
        m_scr[...] = jnp.full_like(m_scr, NEG_INIT)

    ri = lax.broadcasted_iota(jnp.int32, (L, L), 0)
    ci = lax.broadcasted_iota(jnp.int32, (L, L), 1)
    eye = ri == ci
    tril = ci <= ri
    triu = ri <= ci
    gates = gt_ref[...]

    def to_row(col):
        return jnp.sum(jnp.where(eye, col, 0.0), axis=0, keepdims=True)

    for h in range(ML_HEADS):
        qh = q_ref[:, h * ML_DK:(h + 1) * ML_DK]
        kh = k_ref[:, h * ML_DK:(h + 1) * ML_DK]
        vh = v_ref[:, h * ML_DV:(h + 1) * ML_DV]
        ig_col = gates[:, h:h + 1]
        fg_col = gates[:, ML_HEADS + h:ML_HEADS + h + 1]
        lf_col = jnp.minimum(fg_col, 0.0) - jnp.log(1.0 + jnp.exp(-jnp.abs(fg_col)))
        lf_row = to_row(lf_col)
        ig_row = to_row(ig_col)
        b_col = jnp.sum(jnp.where(tril, lf_row, 0.0), axis=1, keepdims=True)
        b_row = jnp.sum(jnp.where(triu, lf_col, 0.0), axis=0, keepdims=True)
        m_old = m_scr[h:h + 1, 0:1]
        dmat = jnp.where(tril, b_col - b_row + ig_row, -jnp.inf)
        m_inter = b_col + m_old
        m_t = jnp.maximum(m_inter, jnp.max(dmat, axis=1, keepdims=True))
        qk = lax.dot_general(qh, kh, (((1,), (1,)), ((), ())), preferred_element_type=F32)
        a = jnp.exp(dmat - m_t) * qk
        dec = jnp.exp(m_inter - m_t)
        c_old = c_scr[h]
        n_old = n_scr[h:h + 1, :]
        num = (jnp.dot(a.astype(BF16), vh, preferred_element_type=F32)
               + dec * jnp.dot(qh, c_old.astype(BF16), preferred_element_type=F32))
        qn = jnp.sum(qh.astype(F32) * n_old, axis=1, keepdims=True)
        den = jnp.sum(a, axis=1, keepdims=True) + dec * qn
        hx = num / jnp.maximum(jnp.abs(den), jnp.exp(-m_t))

        b_last = b_col[L - 1:L, :]
        g_col = b_last - b_col + ig_col
        m_new = jnp.maximum(b_last + m_old, jnp.max(g_col, axis=0, keepdims=True))
        w_col = jnp.exp(g_col - m_new)
        cd = jnp.exp(b_last + m_old - m_new)
        kw = kh.astype(F32) * w_col
        c_scr[h] = cd * c_old + lax.dot_general(kw.astype(BF16), vh, (((0,), (0,)), ((), ())),
                                                preferred_element_type=F32)
        n_scr[h:h + 1, :] = cd * n_old + jnp.sum(kw, axis=0, keepdims=True)
        m_scr[h:h + 1, :] = jnp.broadcast_to(m_new, (1, LANE))

        sl = slice(h * ML_DV, (h + 1) * ML_DV)
        hn = _rms(hx, og_ref[:, sl])
        y_ref[:, sl] = (jax.nn.sigmoid(o_ref[:, sl].astype(F32)) * hn).astype(y_ref.dtype)


def _mlstm(qkvo, gates, out_g):
    s = qkvo.shape[0]
    L = min(ML_CHUNK, s)
    assert s % L == 0
    wq = ML_HEADS * ML_DK
    wv = ML_HEADS * ML_DV
    return pl.pallas_call(
        _mlstm_kernel,
        grid=(s // L,),
        in_specs=[
            pl.BlockSpec((L, wq), lambda c: (c, 0)),
            pl.BlockSpec((L, wq), lambda c: (c, 1)),
            pl.BlockSpec((L, wv), lambda c: (c, 1)),
            pl.BlockSpec((L, wv), lambda c: (c, 2)),
            pl.BlockSpec((L, LANE), lambda c: (c, 0)),
            pl.BlockSpec((1, wv), lambda c: (0, 0)),
        ],
        out_specs=pl.BlockSpec((L, wv), lambda c: (c, 0)),
        out_shape=jax.ShapeDtypeStruct((s, wv), BF16),
        scratch_shapes=[
            pltpu.VMEM((ML_HEADS, ML_DK, ML_DV), F32),
            pltpu.VMEM((ML_HEADS, ML_DK), F32),
            pltpu.VMEM((ML_HEADS, LANE), F32),
        ],
        compiler_params=_params("arbitrary"),
        name="mlstm_scan",
    )(qkvo, qkvo, qkvo, qkvo, gates, out_g)


def _pad_cols(a, n):
    return jnp.pad(a, ((0, 0), (0, n - a.shape[1])))


def _nsa_layer(x, norm_g, w_in, b_gate, q_g, k_g, cmp_pos, cmp_w1, cmp_b1, cmp_w2, w_out, layer,
               *, tm, tq, tk):
    s, d = x.shape
    qd = NSA_HEADS * DH
    kvd = NSA_GROUPS * DH
    norm_g = norm_g.reshape(1, d)

    ones = jnp.ones((kvd,), F32)
    gain = jnp.concatenate([jnp.tile(q_g, NSA_HEADS) * (DH ** -0.5 * LOG2E), ones, ones,
                            jnp.tile(k_g[1], NSA_GROUPS), ones, jnp.tile(k_g[2], NSA_GROUPS), ones])
    flag = jnp.concatenate([jnp.ones((qd,), F32), 0 * ones, 0 * ones, ones, 0 * ones, ones, 0 * ones])
    w_gate = _pad_cols(w_in[layer, :, qd + 6 * kvd:], LANE)
    bias = _pad_cols(b_gate.reshape(1, -1), LANE)
    tn = 2 * kvd
    assert qd % tn == 0
    proj, gates, xc = _proj(x, norm_g, w_in, layer, gain.reshape(1, -1), flag.reshape(1, -1), w_gate, bias,
                            mode="headnorm", gate_mode="sigmoid", tm=min(2 * tm, s), tn=tn, name="nsa_proj",
                            chunk_step=qd // tn)
    nc = s // CMP_STRIDE
    xc = xc.reshape(2, NSA_GROUPS, nc, CMP_STRIDE * DH)
    kvc = _compress(xc, cmp_w1.astype(BF16), cmp_b1.reshape(2, 1, DH), cmp_w2.astype(BF16),
                    cmp_pos.reshape(2, 1, CMP_LEN * DH).astype(BF16), k_g[0].reshape(1, DH))

    n_sel = s // SEL_BLOCK
    nselp = -(-n_sel // LANE) * LANE
    cstart = jnp.arange(nc) * CMP_STRIDE
    sstart = jnp.arange(nselp) * SEL_BLOCK
    overlap_t = ((cstart[None, :] < sstart[:, None] + SEL_BLOCK)
                 & (cstart[None, :] + CMP_LEN > sstart[:, None])
                 & (jnp.arange(nselp)[:, None] < n_sel)
                 & (jnp.arange(nc)[None, :] < nc - 1)).astype(BF16)
    o_cmp, notsel = _cmp_attention(proj, kvc[0], kvc[1], overlap_t, tq=tq)

    blk = jnp.arange(min(s, SEL_BLOCK * LANE)) // SEL_BLOCK
    expand =jnp.where(blk[:, None] == jnp.arange(LANE)[None, :], MASK_BIAS, 0.0).astype(BF16)
    col0 = qd // DH
    o_sel = _sel_attention(proj, notsel, proj, expand, k_col=col0 + 2 * NSA_GROUPS,
                           v_col=col0 + 3 * NSA_GROUPS, tq=min(2 * tq, s), tk=tk)
    mixed = _win_attention(proj, proj, o_cmp, o_sel, gates, k_blk=(qd + 4 * kvd) // kvd,
                           v_blk=(qd + 5 * kvd) // kvd, tq=min(2 * tq, s))
    return _matmul_res(mixed, w_out, layer, x, tm=tm, name="nsa_out_proj")


def _mlstm_layer(x, norm_g, w_in, b_if, out_g, w_out, layer, *, tm):
    s, d = x.shape
    norm_g = norm_g.reshape(1, d)
    wq = ML_HEADS * ML_DK
    wv = ML_HEADS * ML_DV
    nmain = 2 * wq + 2 * wv
    scale =jnp.concatenate([jnp.ones((wq,), F32), jnp.full((wq,), ML_DK ** -0.5, F32),
                             jnp.ones((2 * wv,), F32)]).reshape(1, -1)
    w_gate = _pad_cols(w_in[layer, :, nmain:], LANE)
    bias = _pad_cols(b_if.reshape(1, -1), LANE)
    qkvo, gates = _proj(x, norm_g, w_in, layer, scale, scale, w_gate, bias, mode="scale", gate_mode="bias",
                        tm=min(2 * tm, s), tn=1024, name="ml_proj")
    y = _mlstm(qkvo, gates, out_g.reshape(1, -1))
    return _matmul_res(y, w_out, layer, x, tm=tm, name="ml_out_proj")


def _ffn_layer(x, norm_g, wg, wu, wd, layer, *, tm):
    return _ffn(x, norm_g.reshape(1, -1), wg, wu, wd, layer, tm=tm, tf=512)


def kernel(x, norm_mix_g, norm_ffn_g, nsa_w_in, nsa_b_gate, nsa_q_norm_g, nsa_k_norm_g, nsa_cmp_pos,
           nsa_cmp_w1, nsa_cmp_b1, nsa_cmp_w2, nsa_w_out, ml_w_in, ml_b_if, ml_out_norm_g, ml_w_out,
           ffn_w_gate, ffn_w_up, ffn_w_down):
    b, s, d = x.shape
    depth = norm_mix_g.shape[0]
    tm = min(512, s)
    tq = 128
    tk = min(1024, s)
    nsa_w_in, ml_w_in = nsa_w_in.astype(BF16), ml_w_in.astype(BF16)
    nsa_w_out, ml_w_out = nsa_w_out.astype(BF16), ml_w_out.astype(BF16)
    ffn_w = (ffn_w_gate.astype(BF16), ffn_w_up.astype(BF16), ffn_w_down.astype(BF16))
    outs = []
    for bi in range(b):
        xb = x[bi]
        for i in range(depth):
            j = i // 2
            if i % 2 == 0:
                xb = _nsa_layer(xb, norm_mix_g[i], nsa_w_in, nsa_b_gate[j], nsa_q_norm_g[j],
                                nsa_k_norm_g[j], nsa_cmp_pos[j], nsa_cmp_w1[j], nsa_cmp_b1[j],
                                nsa_cmp_w2[j], nsa_w_out, j, tm=tm, tq=tq, tk=tk)
            else:
                xb = _mlstm_layer(xb, norm_mix_g[i], ml_w_in, ml_b_if[j], ml_out_norm_g[j],
                                  ml_w_out, j, tm=tm)
            xb = _ffn_layer(xb, norm_ffn_g[i], *ffn_w, i, tm=min(2 * tm, s))
        outs.append(xb)
    return jnp.stack(outs, axis=0)
```

```python
import functools
from typing import NamedTuple

import jax
import jax.numpy as jnp
from jax import lax
from jax.experimental import pallas as pl
from jax.experimental.pallas import tpu as pltpu

F32 = jnp.float32
BF16 = jnp.bfloat16

EPS = 1e-6
NEG_INIT = -1e30
LOG2E = 1.4426950408889634

LANE = 128
SUBLANE = 8
VMEM_LIMIT = 56 * 1024 * 1024
PROJ_CHUNK = 512

NSA_HEADS = 16
NSA_GROUPS = 4
NSA_REP = NSA_HEADS // NSA_GROUPS
DH = 128
CMP_LEN = 32
CMP_STRIDE = 16
SEL_BLOCK = 64
N_SELECT = 16
WINDOW = 512
ML_HEADS = 8
ML_DK = 128
ML_DV = 256
ML_CHUNK = 512

MASK_BIAS = -(2.0 ** 100)


def _params(*sem):
    return pltpu.CompilerParams(dimension_semantics=sem, vmem_limit_bytes=VMEM_LIMIT)


def _rms(x, g):
    ms = jnp.mean(x * x, axis=-1, keepdims=True)
    return x * lax.rsqrt(ms + EPS) * g


def _proj_kernel(x_ref, g_ref, w_ref, a_ref, b_ref, wg_ref, bg_ref, o_ref, og_ref, *rest,
                 mode, gate_mode, chunk_step):
    if chunk_step is None:
        (h_scr,) = rest
    else:
        oc_ref, h_scr, y_scr = rest

    @pl.when(pl.program_id(1) == 0)
    def _():
        h = _rms(x_ref[...], g_ref[...]).astype(BF16)
        h_scr[...] = h
        gl = jnp.dot(h, wg_ref[...], preferred_element_type=F32) + bg_ref[...]
        og_ref[...] = jax.nn.sigmoid(gl) if gate_mode == "sigmoid" else gl

    h = h_scr[...]
    for c in range(o_ref.shape[1] // PROJ_CHUNK):
        cs = slice(c * PROJ_CHUNK, (c + 1) * PROJ_CHUNK)
        y = jnp.dot(h, w_ref[:, cs], preferred_element_type=F32)
        if mode == "headnorm":
            for u in range(PROJ_CHUNK // LANE):
                sl = slice(c * PROJ_CHUNK + u * LANE, c * PROJ_CHUNK + (u + 1) * LANE)
                yc = y[:, u * LANE:(u + 1) * LANE]
                ms = jnp.mean(yc * yc, axis=-1, keepdims=True)
                mult = jnp.where(b_ref[:, sl] > 0.0, lax.rsqrt(ms + EPS), 1.0) * a_ref[:, sl]
                o_ref[:, sl] = (yc * mult).astype(o_ref.dtype)
        elif mode == "scale":
            o_ref[:, cs] = (y * a_ref[:, cs]).astype(o_ref.dtype)
        else:
            raise ValueError(mode)

    if chunk_step is not None:
        @pl.when(pl.program_id(1) == chunk_step)
        def _():
            nchunk = y_scr.shape[0] // CMP_STRIDE
            for blk in range(o_ref.shape[1] // LANE):
                y_scr[...] = o_ref[:, blk * LANE:(blk + 1) * LANE].astype(F32)
                for l in range(CMP_STRIDE):
                    rows = y_scr[pl.ds(l, nchunk, stride=CMP_STRIDE), :]
                    oc_ref[blk, :, l * LANE:(l + 1) * LANE] = rows.astype(oc_ref.dtype)


def _proj(x, g, w, layer, a, b, wg, bg, *, mode, gate_mode, tm, tn, name, chunk_step=None):
    s, d = x.shape
    n = a.shape[1]
    ng = wg.shape[1]
    assert s % tm == 0 and n % tn == 0 and tn % PROJ_CHUNK == 0 and n <= w.shape[2]
    out_specs = [
        pl.BlockSpec((tm, tn), lambda i, j: (i, j)),
        pl.BlockSpec((tm, ng), lambda i, j: (i, 0)),
    ]
    out_shape = [
        jax.ShapeDtypeStruct((s, n), BF16),
        jax.ShapeDtypeStruct((s, ng), F32),
    ]
    scratch = [pltpu.VMEM((tm, d), BF16)]
    if chunk_step is not None:
        assert tm % (SUBLANE * CMP_STRIDE) == 0
        nblk = tn // LANE
        out_specs.append(pl.BlockSpec((nblk, tm // CMP_STRIDE, CMP_STRIDE * LANE), lambda i, j: (0, i, 0)))
        out_shape.append(jax.ShapeDtypeStruct((nblk, s // CMP_STRIDE, CMP_STRIDE * LANE), BF16))
        scratch.append(pltpu.VMEM((tm, LANE), F32))
    return pl.pallas_call(
        functools.partial(_proj_kernel, mode=mode, gate_mode=gate_mode, chunk_step=chunk_step),
        grid=(s // tm, n // tn),
        in_specs=[
            pl.BlockSpec((tm, d), lambda i, j: (i, 0)),
            pl.BlockSpec((1, d), lambda i, j: (0, 0)),
            pl.BlockSpec((None, d, tn), lambda i, j: (layer, 0, j)),
            pl.BlockSpec((1, tn), lambda i, j: (0, j)),
            pl.BlockSpec((1, tn), lambda i, j: (0, j)),
            pl.BlockSpec((d, ng), lambda i, j: (0, 0)),
            pl.BlockSpec((1, ng), lambda i, j: (0, 0)),
        ],
        out_specs=out_specs,
        out_shape=out_shape,
        scratch_shapes=scratch,
        compiler_params=_params("parallel", "arbitrary"),
        name=name,
    )(x, g, w, a, b, wg, bg)


def _matmul_res_kernel(a_ref, w_ref, r_ref, o_ref):
    a = a_ref[...]
    for c in range(o_ref.shape[1] // PROJ_CHUNK):
        cs = slice(c * PROJ_CHUNK, (c + 1) * PROJ_CHUNK)
        o_ref[:, cs] = r_ref[:, cs] + jnp.dot(a, w_ref[:, cs], preferred_element_type=F32)


def _matmul_res(a, w, layer, res, *, tm, name):
    s, k = a.shape
    n = w.shape[2]
    assert s % tm == 0 and n % PROJ_CHUNK == 0
    return pl.pallas_call(
        _matmul_res_kernel,
        grid=(s // tm,),
        in_specs=[
            pl.BlockSpec((tm, k), lambda i: (i, 0)),
            pl.BlockSpec((None, k, n), lambda i: (layer, 0, 0), pipeline_mode=pl.Buffered(1)),
            pl.BlockSpec((tm, n), lambda i: (i, 0)),
        ],
        out_specs=pl.BlockSpec((tm, n), lambda i: (i, 0)),
        out_shape=jax.ShapeDtypeStruct((s, n), F32),
        compiler_params=_params("parallel"),
        name=name,
    )(a, w, res)


def _ffn_kernel(x_ref, g_ref, wg_ref, wu_ref, wd_ref, o_ref, h_scr):
    @pl.when(pl.program_id(1) == 0)
    def _():
        x = x_ref[...]
        h_scr[...] = _rms(x, g_ref[...]).astype(BF16)
        o_ref[...] = x

    half = h_scr.shape[0] // 2
    for u in range(2):
        rs = slice(u * half, (u + 1) * half)
        h = h_scr[rs, :]
        gate = jnp.dot(h, wg_ref[...], preferred_element_type=F32)
        up = jnp.dot(h, wu_ref[...], preferred_element_type=F32)
        act = (gate * jax.nn.sigmoid(gate) * up).astype(BF16)
        o_ref[rs, :] += jnp.dot(act, wd_ref[...], preferred_element_type=F32)


def _ffn(x, g, wg, wu, wd, layer, *, tm, tf):
    s, d = x.shape
    dff = wg.shape[2]
    assert s % tm == 0 and dff % tf == 0
    return pl.pallas_call(
        _ffn_kernel,
        grid=(s // tm, dff // tf),
        in_specs=[
            pl.BlockSpec((tm, d), lambda i, f: (i, 0)),
            pl.BlockSpec((1, d), lambda i, f: (0, 0)),
            pl.BlockSpec((None, d, tf), lambda i, f: (layer, 0, f)),
            pl.BlockSpec((None, d, tf), lambda i, f: (layer, 0, f)),
            pl.BlockSpec((None, tf, d), lambda i, f: (layer, f, 0)),
        ],
        out_specs=pl.BlockSpec((tm, d), lambda i, f: (i, 0)),
        out_shape=jax.ShapeDtypeStruct((s, d), F32),
        scratch_shapes=[pltpu.VMEM((tm, d), BF16)],
        compiler_params=_params("parallel", "arbitrary"),
        name="ffn",
    )(x, g, wg, wu, wd)


def _compress_kernel(x_ref, w1_ref, b1_ref, w2_ref, pos_ref, kg_ref, o_ref):
    half = CMP_STRIDE * DH
    x = x_ref[0, 0]
    nc = x.shape[0]
    top = jnp.dot(x, w1_ref[0, :half, :], preferred_element_type=F32)
    bot = jnp.dot(x, w1_ref[0, half:, :], preferred_element_type=F32)
    bot = pltpu.roll(bot, nc - 1, 0)
    row = lax.broadcasted_iota(jnp.int32, bot.shape, 0)
    bot = jnp.where(row == nc - 1, 0.0, bot)
    pos8 = jnp.broadcast_to(pos_ref[0], (8, 2 * half))
    posb = jnp.dot(pos8, w1_ref[0], preferred_element_type=F32)[0:1]
    hdn = jax.nn.gelu(top + bot + posb + b1_ref[0])
    y = jnp.dot(hdn.astype(BF16), w2_ref[0], preferred_element_type=F32)
    yn = _rms(y, kg_ref[...])
    o_ref[0, 0] = jnp.where(pl.program_id(0) == 0, yn, y).astype(o_ref.dtype)


def _compress(xc, w1, b1, w2, pos, kg):
    _, g, nc, k = xc.shape
    return pl.pallas_call(
        _compress_kernel,
        grid=(2, g),
        in_specs=[
            pl.BlockSpec((1, 1, nc, k), lambda s, i: (s, i, 0, 0)),
            pl.BlockSpec((1, 2 * k, DH), lambda s, i: (s, 0, 0)),
            pl.BlockSpec((1, 1, DH), lambda s, i: (s, 0, 0)),
            pl.BlockSpec((1, DH, DH), lambda s, i: (s, 0, 0)),
            pl.BlockSpec((1, 1, 2 * k), lambda s, i: (s, 0, 0)),
            pl.BlockSpec((1, DH), lambda s, i: (0, 0)),
        ],
        out_specs=pl.BlockSpec((1, 1, nc, DH), lambda s, i: (s, i, 0, 0)),
        out_shape=jax.ShapeDtypeStruct((2, g, nc, DH), BF16),
        compiler_params=_params("parallel", "parallel"),
        name="nsa_compress",
    )(xc, w1, b1, w2, pos, kg)


def _stack_heads(qb):
    return jnp.concatenate([qb[:, r * DH:(r + 1) * DH] for r in range(NSA_REP)], axis=0)


def _cmp_kernel(q_ref, kc_ref, vc_ref, ovt_ref, o_ref, mn_ref, imp_scr, *, tq, ktop, col_steps):
    qi = pl.program_id(0)
    gw = NSA_REP * DH

    def attend(ncols, nb):
        for g in range(NSA_GROUPS):
            q4 = _stack_heads(q_ref[:, g * gw:(g + 1) * gw])
            s = lax.dot_general(q4, kc_ref[g, 0:ncols, :], (((1,), (1,)), ((), ())),
                                preferred_element_type=F32)
            row = lax.broadcasted_iota(jnp.int32, s.shape, 0)
            col = lax.broadcasted_iota(jnp.int32, s.shape, 1)
            t = qi * tq + (row & (tq - 1))
            s = jnp.where(col * CMP_STRIDE + (CMP_LEN - 1) <= t, s, -jnp.inf)
            mx = jnp.max(s, axis=-1, keepdims=True)
            mx = jnp.where(jnp.abs(mx) < jnp.inf, mx, 0.0)
            p = jnp.exp2(s - mx)
            p = p / jnp.maximum(jnp.sum(p, axis=-1, keepdims=True), 1e-30)
            o = jnp.dot(p.astype(BF16), vc_ref[g, 0:ncols, :], preferred_element_type=F32)
            for r in range(NSA_REP):
                o_ref[:, g * gw + r * DH:g * gw + (r + 1) * DH] = o[r * tq:(r + 1) * tq].astype(o_ref.dtype)
            ps = p[0:tq]
            for r in range(1, NSA_REP):
                ps = ps + p[r * tq:(r + 1) * tq]
            imp_scr[g, 0:nb, :] = lax.dot_general(ovt_ref[0:nb, 0:ncols], ps.astype(BF16),
                                                  (((1,), (1,)), ((), ())), preferred_element_type=F32)

    def select(nb):
        nselp = imp_scr.shape[1]
        jj = lax.broadcasted_iota(jnp.int32, (nb, tq), 0)
        cur = (qi * tq + lax.broadcasted_iota(jnp.int32, (nb, tq), 1)) // SEL_BLOCK
        forced = (jj == 0) | (jj == cur) | (jj == cur - 1)
        free = (jj >= 1) & (jj <= cur - 2)
        jjf = jj.astype(F32)
        scores = [jnp.where(free, imp_scr[g, 0:nb, :], -jnp.inf) for g in range(NSA_GROUPS)]
        for _ in range(ktop - 3):
            for g in range(NSA_GROUPS):
                top = jnp.max(scores[g], axis=0, keepdims=True)
                first = jnp.min(jnp.where(scores[g] == top, jjf, 1e9), axis=0, keepdims=True)
                scores[g] = jnp.where(jjf == first, -jnp.inf, scores[g])
        for g in range(NSA_GROUPS):
            picked = forced | (free & (scores[g] == -jnp.inf))
            notsel = jnp.where(picked, 0.0, 1.0)
            if nb < nselp:
                notsel = jnp.concatenate([notsel, jnp.ones((nselp - nb, tq), F32)], axis=0)
            mn_ref[g] = notsel.T.astype(mn_ref.dtype)

    def variant(ncols):
        nb = min(imp_scr.shape[1], -(-(ncols * CMP_STRIDE // SEL_BLOCK) // SUBLANE) * SUBLANE)
        attend(ncols, nb)
        select(nb)

    needed = ((qi + 1) * tq - CMP_LEN) // CMP_STRIDE + 1
    lo = 0
    for ncols in col_steps:
        pl.when((needed > lo) & (needed <= ncols))(functools.partial(variant, ncols))
        lo = ncols


def _cmp_attention(q, kc, vc, overlap_t, *, tq):
    s = q.shape[0]
    g, nc, _ = kc.shape
    nselp = overlap_t.shape[0]
    ktop = min(N_SELECT, s // SEL_BLOCK)
    assert ktop >= 3
    gw = NSA_REP * DH
    col_steps = tuple(range(2 * LANE, nc + 1, 2 * LANE)) if nc % (2 * LANE) == 0 else (nc,)
    return pl.pallas_call(
        functools.partial(_cmp_kernel, tq=tq, ktop=ktop, col_steps=col_steps),
        grid=(s // tq,),
        in_specs=[
            pl.BlockSpec((tq, g * gw), lambda i: (i, 0)),
            pl.BlockSpec((g, nc, DH), lambda i: (0, 0, 0)),
            pl.BlockSpec((g, nc, DH), lambda i: (0, 0, 0)),
            pl.BlockSpec((nselp, nc), lambda i: (0, 0)),
        ],
        out_specs=[
            pl.BlockSpec((tq, g * gw), lambda i: (i, 0)),
            pl.BlockSpec((g, tq, nselp), lambda i: (0, i, 0)),
        ],
        out_shape=[
            jax.ShapeDtypeStruct((s, g * gw), BF16),
            jax.ShapeDtypeStruct((g, s, nselp), BF16),
        ],
        scratch_shapes=[pltpu.VMEM((g, nselp, tq), F32)],
        compiler_params=_params("parallel"),
        name="nsa_cmp_select",
    )(q, kc, vc, overlap_t)


def _sel_kernel(q_ref, mn_ref, k_ref, v_ref, e_ref, o_ref, ka_scr, va_scr, qa_scr, s_scr, m_scr, acc_scr,
                *, tq, tk):
    qi = pl.program_id(1)
    nhalf = qa_scr.shape[0]
    nper = e_ref.shape[0]

    @pl.when(qi == 0)
    def _():
        for c in range(ka_scr.shape[0] // nper):
            rows = slice(c * nper, (c + 1) * nper)
            ka_scr[rows, 0:DH] = k_ref[rows, :]
            ka_scr[rows, DH:2 * DH] = e_ref[...]
            va_scr[rows, 0:DH] = v_ref[rows, :]
            va_scr[rows, DH:2 * DH] = jnp.ones((nper, DH), BF16)

    q4 = _stack_heads(q_ref[...])
    mn = mn_ref[0]
    for hf in range(nhalf):
        part = mn[:, hf * LANE:(hf + 1) * LANE]
        qa_scr[hf, :, 0:DH] = q4
        qa_scr[hf, :, DH:2 * DH] = jnp.concatenate([part] * NSA_REP, axis=0)
    m_scr[...] = jnp.full_like(m_scr, -jnp.inf)
    acc_scr[...] = jnp.zeros_like(acc_scr)

    def scores(j, slot):
        k0 = pl.multiple_of(j * tk, tk)
        s_scr[slot] = lax.dot_general(qa_scr[k0 // (SEL_BLOCK * LANE)], ka_scr[pl.ds(k0, tk), :],
                                      (((1,), (1,)), ((), ())), preferred_element_type=F32)

    def accumulate(j, slot, masked):
        k0 = pl.multiple_of(j * tk, tk)
        s = s_scr[slot]
        if masked:
            row = lax.broadcasted_iota(jnp.int32, s.shape, 0)
            col = lax.broadcasted_iota(jnp.int32, s.shape, 1)
            s = jnp.where(k0 + col <= qi * tq + (row & (tq - 1)), s, -jnp.inf)
        m_old = m_scr[...]
        m_new = jnp.maximum(m_old, jnp.max(s, axis=-1, keepdims=True))
        p = jnp.exp2(s - m_new).astype(BF16)
        acc_scr[...] = (jnp.exp2(m_old - m_new) * acc_scr[...]
                        + jnp.dot(p, va_scr[pl.ds(k0, tk), :], preferred_element_type=F32))
        m_scr[...] = m_new

    n = (qi * tq + tq - 1) // tk + 1
    unroll = 4
    ntrip = (n - 1) // unroll
    scores(0, 0)

    def trip(i, c):
        for u in range(unroll):
            scores(unroll * i + u + 1, (u + 1) % 2)
            accumulate(unroll * i + u, u % 2, False)
        return c

    lax.fori_loop(0, ntrip, trip, 0)

    first = ntrip * unroll
    for rest in range(1, unroll + 1):
        @pl.when(n - first == rest)
        def _(rest=rest):
            for u in range(rest):
                if u + 1 < rest:
                    scores(first + u + 1, (u + 1) % 2)
                accumulate(first + u, u % 2, u + 1 == rest)

    acc = acc_scr[...]
    o = acc[:, 0:DH] / jnp.maximum(acc[:, DH:2 * DH], 1e-30)
    for r in range(NSA_REP):
        o_ref[:, r * DH:(r + 1) * DH] = o[r * tq:(r + 1) * tq].astype(o_ref.dtype)


def _sel_attention(q, notsel, kv_arr, expand, *, k_col, v_col, tq, tk):
    s = q.shape[0]
    g = notsel.shape[0]
    nselp = notsel.shape[2]
    gw = NSA_REP * DH
    nper = expand.shape[0]
    assert s % nper == 0 and nper % tk == 0 and tk % tq == 0 and s % tq == 0
    once = pl.Buffered(1)
    return pl.pallas_call(
        functools.partial(_sel_kernel, tq=tq, tk=tk),
        grid=(g, s // tq),
        in_specs=[
            pl.BlockSpec((tq, gw), lambda gi, i: (i, gi)),
            pl.BlockSpec((1, tq, nselp), lambda gi, i: (gi, i, 0)),
            pl.BlockSpec((s, DH), lambda gi, i: (0, k_col + gi), pipeline_mode=once),
            pl.BlockSpec((s, DH), lambda gi, i: (0, v_col + gi), pipeline_mode=once),
            pl.BlockSpec((nper, LANE), lambda gi, i: (0, 0), pipeline_mode=once),
        ],
        out_specs=pl.BlockSpec((tq, gw), lambda gi, i: (i, gi)),
        out_shape=jax.ShapeDtypeStruct((s, g * gw), BF16),
        scratch_shapes=[
            pltpu.VMEM((s, 2 * DH), BF16),
            pltpu.VMEM((s, 2 * DH), BF16),
            pltpu.VMEM((nselp // LANE, NSA_REP * tq, 2 * DH), BF16),
            pltpu.VMEM((2, NSA_REP * tq, tk), F32),
            pltpu.VMEM((NSA_REP * tq, 1), F32),
            pltpu.VMEM((NSA_REP * tq, 2 * DH), F32),
        ],
        compiler_params=_params("arbitrary", "arbitrary"),
        name="nsa_sel_attention",
    )(q, notsel, kv_arr, kv_arr, expand)


def _win_kernel(q_ref, *refs, tq, nback):
    nblk = nback + 1
    k_refs = refs[:nblk]
    v_refs = refs[nblk:2 * nblk]
    band_ref, oc_ref, os_ref, gt_ref, o_ref = refs[2 * nblk:]
    qi = pl.program_id(0)
    gw = NSA_REP * DH
    band = band_ref[...]
    col = lax.broadcasted_iota(jnp.int32, band.shape, 1)
    band = jnp.where(col >= (nback - qi) * tq, band, -jnp.inf)
    for g in range(NSA_GROUPS):
        q4 = _stack_heads(q_ref[:, g * gw:(g + 1) * gw])
        kc = jnp.concatenate([r[:, g * DH:(g + 1) * DH] for r in k_refs], axis=0)
        vc = jnp.concatenate([r[:, g * DH:(g + 1) * DH] for r in v_refs], axis=0)
        s = lax.dot_general(q4, kc, (((1,), (1,)), ((), ())), preferred_element_type=F32) + band
        mx = jnp.max(s, axis=-1, keepdims=True)
        mx = jnp.where(jnp.abs(mx) < jnp.inf, mx, 0.0)
        p = jnp.exp2(s - mx)
        p = p / jnp.maximum(jnp.sum(p, axis=-1, keepdims=True), 1e-30)
        ow = jnp.dot(p.astype(BF16), vc, preferred_element_type=F32)
        gates = gt_ref[...]
        for r in range(NSA_REP):
            sl = slice(g * gw + r * DH, g * gw + (r + 1) * DH)
            c0 = 3 * (g * NSA_REP + r)
            out = (gates[:, c0:c0 + 1] * oc_ref[:, sl].astype(F32)
                   + gates[:, c0 + 1:c0 + 2] * os_ref[:, sl].astype(F32)
                   + gates[:, c0 + 2:c0 + 3] * ow[r * tq:(r + 1) * tq])
            o_ref[:, sl] = out.astype(o_ref.dtype)


def _win_attention(q, kv_arr, o_cmp, o_sel, gates, *, k_blk, v_blk, tq):
    s = q.shape[0]
    g = NSA_GROUPS
    gw = NSA_REP * DH
    nback = WINDOW // tq
    assert nback * tq == WINDOW
    qq = jnp.arange(NSA_REP * tq)[:, None] % tq
    kk = jnp.arange((nback + 1) * tq)[None, :]
    band = jnp.where((kk > qq) & (kk <= qq + WINDOW), 0.0, -jnp.inf).astype(F32)

    def kvmap(blk, b):
        def f(i):
            return (jnp.maximum(i - nback + b, 0), blk)
        return f

    k_specs = [pl.BlockSpec((tq, g * DH), kvmap(k_blk, b)) for b in range(nback + 1)]
    v_specs = [pl.BlockSpec((tq, g * DH), kvmap(v_blk, b)) for b in range(nback + 1)]
    wide = pl.BlockSpec((tq, g * gw), lambda i: (i, 0))
    return pl.pallas_call(
        functools.partial(_win_kernel, tq=tq, nback=nback),
        grid=(s // tq,),
        in_specs=([wide] + k_specs + v_specs
                  + [pl.BlockSpec(band.shape, lambda i: (0, 0)), wide, wide,
                     pl.BlockSpec((tq, LANE), lambda i: (i, 0))]),
        out_specs=wide,
        out_shape=jax.ShapeDtypeStruct((s, g * gw), BF16),
        compiler_params=_params("parallel"),
        name="nsa_win_combine",
    )(q, *([kv_arr] * (2 * (nback + 1))), band, o_cmp, o_sel, gates)


def _mlstm_kernel(q_ref, k_ref, v_ref, o_ref, gt_ref, og_ref, y_ref, c_scr, n_scr, m_scr):
    L = q_ref.shape[0]

    @pl.when(pl.program_id(0) == 0)
    def _():
        c_scr[...] = jnp.zeros_like(c_scr)
        n_scr[...] = jnp.zeros_like(n_scr)
        m_scr[...] = jnp.full_like(m_scr, NEG_INIT)

    ri = lax.broadcasted_iota(jnp.int32, (L, L), 0)
    ci = lax.broadcasted_iota(jnp.int32, (L, L), 1)
    eye = ri == ci
    tril = ci <= ri
    triu = ri <= ci
    gates = gt_ref[...]

    def to_row(col):
        return jnp.sum(jnp.where(eye, col, 0.0), axis=0, keepdims=True)

    for h in range(ML_HEADS):
        qh = q_ref[:, h * ML_DK:(h + 1) * ML_DK]
        kh = k_ref[:, h * ML_DK:(h + 1) * ML_DK]
        vh = v_ref[:, h * ML_DV:(h + 1) * ML_DV]
        ig_col = gates[:, h:h + 1]
        fg_col = gates[:, ML_HEADS + h:ML_HEADS + h + 1]
        lf_col = jnp.minimum(fg_col, 0.0) - jnp.log(1.0 + jnp.exp(-jnp.abs(fg_col)))
        lf_row = to_row(lf_col)
        ig_row = to_row(ig_col)
        b_col = jnp.sum(jnp.where(tril, lf_row, 0.0), axis=1, keepdims=True)
        b_row = jnp.sum(jnp.where(triu, lf_col, 0.0), axis=0, keepdims=True)
        m_old = m_scr[h:h + 1, 0:1]
        dmat = jnp.where(tril, b_col - b_row + ig_row, -jnp.inf)
        m_inter = b_col + m_old
        m_t = jnp.maximum(m_inter, jnp.max(dmat, axis=1, keepdims=True))
        qk = lax.dot_general(qh, kh, (((1,), (1,)), ((), ())), preferred_element_type=F32)
        a = jnp.exp(dmat - m_t) * qk
        dec = jnp.exp(m_inter - m_t)
        c_old = c_scr[h]
        n_old = n_scr[h:h + 1, :]
        num = (jnp.dot(a.astype(BF16), vh, preferred_element_type=F32)
               + dec * jnp.dot(qh, c_old.astype(BF16), preferred_element_type=F32))
        qn = jnp.sum(qh.astype(F32) * n_old, axis=1, keepdims=True)
        den = jnp.sum(a, axis=1, keepdims=True) + dec * qn
        hx = num / jnp.maximum(jnp.abs(den), jnp.exp(-m_t))

        b_last = b_col[L - 1:L, :]
        g_col = b_last - b_col + ig_col
        m_new = jnp.maximum(b_last + m_old, jnp.max(g_col, axis=0, keepdims=True))
        w_col = jnp.exp(g_col - m_new)
        cd = jnp.exp(b_last + m_old - m_new)
        kw = kh.astype(F32) * w_col
        c_scr[h] = cd * c_old + lax.dot_general(kw.astype(BF16), vh, (((0,), (0,)), ((), ())),
                                                preferred_element_type=F32)
        n_scr[h:h + 1, :] = cd * n_old + jnp.sum(kw, axis=0, keepdims=True)
        m_scr[h:h + 1, :] = jnp.broadcast_to(m_new, (1, LANE))

        sl = slice(h * ML_DV, (h + 1) * ML_DV)
        hn = _rms(hx, og_ref[:, sl])
        y_ref[:, sl] = (jax.nn.sigmoid(o_ref[:, sl].astype(F32)) * hn).astype(y_ref.dtype)


def _mlstm(qkvo, gates, out_g):
    s = qkvo.shape[0]
    L = min(ML_CHUNK, s)
    assert s % L == 0
    wq = ML_HEADS * ML_DK
    wv = ML_HEADS * ML_DV
    return pl.pallas_call(
        _mlstm_kernel,
        grid=(s // L,),
        in_specs=[
            pl.BlockSpec((L, wq), lambda c: (c, 0)),
            pl.BlockSpec((L, wq), lambda c: (c, 1)),
            pl.BlockSpec((L, wv), lambda c: (c, 1)),
            pl.BlockSpec((L, wv), lambda c: (c, 2)),
            pl.BlockSpec((L, LANE), lambda c: (c, 0)),
            pl.BlockSpec((1, wv), lambda c: (0, 0)),
        ],
        out_specs=pl.BlockSpec((L, wv), lambda c: (c, 0)),
        out_shape=jax.ShapeDtypeStruct((s, wv), BF16),
        scratch_shapes=[
            pltpu.VMEM((ML_HEADS, ML_DK, ML_DV), F32),
            pltpu.VMEM((ML_HEADS, ML_DK), F32),
            pltpu.VMEM((ML_HEADS, LANE), F32),
        ],
        compiler_params=_params("arbitrary"),
        name="mlstm_scan",
    )(qkvo, qkvo, qkvo, qkvo, gates, out_g)


class _Tiles(NamedTuple):
    proj: int
    out: int
    ffn: int
    ffn_cols: int
    proj_cols: int
    cmp: int
    sel: int
    sel_keys: int
    win: int


def _tiles(s):
    return _Tiles(proj=min(1024, s), out=min(512, s), ffn=min(1024, s), ffn_cols=512, proj_cols=1024,
                  cmp=min(128, s), sel=min(256, s), sel_keys=min(1024, s), win=min(256, s))


def _pad_cols(a, n):
    return jnp.pad(a, ((0, 0), (0, n - a.shape[1])))


def _nsa_layer(x, norm_g, w_main, w_gate, b_gate, q_g, k_g, cmp_pos, cmp_w1, cmp_b1, cmp_w2, w_out, layer, t):
    s, d = x.shape
    qd = NSA_HEADS * DH
    kvd = NSA_GROUPS * DH
    norm_g = norm_g.reshape(1, d)

    ones = jnp.ones((kvd,), F32)
    gain = jnp.concatenate([jnp.tile(q_g, NSA_HEADS) * (DH ** -0.5 * LOG2E), ones, ones,
                            jnp.tile(k_g[1], NSA_GROUPS), ones, jnp.tile(k_g[2], NSA_GROUPS), ones])
    flag = jnp.concatenate([jnp.ones((qd,), F32), 0 * ones, 0 * ones, ones, 0 * ones, ones, 0 * ones])
    w_gate = _pad_cols(w_gate, LANE).astype(BF16)
    bias = _pad_cols(b_gate.reshape(1, -1), LANE)
    tn = 2 * kvd
    assert qd % tn == 0
    proj, gates, xc = _proj(x, norm_g, w_main, layer, gain.reshape(1, -1), flag.reshape(1, -1), w_gate, bias,
                            mode="headnorm", gate_mode="sigmoid", tm=t.proj, tn=tn, name="nsa_proj",
                            chunk_step=qd // tn)
    nc = s // CMP_STRIDE
    xc = xc.reshape(2, NSA_GROUPS, nc, CMP_STRIDE * DH)
    kvc = _compress(xc, cmp_w1.astype(BF16), cmp_b1.reshape(2, 1, DH), cmp_w2.astype(BF16),
                    cmp_pos.reshape(2, 1, CMP_LEN * DH).astype(BF16), k_g[0].reshape(1, DH))

    n_sel = s // SEL_BLOCK
    nselp = -(-n_sel // LANE) * LANE
    cstart = jnp.arange(nc) * CMP_STRIDE
    sstart = jnp.arange(nselp) * SEL_BLOCK
    overlap_t = ((cstart[None, :] < sstart[:, None] + SEL_BLOCK)
                 & (cstart[None, :] + CMP_LEN > sstart[:, None])
                 & (jnp.arange(nselp)[:, None] < n_sel)
                 & (jnp.arange(nc)[None, :] < nc - 1)).astype(BF16)
    o_cmp, notsel = _cmp_attention(proj, kvc[0], kvc[1], overlap_t, tq=t.cmp)

    blk = jnp.arange(min(s, SEL_BLOCK * LANE)) // SEL_BLOCK
    expand = jnp.where(blk[:, None] == jnp.arange(LANE)[None, :], MASK_BIAS, 0.0).astype(BF16)
    col0 = qd // DH
    o_sel = _sel_attention(proj, notsel, proj, expand, k_col=col0 + 2 * NSA_GROUPS,
                           v_col=col0 + 3 * NSA_GROUPS, tq=t.sel, tk=t.sel_keys)
    mixed = _win_attention(proj, proj, o_cmp, o_sel, gates, k_blk=(qd + 4 * kvd) // kvd,
                           v_blk=(qd + 5 * kvd) // kvd, tq=t.win)
    return _matmul_res(mixed, w_out, layer, x, tm=t.out, name="nsa_out_proj")


def _mlstm_layer(x, norm_g, w_main, w_gate, b_if, out_g, w_out, layer, t):
    s, d = x.shape
    norm_g = norm_g.reshape(1, d)
    wq = ML_HEADS * ML_DK
    wv = ML_HEADS * ML_DV
    scale = jnp.concatenate([jnp.ones((wq,), F32), jnp.full((wq,), ML_DK ** -0.5, F32),
                             jnp.ones((2 * wv,), F32)]).reshape(1, -1)
    w_gate = _pad_cols(w_gate, LANE).astype(BF16)
    bias = _pad_cols(b_if.reshape(1, -1), LANE)
    qkvo, gates = _proj(x, norm_g, w_main, layer, scale, scale, w_gate, bias, mode="scale", gate_mode="bias",
                        tm=t.proj, tn=t.proj_cols, name="ml_proj")
    y = _mlstm(qkvo, gates, out_g.reshape(1, -1))
    return _matmul_res(y, w_out, layer, x, tm=t.out, name="ml_out_proj")


def _ffn_layer(x, norm_g, wg, wu, wd, layer, t):
    return _ffn(x, norm_g.reshape(1, -1), wg, wu, wd, layer, tm=t.ffn, tf=t.ffn_cols)


def kernel(x, norm_mix_g, norm_ffn_g, nsa_w_in, nsa_b_gate, nsa_q_norm_g, nsa_k_norm_g, nsa_cmp_pos,
           nsa_cmp_w1, nsa_cmp_b1, nsa_cmp_w2, nsa_w_out, ml_w_in, ml_b_if, ml_out_norm_g, ml_w_out,
           ffn_w_gate, ffn_w_up, ffn_w_down):
    b, s, d = x.shape
    depth = norm_mix_g.shape[0]
    t = _tiles(s)
    nsa_main = NSA_HEADS * DH + 6 * NSA_GROUPS * DH
    ml_main = 2 * ML_HEADS * ML_DK + 2 * ML_HEADS * ML_DV
    nsa_w_main, nsa_w_gate = nsa_w_in.astype(BF16), nsa_w_in[:, :, nsa_main:]
    ml_w_main, ml_w_gate = ml_w_in.astype(BF16), ml_w_in[:, :, ml_main:]
    nsa_w_out, ml_w_out = nsa_w_out.astype(BF16), ml_w_out.astype(BF16)
    ffn_w = (ffn_w_gate.astype(BF16), ffn_w_up.astype(BF16), ffn_w_down.astype(BF16))
    outs = []
    for bi in range(b):
        xb = x[bi]
        for i in range(depth):
            j = i // 2
            if i % 2 == 0:
                xb = _nsa_layer(xb, norm_mix_g[i], nsa_w_main, nsa_w_gate[j], nsa_b_gate[j], nsa_q_norm_g[j],
                                nsa_k_norm_g[j], nsa_cmp_pos[j], nsa_cmp_w1[j], nsa_cmp_b1[j],
                                nsa_cmp_w2[j], nsa_w_out, j, t)
            else:
                xb = _mlstm_layer(xb, norm_mix_g[i], ml_w_main, ml_w_gate[j], ml_b_if[j], ml_out_norm_g[j],
                                  ml_w_out, j, t)
            xb = _ffn_layer(xb, norm_ffn_g[i], *ffn_w, i, t)
        outs.append(xb)
    return jnp.stack(outs, axis=0)
```

```python
import functools
from typing import NamedTuple

import jax
import jax.numpy as jnp
from jax import lax
from jax.experimental import pallas as pl
from jax.experimental.pallas import tpu as pltpu

F32 = jnp.float32
BF16 = jnp.bfloat16

EPS = 1e-6
NEG_INIT = -1e30
LOG2E = 1.4426950408889634

LANE = 128
SUBLANE = 8
MIB = 1024 * 1024
VMEM_LIMIT = 56 * MIB
PROJ_CHUNK = 512

NSA_HEADS = 16
NSA_GROUPS = 4
NSA_REP = NSA_HEADS // NSA_GROUPS
DH = 128
CMP_LEN = 32
CMP_STRIDE = 16
SEL_BLOCK = 64
N_SELECT = 16
WINDOW = 512
ML_HEADS = 8
ML_DK = 128
ML_DV = 256
ML_CHUNK = 512

MASK_BIAS = -(2.0 ** 100)


def _params(*sem, vmem=VMEM_LIMIT):
    return pltpu.CompilerParams(dimension_semantics=sem, vmem_limit_bytes=vmem)


def _rms(x, g):
    ms = jnp.mean(x * x, axis=-1, keepdims=True)
    return x * lax.rsqrt(ms + EPS) * g


def _proj_kernel(x_ref, g_ref, w_ref, a_ref, b_ref, wg_ref, bg_ref, o_ref, og_ref, *rest,
                 mode, gate_mode, chunk_step):
    if chunk_step is None:
        (h_scr,) = rest
    else:
        oc_ref, h_scr, y_scr = rest

    @pl.when(pl.program_id(1) == 0)
    def _():
        h = _rms(x_ref[...], g_ref[...]).astype(BF16)
        h_scr[...] = h
        gl = jnp.dot(h, wg_ref[...], preferred_element_type=F32) + bg_ref[...]
        og_ref[...] = jax.nn.sigmoid(gl) if gate_mode == "sigmoid" else gl

    h = h_scr[...]
    for c in range(o_ref.shape[1] // PROJ_CHUNK):
        cs = slice(c * PROJ_CHUNK, (c + 1) * PROJ_CHUNK)
        y = jnp.dot(h, w_ref[:, cs], preferred_element_type=F32)
        if mode == "headnorm":
            for u in range(PROJ_CHUNK // LANE):
                sl = slice(c * PROJ_CHUNK + u * LANE, c * PROJ_CHUNK + (u + 1) * LANE)
                yc = y[:, u * LANE:(u + 1) * LANE]
                ms = jnp.mean(yc * yc, axis=-1, keepdims=True)
                mult = jnp.where(b_ref[:, sl] > 0.0, lax.rsqrt(ms + EPS), 1.0) * a_ref[:, sl]
                o_ref[:, sl] = (yc * mult).astype(o_ref.dtype)
        elif mode == "scale":
            o_ref[:, cs] = (y * a_ref[:, cs]).astype(o_ref.dtype)
        else:
            raise ValueError(mode)

    if chunk_step is not None:
        @pl.when(pl.program_id(1) == chunk_step)
        def _():
            nchunk = y_scr.shape[0] // CMP_STRIDE
            for blk in range(o_ref.shape[1] // LANE):
                y_scr[...] = o_ref[:, blk * LANE:(blk + 1) * LANE].astype(F32)
                for l in range(CMP_STRIDE):
                    rows = y_scr[pl.ds(l, nchunk, stride=CMP_STRIDE), :]
                    oc_ref[blk, :, l * LANE:(l + 1) * LANE] = rows.astype(oc_ref.dtype)


def _proj(x, g, w, layer, a, b, wg, bg, *, mode, gate_mode, tm, tn, name, chunk_step=None):
    s, d = x.shape
    n = a.shape[1]
    ng = wg.shape[1]
    assert s % tm == 0 and n % tn == 0 and tn % PROJ_CHUNK == 0 and n <= w.shape[2]
    out_specs = [
        pl.BlockSpec((tm, tn), lambda i, j: (i, j)),
        pl.BlockSpec((tm, ng), lambda i, j: (i, 0)),
    ]
    out_shape = [
        jax.ShapeDtypeStruct((s, n), BF16),
        jax.ShapeDtypeStruct((s, ng), F32),
    ]
    scratch = [pltpu.VMEM((tm, d), BF16)]
    if chunk_step is not None:
        assert tm % (SUBLANE * CMP_STRIDE) == 0
        nblk = tn // LANE
        out_specs.append(pl.BlockSpec((nblk, tm // CMP_STRIDE, CMP_STRIDE * LANE), lambda i, j: (0, i, 0)))
        out_shape.append(jax.ShapeDtypeStruct((nblk, s // CMP_STRIDE, CMP_STRIDE * LANE), BF16))
        scratch.append(pltpu.VMEM((tm, LANE), F32))
    return pl.pallas_call(
        functools.partial(_proj_kernel, mode=mode, gate_mode=gate_mode, chunk_step=chunk_step),
        grid=(s // tm, n // tn),
        in_specs=[
            pl.BlockSpec((tm, d), lambda i, j: (i, 0)),
            pl.BlockSpec((1, d), lambda i, j: (0, 0)),
            pl.BlockSpec((None, d, tn), lambda i, j: (layer, 0, j)),
            pl.BlockSpec((1, tn), lambda i, j: (0, j)),
            pl.BlockSpec((1, tn), lambda i, j: (0, j)),
            pl.BlockSpec((d, ng), lambda i, j: (0, 0)),
            pl.BlockSpec((1, ng), lambda i, j: (0, 0)),
        ],
        out_specs=out_specs,
        out_shape=out_shape,
        scratch_shapes=scratch,
        compiler_params=_params("parallel", "arbitrary", vmem=48 * MIB),
        name=name,
    )(x, g, w, a, b, wg, bg)


def _matmul_res_kernel(a_ref, w_ref, r_ref, o_ref):
    a = a_ref[...]
    for c in range(o_ref.shape[1] // PROJ_CHUNK):
        cs = slice(c * PROJ_CHUNK, (c + 1) * PROJ_CHUNK)
        o_ref[:, cs] = r_ref[:, cs] + jnp.dot(a, w_ref[:, cs], preferred_element_type=F32)


def _matmul_res(a, w, layer, res, *, tm, name):
    s, k = a.shape
    n = w.shape[2]
    assert s % tm == 0 and n % PROJ_CHUNK == 0
    return pl.pallas_call(
        _matmul_res_kernel,
        grid=(s // tm,),
        in_specs=[
            pl.BlockSpec((tm, k), lambda i: (i, 0)),
            pl.BlockSpec((None, k, n), lambda i: (layer, 0, 0), pipeline_mode=pl.Buffered(1)),
            pl.BlockSpec((tm, n), lambda i: (i, 0)),
        ],
        out_specs=pl.BlockSpec((tm, n), lambda i: (i, 0)),
        out_shape=jax.ShapeDtypeStruct((s, n), F32),
        compiler_params=_params("parallel", vmem=36 * MIB),
        name=name,
    )(a, w, res)


def _ffn_kernel(x_ref, g_ref, wg_ref, wu_ref, wd_ref, o_ref, h_scr):
    @pl.when(pl.program_id(1) == 0)
    def _():
        x = x_ref[...]
        h_scr[...] = _rms(x, g_ref[...]).astype(BF16)
        o_ref[...] = x

    half = h_scr.shape[0] // 2
    for u in range(2):
        rs = slice(u * half, (u + 1) * half)
        h = h_scr[rs, :]
        gate = jnp.dot(h, wg_ref[...], preferred_element_type=F32)
        up = jnp.dot(h, wu_ref[...], preferred_element_type=F32)
        act = (gate * jax.nn.sigmoid(gate) * up).astype(BF16)
        o_ref[rs, :] += jnp.dot(act, wd_ref[...], preferred_element_type=F32)


def _ffn(x, g, wg, wu, wd, layer, *, tm, tf):
    s, d = x.shape
    dff = wg.shape[2]
    assert s % tm == 0 and dff % tf == 0
    return pl.pallas_call(
        _ffn_kernel,
        grid=(s // tm, dff // tf),
        in_specs=[
            pl.BlockSpec((tm, d), lambda i, f: (i, 0)),
            pl.BlockSpec((1, d), lambda i, f: (0, 0)),
            pl.BlockSpec((None, d, tf), lambda i, f: (layer, 0, f)),
            pl.BlockSpec((None, d, tf), lambda i, f: (layer, 0, f)),
            pl.BlockSpec((None, tf, d), lambda i, f: (layer, f, 0)),
        ],
        out_specs=pl.BlockSpec((tm, d), lambda i, f: (i, 0)),
        out_shape=jax.ShapeDtypeStruct((s, d), F32),
        scratch_shapes=[pltpu.VMEM((tm, d), BF16)],
        compiler_params=_params("parallel", "arbitrary"),
        name="ffn",
    )(x, g, wg, wu, wd)


def _compress_kernel(x_ref, w1_ref, b1_ref, w2_ref, pos_ref, kg_ref, o_ref):
    half = CMP_STRIDE * DH
    x = x_ref[0, 0]
    nc = x.shape[0]
    top = jnp.dot(x, w1_ref[0, :half, :], preferred_element_type=F32)
    bot = jnp.dot(x, w1_ref[0, half:, :], preferred_element_type=F32)
    bot = pltpu.roll(bot, nc - 1, 0)
    row = lax.broadcasted_iota(jnp.int32, bot.shape, 0)
    bot = jnp.where(row == nc - 1, 0.0, bot)
    pos8 = jnp.broadcast_to(pos_ref[0], (8, 2 * half))
    posb = jnp.dot(pos8, w1_ref[0], preferred_element_type=F32)[0:1]
    hdn = jax.nn.gelu(top + bot + posb + b1_ref[0])
    y = jnp.dot(hdn.astype(BF16), w2_ref[0], preferred_element_type=F32)
    yn = _rms(y, kg_ref[...])
    o_ref[0, 0] = jnp.where(pl.program_id(0) == 0, yn, y).astype(o_ref.dtype)


def _compress(xc, w1, b1, w2, pos, kg):
    _, g, nc, k = xc.shape
    return pl.pallas_call(
        _compress_kernel,
        grid=(2, g),
        in_specs=[
            pl.BlockSpec((1, 1, nc, k), lambda s, i: (s, i, 0, 0)),
            pl.BlockSpec((1, 2 * k, DH), lambda s, i: (s, 0, 0)),
            pl.BlockSpec((1, 1, DH), lambda s, i: (s, 0, 0)),
            pl.BlockSpec((1, DH, DH), lambda s, i: (s, 0, 0)),
            pl.BlockSpec((1, 1, 2 * k), lambda s, i: (s, 0, 0)),
            pl.BlockSpec((1, DH), lambda s, i: (0, 0)),
        ],
        out_specs=pl.BlockSpec((1, 1, nc, DH), lambda s, i: (s, i, 0, 0)),
        out_shape=jax.ShapeDtypeStruct((2, g, nc, DH), BF16),
        compiler_params=_params("parallel", "parallel", vmem=20 * MIB),
        name="nsa_compress",
    )(xc, w1, b1, w2, pos, kg)


def _stack_heads(qb):
    return jnp.concatenate([qb[:, r * DH:(r + 1) * DH] for r in range(NSA_REP)], axis=0)


def _cmp_kernel(q_ref, kc_ref, vc_ref, ovt_ref, o_ref, mn_ref, imp_scr, *, tq, ktop, col_steps):
    qi = pl.program_id(0)
    gw = NSA_REP * DH

    def attend(ncols, nb):
        for g in range(NSA_GROUPS):
            q4 = _stack_heads(q_ref[:, g * gw:(g + 1) * gw])
            s = lax.dot_general(q4, kc_ref[g, 0:ncols, :], (((1,), (1,)), ((), ())),
                                preferred_element_type=F32)
            row = lax.broadcasted_iota(jnp.int32, s.shape, 0)
            col = lax.broadcasted_iota(jnp.int32, s.shape, 1)
            t = qi * tq + (row & (tq - 1))
            s = jnp.where(col * CMP_STRIDE + (CMP_LEN - 1) <= t, s, -jnp.inf)
            mx = jnp.max(s, axis=-1, keepdims=True)
            mx = jnp.where(jnp.abs(mx) < jnp.inf, mx, 0.0)
            p = jnp.exp2(s - mx)
            p = p / jnp.maximum(jnp.sum(p, axis=-1, keepdims=True), 1e-30)
            o = jnp.dot(p.astype(BF16), vc_ref[g, 0:ncols, :], preferred_element_type=F32)
            for r in range(NSA_REP):
                o_ref[:, g * gw + r * DH:g * gw + (r + 1) * DH] = o[r * tq:(r + 1) * tq].astype(o_ref.dtype)
            ps = p[0:tq]
            for r in range(1, NSA_REP):
                ps = ps + p[r * tq:(r + 1) * tq]
            imp_scr[g, 0:nb, :] = lax.dot_general(ovt_ref[0:nb, 0:ncols], ps.astype(BF16),
                                                  (((1,), (1,)), ((), ())), preferred_element_type=F32)

    def select(nb):
        nselp = imp_scr.shape[1]
        jj = lax.broadcasted_iota(jnp.int32, (nb, tq), 0)
        cur = (qi * tq + lax.broadcasted_iota(jnp.int32, (nb, tq), 1)) // SEL_BLOCK
        forced = (jj == 0) | (jj == cur) | (jj == cur - 1)
        free = (jj >= 1) & (jj <= cur - 2)
        jjf = jj.astype(F32)
        scores = [jnp.where(free, imp_scr[g, 0:nb, :], -jnp.inf) for g in range(NSA_GROUPS)]
        for _ in range(ktop - 3):
            for g in range(NSA_GROUPS):
                top = jnp.max(scores[g], axis=0, keepdims=True)
                first = jnp.min(jnp.where(scores[g] == top, jjf, 1e9), axis=0, keepdims=True)
                scores[g] = jnp.where(jjf == first, -jnp.inf, scores[g])
        for g in range(NSA_GROUPS):
            picked = forced | (free & (scores[g] == -jnp.inf))
            notsel = jnp.where(picked, 0.0, 1.0)
            if nb < nselp:
                notsel = jnp.concatenate([notsel, jnp.ones((nselp - nb, tq), F32)], axis=0)
            mn_ref[g] = notsel.T.astype(mn_ref.dtype)

    def variant(ncols):
        nb = min(imp_scr.shape[1], -(-(ncols * CMP_STRIDE // SEL_BLOCK) // SUBLANE) * SUBLANE)
        attend(ncols, nb)
        select(nb)

    needed = ((qi + 1) * tq - CMP_LEN) // CMP_STRIDE + 1
    lo = 0
    for ncols in col_steps:
        pl.when((needed > lo) & (needed <= ncols))(functools.partial(variant, ncols))
        lo = ncols


def _cmp_attention(q, kc, vc, overlap_t, *, tq):
    s = q.shape[0]
    g, nc, _ = kc.shape
    nselp = overlap_t.shape[0]
    ktop = min(N_SELECT, s // SEL_BLOCK)
    assert ktop >= 3
    gw = NSA_REP * DH
    col_steps = tuple(range(2 * LANE, nc + 1, 2 * LANE)) if nc % (2 * LANE) == 0 else (nc,)
    return pl.pallas_call(
        functools.partial(_cmp_kernel, tq=tq, ktop=ktop, col_steps=col_steps),
        grid=(s // tq,),
        in_specs=[
            pl.BlockSpec((tq, g * gw), lambda i: (i, 0)),
            pl.BlockSpec((g, nc, DH), lambda i: (0, 0, 0)),
            pl.BlockSpec((g, nc, DH), lambda i: (0, 0, 0)),
            pl.BlockSpec((nselp, nc), lambda i: (0, 0)),
        ],
        out_specs=[
            pl.BlockSpec((tq, g * gw), lambda i: (i, 0)),
            pl.BlockSpec((g, tq, nselp), lambda i: (0, i, 0)),
        ],
        out_shape=[
            jax.ShapeDtypeStruct((s, g * gw), BF16),
            jax.ShapeDtypeStruct((g, s, nselp), BF16),
        ],
        scratch_shapes=[pltpu.VMEM((g, nselp, tq), F32)],
        compiler_params=_params("parallel", vmem=16 * MIB),
        name="nsa_cmp_select",
    )(q, kc, vc, overlap_t)


def _sel_kernel(q_ref, mn_ref, k_ref, v_ref, e_ref, o_ref, ka_scr, va_scr, qa_scr, s_scr, m_scr, acc_scr,
                *, tq, tk):
    qi = pl.program_id(1)
    nhalf = qa_scr.shape[0]
    nper = e_ref.shape[0]

    @pl.when(qi == 0)
    def _():
        for c in range(ka_scr.shape[0] // nper):
            rows = slice(c * nper, (c + 1) * nper)
            ka_scr[rows, 0:DH] = k_ref[rows, :]
            ka_scr[rows, DH:2 * DH] = e_ref[...]
            va_scr[rows, 0:DH] = v_ref[rows, :]
            va_scr[rows, DH:2 * DH] = jnp.ones((nper, DH), BF16)

    q4 = _stack_heads(q_ref[...])
    mn = mn_ref[0]
    for hf in range(nhalf):
        part = mn[:, hf * LANE:(hf + 1) * LANE]
        qa_scr[hf, :, 0:DH] = q4
        qa_scr[hf, :, DH:2 * DH] = jnp.concatenate([part] * NSA_REP, axis=0)
    m_scr[...] = jnp.full_like(m_scr, -jnp.inf)
    acc_scr[...] = jnp.zeros_like(acc_scr)

    def scores(j, slot):
        k0 = pl.multiple_of(j * tk, tk)
        s_scr[slot] = lax.dot_general(qa_scr[k0 // (SEL_BLOCK * LANE)], ka_scr[pl.ds(k0, tk), :],
                                      (((1,), (1,)), ((), ())), preferred_element_type=F32)

    def accumulate(j, slot, masked):
        k0 = pl.multiple_of(j * tk, tk)
        s = s_scr[slot]
        if masked:
            row = lax.broadcasted_iota(jnp.int32, s.shape, 0)
            col = lax.broadcasted_iota(jnp.int32, s.shape, 1)
            s = jnp.where(k0 + col <= qi * tq + (row & (tq - 1)), s, -jnp.inf)
        m_old = m_scr[...]
        m_new = jnp.maximum(m_old, jnp.max(s, axis=-1, keepdims=True))
        p = jnp.exp2(s - m_new).astype(BF16)
        acc_scr[...] = (jnp.exp2(m_old - m_new) * acc_scr[...]
                        + jnp.dot(p, va_scr[pl.ds(k0, tk), :], preferred_element_type=F32))
        m_scr[...] = m_new

    n = (qi * tq + tq - 1) // tk + 1
    unroll = 4
    ntrip = (n - 1) // unroll
    scores(0, 0)

    def trip(i, c):
        for u in range(unroll):
            scores(unroll * i + u + 1, (u + 1) % 2)
            accumulate(unroll * i + u, u % 2, False)
        return c

    lax.fori_loop(0, ntrip, trip, 0)

    first = ntrip * unroll
    for rest in range(1, unroll + 1):
        @pl.when(n - first == rest)
        def _(rest=rest):
            for u in range(rest):
                if u + 1 < rest:
                    scores(first + u + 1, (u + 1) % 2)
                accumulate(first + u, u % 2, u + 1 == rest)

    acc = acc_scr[...]
    o = acc[:, 0:DH] / jnp.maximum(acc[:, DH:2 * DH], 1e-30)
    for r in range(NSA_REP):
        o_ref[:, r * DH:(r + 1) * DH] = o[r * tq:(r + 1) * tq].astype(o_ref.dtype)


def _sel_attention(q, notsel, kv_arr, expand, *, k_col, v_col, tq, tk):
    s = q.shape[0]
    g = notsel.shape[0]
    nselp = notsel.shape[2]
    gw = NSA_REP * DH
    nper = expand.shape[0]
    assert s % nper == 0 and nper % tk == 0 and tk % tq == 0 and s % tq == 0
    once = pl.Buffered(1)
    return pl.pallas_call(
        functools.partial(_sel_kernel, tq=tq, tk=tk),
        grid=(g, s // tq),
        in_specs=[
            pl.BlockSpec((tq, gw), lambda gi, i: (i, gi)),
            pl.BlockSpec((1, tq, nselp), lambda gi, i: (gi, i, 0)),
            pl.BlockSpec((s, DH), lambda gi, i: (0, k_col + gi), pipeline_mode=once),
            pl.BlockSpec((s, DH), lambda gi, i: (0, v_col + gi), pipeline_mode=once),
            pl.BlockSpec((nper, LANE), lambda gi, i: (0, 0), pipeline_mode=once),
        ],
        out_specs=pl.BlockSpec((tq, gw), lambda gi, i: (i, gi)),
        out_shape=jax.ShapeDtypeStruct((s, g * gw), BF16),
        scratch_shapes=[
            pltpu.VMEM((s, 2 * DH), BF16),
            pltpu.VMEM((s, 2 * DH), BF16),
            pltpu.VMEM((nselp // LANE, NSA_REP * tq, 2 * DH), BF16),
            pltpu.VMEM((2, NSA_REP * tq, tk), F32),
            pltpu.VMEM((NSA_REP * tq, 1), F32),
            pltpu.VMEM((NSA_REP * tq, 2 * DH), F32),
        ],
        compiler_params=_params("arbitrary", "arbitrary"),
        name="nsa_sel_attention",
    )(q, notsel, kv_arr, kv_arr, expand)


def _win_kernel(q_ref, *refs, tq, nback):
    nblk = nback + 1
    k_refs = refs[:nblk]
    v_refs = refs[nblk:2 * nblk]
    band_ref, oc_ref, os_ref, gt_ref, o_ref = refs[2 * nblk:]
    qi = pl.program_id(0)
    gw = NSA_REP * DH
    band = band_ref[...]
    col = lax.broadcasted_iota(jnp.int32, band.shape, 1)
    band = jnp.where(col >= (nback - qi) * tq, band, -jnp.inf)
    for g in range(NSA_GROUPS):
        q4 = _stack_heads(q_ref[:, g * gw:(g + 1) * gw])
        kc = jnp.concatenate([r[:, g * DH:(g + 1) * DH] for r in k_refs], axis=0)
        vc = jnp.concatenate([r[:, g * DH:(g + 1) * DH] for r in v_refs], axis=0)
        s = lax.dot_general(q4, kc, (((1,), (1,)), ((), ())), preferred_element_type=F32) + band
        mx = jnp.max(s, axis=-1, keepdims=True)
        mx = jnp.where(jnp.abs(mx) < jnp.inf, mx, 0.0)
        p = jnp.exp2(s - mx)
        p = p / jnp.maximum(jnp.sum(p, axis=-1, keepdims=True), 1e-30)
        ow = jnp.dot(p.astype(BF16), vc, preferred_element_type=F32)
        gates = gt_ref[...]
        for r in range(NSA_REP):
            sl = slice(g * gw + r * DH, g * gw + (r + 1) * DH)
            c0 = 3 * (g * NSA_REP + r)
            out = (gates[:, c0:c0 + 1] * oc_ref[:, sl].astype(F32)
                   + gates[:, c0 + 1:c0 + 2] * os_ref[:, sl].astype(F32)
                   + gates[:, c0 + 2:c0 + 3] * ow[r * tq:(r + 1) * tq])
            o_ref[:, sl] = out.astype(o_ref.dtype)


def _win_attention(q, kv_arr, o_cmp, o_sel, gates, *, k_blk, v_blk, tq):
    s = q.shape[0]
    g = NSA_GROUPS
    gw = NSA_REP * DH
    nback = WINDOW // tq
    assert nback * tq == WINDOW
    qq = jnp.arange(NSA_REP * tq)[:, None] % tq
    kk = jnp.arange((nback + 1) * tq)[None, :]
    band = jnp.where((kk > qq) & (kk <= qq + WINDOW), 0.0, -jnp.inf).astype(F32)

    def kvmap(blk, b):
        def f(i):
            return (jnp.maximum(i - nback + b, 0), blk)
        return f

    k_specs = [pl.BlockSpec((tq, g * DH), kvmap(k_blk, b)) for b in range(nback + 1)]
    v_specs = [pl.BlockSpec((tq, g * DH), kvmap(v_blk, b)) for b in range(nback + 1)]
    wide = pl.BlockSpec((tq, g * gw), lambda i: (i, 0))
    return pl.pallas_call(
        functools.partial(_win_kernel, tq=tq, nback=nback),
        grid=(s // tq,),
        in_specs=([wide] + k_specs + v_specs
                  + [pl.BlockSpec(band.shape, lambda i: (0, 0)), wide, wide,
                     pl.BlockSpec((tq, LANE), lambda i: (i, 0))]),
        out_specs=wide,
        out_shape=jax.ShapeDtypeStruct((s, g * gw), BF16),
        compiler_params=_params("parallel", vmem=28 * MIB),
        name="nsa_win_combine",
    )(q, *([kv_arr] * (2 * (nback + 1))), band, o_cmp, o_sel, gates)


def _mlstm_kernel(q_ref, k_ref, v_ref, o_ref, gt_ref, og_ref, y_ref, c_scr, n_scr, m_scr):
    L = q_ref.shape[0]

    @pl.when(pl.program_id(0) == 0)
    def _():
        c_scr[...] = jnp.zeros_like(c_scr)
        n_scr[...] = jnp.zeros_like(n_scr)
        m_scr[...] = jnp.full_like(m_scr, NEG_INIT)

    ri = lax.broadcasted_iota(jnp.int32, (L, L), 0)
    ci = lax.broadcasted_iota(jnp.int32, (L, L), 1)
    eye = ri == ci
    tril = ci <= ri
    triu = ri <= ci
    gates = gt_ref[...]

    def to_row(col):
        return jnp.sum(jnp.where(eye, col, 0.0), axis=0, keepdims=True)

    for h in range(ML_HEADS):
        qh = q_ref[:, h * ML_DK:(h + 1) * ML_DK]
        kh = k_ref[:, h * ML_DK:(h + 1) * ML_DK]
        vh = v_ref[:, h * ML_DV:(h + 1) * ML_DV]
        ig_col = gates[:, h:h + 1]
        fg_col = gates[:, ML_HEADS + h:ML_HEADS + h + 1]
        lf_col = jnp.minimum(fg_col, 0.0) - jnp.log(1.0 + jnp.exp(-jnp.abs(fg_col)))
        lf_row = to_row(lf_col)
        ig_row = to_row(ig_col)
        b_col = jnp.sum(jnp.where(tril, lf_row, 0.0), axis=1, keepdims=True)
        b_row = jnp.sum(jnp.where(triu, lf_col, 0.0), axis=0, keepdims=True)
        m_old = m_scr[h:h + 1, 0:1]
        dmat = jnp.where(tril, b_col - b_row + ig_row, -jnp.inf)
        m_inter = b_col + m_old
        m_t = jnp.maximum(m_inter, jnp.max(dmat, axis=1, keepdims=True))
        qk = lax.dot_general(qh, kh, (((1,), (1,)), ((), ())), preferred_element_type=F32)
        a = jnp.exp(dmat - m_t) * qk
        dec = jnp.exp(m_inter - m_t)
        c_old = c_scr[h]
        n_old = n_scr[h:h + 1, :]
        num = (jnp.dot(a.astype(BF16), vh, preferred_element_type=F32)
               + dec * jnp.dot(qh, c_old.astype(BF16), preferred_element_type=F32))
        qn = jnp.sum(qh.astype(F32) * n_old, axis=1, keepdims=True)
        den = jnp.sum(a, axis=1, keepdims=True) + dec * qn
        hx = num / jnp.maximum(jnp.abs(den), jnp.exp(-m_t))

        b_last = b_col[L - 1:L, :]
        g_col = b_last - b_col + ig_col
        m_new = jnp.maximum(b_last + m_old, jnp.max(g_col, axis=0, keepdims=True))
        w_col = jnp.exp(g_col - m_new)
        cd = jnp.exp(b_last + m_old - m_new)
        kw = kh.astype(F32) * w_col
        c_scr[h] = cd * c_old + lax.dot_general(kw.astype(BF16), vh, (((0,), (0,)), ((), ())),
                                                preferred_element_type=F32)
        n_scr[h:h + 1, :] = cd * n_old + jnp.sum(kw, axis=0, keepdims=True)
        m_scr[h:h + 1, :] = jnp.broadcast_to(m_new, (1, LANE))

        sl = slice(h * ML_DV, (h + 1) * ML_DV)
        hn = _rms(hx, og_ref[:, sl])
        y_ref[:, sl] = (jax.nn.sigmoid(o_ref[:, sl].astype(F32)) * hn).astype(y_ref.dtype)


def _mlstm(qkvo, gates, out_g):
    s = qkvo.shape[0]
    L = min(ML_CHUNK, s)
    assert s % L == 0
    wq = ML_HEADS * ML_DK
    wv = ML_HEADS * ML_DV
    return pl.pallas_call(
        _mlstm_kernel,
        grid=(s // L,),
        in_specs=[
            pl.BlockSpec((L, wq), lambda c: (c, 0)),
            pl.BlockSpec((L, wq), lambda c: (c, 1)),
            pl.BlockSpec((L, wv), lambda c: (c, 1)),
            pl.BlockSpec((L, wv), lambda c: (c, 2)),
            pl.BlockSpec((L, LANE), lambda c: (c, 0)),
            pl.BlockSpec((1, wv), lambda c: (0, 0)),
        ],
        out_specs=pl.BlockSpec((L, wv), lambda c: (c, 0)),
        out_shape=jax.ShapeDtypeStruct((s, wv), BF16),
        scratch_shapes=[
            pltpu.VMEM((ML_HEADS, ML_DK, ML_DV), F32),
            pltpu.VMEM((ML_HEADS, ML_DK), F32),
            pltpu.VMEM((ML_HEADS, LANE), F32),
        ],
        compiler_params=_params("arbitrary", vmem=28 * MIB),
        name="mlstm_scan",
    )(qkvo, qkvo, qkvo, qkvo, gates, out_g)


class _Tiles(NamedTuple):
    proj: int
    out: int
    ffn: int
    ffn_cols: int
    proj_cols: int
    cmp: int
    sel: int
    sel_keys: int
    win: int


def _tiles(s):
    return _Tiles(proj=min(1024, s), out=min(512, s), ffn=min(1024, s), ffn_cols=512, proj_cols=1024,
                  cmp=min(128, s), sel=min(256, s), sel_keys=min(1024, s), win=min(256, s))


def _pad_cols(a, n):
    return jnp.pad(a, ((0, 0), (0, n - a.shape[1])))


def _nsa_layer(x, norm_g, w_main, w_gate, b_gate, q_g, k_g, cmp_pos, cmp_w1, cmp_b1, cmp_w2, w_out, layer, t):
    s, d = x.shape
    qd = NSA_HEADS * DH
    kvd = NSA_GROUPS * DH
    norm_g = norm_g.reshape(1, d)

    ones = jnp.ones((kvd,), F32)
    gain = jnp.concatenate([jnp.tile(q_g, NSA_HEADS) * (DH ** -0.5 * LOG2E), ones, ones,
                            jnp.tile(k_g[1], NSA_GROUPS), ones, jnp.tile(k_g[2], NSA_GROUPS), ones])
    flag = jnp.concatenate([jnp.ones((qd,), F32), 0 * ones, 0 * ones, ones, 0 * ones, ones, 0 * ones])
    w_gate = _pad_cols(w_gate, LANE).astype(BF16)
    bias = _pad_cols(b_gate.reshape(1, -1), LANE)
    tn = 2 * kvd
    assert qd % tn == 0
    proj, gates, xc = _proj(x, norm_g, w_main, layer, gain.reshape(1, -1), flag.reshape(1, -1), w_gate, bias,
                            mode="headnorm", gate_mode="sigmoid", tm=t.proj, tn=tn, name="nsa_proj",
                            chunk_step=qd // tn)
    nc = s // CMP_STRIDE
    xc = xc.reshape(2, NSA_GROUPS, nc, CMP_STRIDE * DH)
    kvc = _compress(xc, cmp_w1.astype(BF16), cmp_b1.reshape(2, 1, DH), cmp_w2.astype(BF16),
                    cmp_pos.reshape(2, 1, CMP_LEN * DH).astype(BF16), k_g[0].reshape(1, DH))

    n_sel = s // SEL_BLOCK
    nselp = -(-n_sel // LANE) * LANE
    cstart = jnp.arange(nc) * CMP_STRIDE
    sstart = jnp.arange(nselp) * SEL_BLOCK
    overlap_t = ((cstart[None, :] < sstart[:, None] + SEL_BLOCK)
                 & (cstart[None, :] + CMP_LEN > sstart[:, None])
                 & (jnp.arange(nselp)[:, None] < n_sel)
                 & (jnp.arange(nc)[None, :] < nc - 1)).astype(BF16)
    o_cmp, notsel = _cmp_attention(proj, kvc[0], kvc[1], overlap_t, tq=t.cmp)

    blk = jnp.arange(min(s, SEL_BLOCK * LANE)) // SEL_BLOCK
    expand = jnp.where(blk[:, None] == jnp.arange(LANE)[None, :], MASK_BIAS, 0.0).astype(BF16)
    col0 = qd // DH
    o_sel = _sel_attention(proj, notsel, proj, expand, k_col=col0 + 2 * NSA_GROUPS,
                           v_col=col0 + 3 * NSA_GROUPS, tq=t.sel, tk=t.sel_keys)
    mixed = _win_attention(proj, proj, o_cmp, o_sel, gates, k_blk=(qd + 4 * kvd) // kvd,
                           v_blk=(qd + 5 * kvd) // kvd, tq=t.win)
    return _matmul_res(mixed, w_out, layer, x, tm=t.out, name="nsa_out_proj")


def _mlstm_layer(x, norm_g, w_main, w_gate, b_if, out_g, w_out, layer, t):
    s, d = x.shape
    norm_g = norm_g.reshape(1, d)
    wq = ML_HEADS * ML_DK
    wv = ML_HEADS * ML_DV
    scale = jnp.concatenate([jnp.ones((wq,), F32), jnp.full((wq,), ML_DK ** -0.5, F32),
                             jnp.ones((2 * wv,), F32)]).reshape(1, -1)
    w_gate = _pad_cols(w_gate, LANE).astype(BF16)
    bias = _pad_cols(b_if.reshape(1, -1), LANE)
    qkvo, gates = _proj(x, norm_g, w_main, layer, scale, scale, w_gate, bias, mode="scale", gate_mode="bias",
                        tm=t.proj, tn=t.proj_cols, name="ml_proj")
    y = _mlstm(qkvo, gates, out_g.reshape(1, -1))
    return _matmul_res(y, w_out, layer, x, tm=t.out, name="ml_out_proj")


def _ffn_layer(x, norm_g, wg, wu, wd, layer, t):
    return _ffn(x, norm_g.reshape(1, -1), wg, wu, wd, layer, tm=t.ffn, tf=t.ffn_cols)


def kernel(x, norm_mix_g, norm_ffn_g, nsa_w_in, nsa_b_gate, nsa_q_norm_g, nsa_k_norm_g, nsa_cmp_pos,
           nsa_cmp_w1, nsa_cmp_b1, nsa_cmp_w2, nsa_w_out, ml_w_in, ml_b_if, ml_out_norm_g, ml_w_out,
           ffn_w_gate, ffn_w_up, ffn_w_down):
    b, s, d = x.shape
    depth = norm_mix_g.shape[0]
    t = _tiles(s)
    nsa_main = NSA_HEADS * DH + 6 * NSA_GROUPS * DH
    ml_main = 2 * ML_HEADS * ML_DK + 2 * ML_HEADS * ML_DV
    nsa_w_main, nsa_w_gate = nsa_w_in.astype(BF16), nsa_w_in[:, :, nsa_main:]
    ml_w_main, ml_w_gate = ml_w_in.astype(BF16), ml_w_in[:, :, ml_main:]
    nsa_w_out, ml_w_out = nsa_w_out.astype(BF16), ml_w_out.astype(BF16)
    ffn_w = (ffn_w_gate.astype(BF16), ffn_w_up.astype(BF16), ffn_w_down.astype(BF16))
    outs = []
    for bi in range(b):
        xb = x[bi]
        for i in range(depth):
            j = i // 2
            if i % 2 == 0:
                xb = _nsa_layer(xb, norm_mix_g[i], nsa_w_main, nsa_w_gate[j], nsa_b_gate[j], nsa_q_norm_g[j],
                                nsa_k_norm_g[j], nsa_cmp_pos[j], nsa_cmp_w1[j], nsa_cmp_b1[j],
                                nsa_cmp_w2[j], nsa_w_out, j, t)
            else:
                xb = _mlstm_layer(xb, norm_mix_g[i], ml_w_main, ml_w_gate[j], ml_b_if[j], ml_out_norm_g[j],
                                  ml_w_out, j, t)
            xb = _ffn_layer(xb, norm_ffn_g[i], *ffn_w, i, t)
        outs.append(xb)
    return jnp.stack(outs, axis=0)
```
